```python
import jax, jax.numpy as jnp
from jax import lax
import numpy as np

D_MODEL = 1024
BATCH = 32
SEQ = 2048
DEPTH = 2

CHUNK = 64
P_DIM = 256
SB_HEAD_DIM = 64
SB_WIDTH = 3 * D_MODEL // 8
SB_HEADS = SB_WIDTH // SB_HEAD_DIM
SB_QBLOCK = 128
ML_HEADS = 4
ML_WIDTH = 3 * D_MODEL // 8
ML_HEAD_DIM = ML_WIDTH // ML_HEADS
CONV_K = 4
SGU_WIDTH = D_MODEL - SB_WIDTH - ML_WIDTH
SGU_GROUPS = 4
SGU_GROUP_DIM = SGU_WIDTH // SGU_GROUPS
SGU_BLOCK = 128
IN_SPLITS = (SB_WIDTH, SB_WIDTH, SB_WIDTH, ML_WIDTH, ML_WIDTH, ML_WIDTH, ML_WIDTH, ML_HEADS, ML_HEADS, SGU_WIDTH, SGU_WIDTH)
N_IN = 3 * SB_WIDTH + 4 * ML_WIDTH + 2 * ML_HEADS + 2 * SGU_WIDTH
N_GROUPS = 4
EXPERTS_PER_GROUP = 8
N_EXPERTS = N_GROUPS * EXPERTS_PER_GROUP
TOP_K = 2
D_EXPERT = D_MODEL // 4
MOE_BLOCK = 512
EPS = 1e-6

kernel_name = "hymba_style_sb_mlstm_gmlp_hmoe_ple"

F32 = jnp.float32


def rms_norm(x, g):
    x32 = x.astype(F32)
    return x32 * lax.rsqrt(jnp.mean(x32 * x32, axis=-1, keepdims=True) + EPS) * g.astype(F32)


def split_cols(z, sizes):
    idx = np.cumsum(sizes)[:-1].tolist()
    return jnp.split(z, idx, axis=-1)


def causal_depthwise_conv(x, w, b):
    y = lax.conv_general_dilated(x.astype(F32), w.astype(F32)[:, None, :], window_strides=(1,),
                                 padding=[(CONV_K - 1, 0)], dimension_numbers=('NWC', 'WIO', 'NWC'),
                                 feature_group_count=x.shape[-1])
    return y + b.astype(F32)


def stick_breaking_attention(q, k, v):
    S = q.shape[1]
    scale = q.shape[-1] ** -0.5
    outs = []
    for n in range(S // SB_QBLOCK):
        t0 = n * SB_QBLOCK
        kend = t0 + SB_QBLOCK
        z = jnp.einsum('bthd,bshd->bhts', q[:, t0:kend], k[:, :kend]) * scale
        tpos = t0 + jnp.arange(SB_QBLOCK)
        spos = jnp.arange(kend)
        mask = spos[None, :] < tpos[:, None]
        log_1m = jnp.where(mask, -jax.nn.softplus(z), 0.0)
        after = lax.cumsum(log_1m, axis=3, reverse=True) - log_1m
        a = jnp.where(mask, jnp.exp(jax.nn.log_sigmoid(z) + after), 0.0)
        outs.append(jnp.einsum('bhts,bshd->bthd', a, v[:, :kend]))
    return jnp.concatenate(outs, axis=1)


def mlstm_chunkwise(q, k, v, i_pre, f_pre):
    Bn, S, H, d = q.shape
    N, L = S // CHUNK, CHUNK
    q = q.reshape(Bn, N, L, H, d)
    k = k.reshape(Bn, N, L, H, d) * (d ** -0.5)
    v = v.reshape(Bn, N, L, H, d)
    ig = i_pre.reshape(Bn, N, L, H)
    bcum = jnp.cumsum(jax.nn.log_sigmoid(f_pre.reshape(Bn, N, L, H)), axis=2)
    btot = bcum[:, :, -1]
    a = btot[:, :, None] - bcum + ig

    def step(carry, xs):
        c, nvec, m = carry
        k_c, v_c, a_c, bt_c = xs
        m_new = jnp.maximum(bt_c + m, a_c.max(axis=1))
        decay = jnp.exp(bt_c + m - m_new)
        w = jnp.exp(a_c - m_new[:, None])
        c_new = decay[..., None, None] * c + jnp.einsum('blh,blhd,blhe->bhde', w, k_c, v_c)
        n_new = decay[..., None] * nvec + jnp.einsum('blh,blhd->bhd', w, k_c)
        return (c_new, n_new, m_new), (c, nvec, m)

    init = (jnp.zeros((Bn, H, d, d), F32), jnp.zeros((Bn, H, d), F32), jnp.zeros((Bn, H), F32))
    xs = (jnp.moveaxis(k, 1, 0), jnp.moveaxis(v, 1, 0), jnp.moveaxis(a, 1, 0), jnp.moveaxis(btot, 1, 0))
    _, (c_in, n_in, m_in) = lax.scan(step, init, xs)
    c_in = jnp.moveaxis(c_in, 0, 1)
    n_in = jnp.moveaxis(n_in, 0, 1)
    m_in = jnp.moveaxis(m_in, 0, 1)

    bT = jnp.moveaxis(bcum, 2, -1)
    igT = jnp.moveaxis(ig, 2, -1)
    causal = jnp.tril(jnp.ones((L, L), dtype=bool))
    dmat = jnp.where(causal, bT[..., :, None] - bT[..., None, :] + igT[..., None, :], -jnp.inf)
    m_inter = bT + m_in[..., None]
    m_t = jnp.maximum(m_inter, dmat.max(axis=-1))
    wd = jnp.where(causal, jnp.exp(dmat - m_t[..., None]), 0.0)
    s = jnp.einsum('bnthd,bnshd->bnhts', q, k) * wd
    inter = jnp.exp(m_inter - m_t)
    num = jnp.einsum('bnhts,bnshd->bnthd', s, v) + jnp.einsum('bnhl,bnlhd,bnhde->bnlhe', inter, q, c_in)
    den = s.sum(axis=-1) + inter * jnp.einsum('bnlhd,bnhd->bnhl', q, n_in)
    den = jnp.maximum(jnp.abs(den), jnp.exp(-m_t))
    h = num / jnp.moveaxis(den, 2, -1)[..., None]
    return h.reshape(Bn, S, H, d)


def spatial_gating(u, v, ln_g, ln_b, w_s, b_s):
    Bn, S, _ = u.shape
    v = v.reshape(Bn, S, SGU_GROUPS, SGU_GROUP_DIM)
    mu = v.mean(axis=-1, keepdims=True)
    var = jnp.mean(jnp.square(v - mu), axis=-1, keepdims=True)
    v = (v - mu) * lax.rsqrt(var + EPS) * ln_g.reshape(SGU_GROUPS, SGU_GROUP_DIM) + ln_b.reshape(SGU_GROUPS, SGU_GROUP_DIM)
    v = v.reshape(Bn, S // SGU_BLOCK, SGU_BLOCK, SGU_GROUPS, SGU_GROUP_DIM)
    cpos = jnp.arange(SGU_BLOCK) // CHUNK
    mask = cpos[None, :] <= cpos[:, None]
    w = jnp.where(mask[None], w_s.astype(F32), 0.0)
    mixed = jnp.einsum('gts,bnsgc->bntgc', w, v) + b_s.astype(F32).T[:, :, None]
    return u * mixed.reshape(Bn, S, SGU_WIDTH)


def routed_experts(xt, e_idx, e_w, w_gate, w_up, w_down):
    T, D = xt.shape
    M = T * TOP_K
    flat_e = e_idx.reshape(M)
    order = jnp.argsort(flat_e)
    sorted_e = flat_e[order]
    sorted_tok = order // TOP_K
    sorted_w = e_w.reshape(M)[order]
    sizes = jnp.bincount(flat_e, length=N_EXPERTS)
    padded = (sizes + MOE_BLOCK - 1) // MOE_BLOCK * MOE_BLOCK
    start = jnp.cumsum(sizes) - sizes
    pend = jnp.cumsum(padded)
    pstart = pend - padded
    dest = pstart[sorted_e] + jnp.arange(M) - start[sorted_e]
    n_blocks = -(-(M + N_EXPERTS * (MOE_BLOCK - 1)) // MOE_BLOCK)
    rows = n_blocks * MOE_BLOCK
    row_tok = jnp.zeros((rows,), jnp.int32).at[dest].set(sorted_tok.astype(jnp.int32))
    row_w = jnp.zeros((rows,), F32).at[dest].set(sorted_w)
    block_e = jnp.minimum(jnp.searchsorted(pend, jnp.arange(n_blocks) * MOE_BLOCK, side='right'), N_EXPERTS - 1)

    def block_ffn(args):
        tok, wt, e = args
        xb = xt[tok]
        hb = jax.nn.silu(xb @ w_gate[e]) * (xb @ w_up[e])
        return (hb @ w_down[e]).astype(F32) * wt[:, None]

    yb = lax.map(block_ffn, (row_tok.reshape(n_blocks, MOE_BLOCK), row_w.reshape(n_blocks, MOE_BLOCK), block_e))
    return jnp.zeros((T, D), F32).at[row_tok].add(yb.reshape(rows, D))


def hierarchical_moe(h, rg_w, rg_b, re_w, re_b, w_gate, w_up, w_down):
    Bn, S, D = h.shape
    xt = h.reshape(Bn * S, D)
    g_logits = (xt @ rg_w + rg_b).astype(F32)
    g_prob = jax.nn.softmax(g_logits, axis=-1)
    g_top = jnp.argmax(g_logits, axis=-1)
    g_w = jnp.take_along_axis(g_prob, g_top[:, None], axis=-1)[:, 0]
    e_logits = (xt @ re_w + re_b).astype(F32).reshape(-1, N_GROUPS, EXPERTS_PER_GROUP)
    e_in = jnp.take_along_axis(e_logits, g_top[:, None, None], axis=1)[:, 0]
    e_val, e_loc = lax.top_k(e_in, TOP_K)
    e_w = jax.nn.softmax(e_val, axis=-1) * g_w[:, None]
    e_idx = g_top[:, None].astype(jnp.int32) * EXPERTS_PER_GROUP + e_loc.astype(jnp.int32)
    y = routed_experts(xt, e_idx, e_w, w_gate, w_up, w_down)
    return y.reshape(Bn, S, D)


def setup_inputs(seed: int = 0) -> dict:
    key = jax.random.key(seed)
    ks = jax.random.split(key, 32)

    def nrm(k, shape, scale):
        return jax.random.normal(k, shape, F32) * scale

    def gain(k, shape):
        return 1.0 + 0.05 * jax.random.normal(k, shape, F32)

    D = D_MODEL
    return {
        "x": nrm(ks[0], (BATCH, SEQ, D), 1.0),
        "p": nrm(ks[1], (DEPTH, BATCH, SEQ, P_DIM), 1.0),
        "norm1_g": gain(ks[2], (DEPTH, D)),
        "w_in": nrm(ks[3], (DEPTH, D, N_IN), D ** -0.5),
        "conv_w": nrm(ks[4], (DEPTH, CONV_K, 2 * ML_WIDTH), CONV_K ** -0.5),
        "conv_b": nrm(ks[5], (DEPTH, 2 * ML_WIDTH), 0.01),
        "igate_b": nrm(ks[6], (DEPTH, ML_HEADS), 0.1),
        "fgate_b": jnp.linspace(3.0, 6.0, ML_HEADS, dtype=F32)[None, :] + nrm(ks[7], (DEPTH, ML_HEADS), 0.1),
        "mnorm_g": gain(ks[8], (DEPTH, ML_WIDTH)),
        "sb_out_g": gain(ks[9], (DEPTH, SB_WIDTH)),
        "sgu_ln_g": gain(ks[10], (DEPTH, SGU_WIDTH)),
        "sgu_ln_b": nrm(ks[11], (DEPTH, SGU_WIDTH), 0.01),
        "sgu_w": nrm(ks[12], (DEPTH, SGU_GROUPS, SGU_BLOCK, SGU_BLOCK), SGU_BLOCK ** -0.5),
        "sgu_b": gain(ks[13], (DEPTH, SGU_GROUPS, SGU_BLOCK)),
        "sgu_out_g": gain(ks[14], (DEPTH, SGU_WIDTH)),
        "w_out": nrm(ks[15], (DEPTH, D, D), D ** -0.5),
        "norm2_g": gain(ks[16], (DEPTH, D)),
        "router_gw": nrm(ks[17], (DEPTH, D, N_GROUPS), D ** -0.5),
        "router_gb": nrm(ks[18], (DEPTH, N_GROUPS), 0.01),
        "router_ew": nrm(ks[19], (DEPTH, D, N_EXPERTS), D ** -0.5),
        "router_eb": nrm(ks[20], (DEPTH, N_EXPERTS), 0.01),
        "w_gate": nrm(ks[21], (DEPTH, N_EXPERTS, D, D_EXPERT), D ** -0.5),
        "w_up": nrm(ks[22], (DEPTH, N_EXPERTS, D, D_EXPERT), D ** -0.5),
        "w_down": nrm(ks[23], (DEPTH, N_EXPERTS, D_EXPERT, D), D_EXPERT ** -0.5),
        "ple_norm_g": gain(ks[24], (DEPTH, D)),
        "ple_gate_w": nrm(ks[25], (DEPTH, D, D), D ** -0.5),
        "ple_proj_w": nrm(ks[26], (DEPTH, P_DIM, D), P_DIM ** -0.5),
        "final_g": gain(ks[27], (D,)),
    }


def reference(x, p, norm1_g, w_in, conv_w, conv_b, igate_b, fgate_b, mnorm_g, sb_out_g,
              sgu_ln_g, sgu_ln_b, sgu_w, sgu_b, sgu_out_g, w_out, norm2_g,
              router_gw, router_gb, router_ew, router_eb, w_gate, w_up, w_down,
              ple_norm_g, ple_gate_w, ple_proj_w, final_g):
    Bn, S, D = x.shape
    h = x.astype(F32)
    for i in range(DEPTH):
        hn = rms_norm(h, norm1_g[i])
        z = hn @ w_in[i]
        qa, ka, va, qb, kb, vb, ob, ib, fb, uc, vc = split_cols(z, IN_SPLITS)
        ya = stick_breaking_attention(qa.reshape(Bn, S, SB_HEADS, SB_HEAD_DIM),
                                      ka.reshape(Bn, S, SB_HEADS, SB_HEAD_DIM),
                                      va.reshape(Bn, S, SB_HEADS, SB_HEAD_DIM)).reshape(Bn, S, SB_WIDTH)
        ya = rms_norm(ya, sb_out_g[i])
        qk = jax.nn.silu(causal_depthwise_conv(jnp.concatenate([qb, kb], axis=-1), conv_w[i], conv_b[i]))
        qb, kb = jnp.split(qk, 2, axis=-1)
        hb = mlstm_chunkwise(qb.reshape(Bn, S, ML_HEADS, ML_HEAD_DIM),
                             kb.reshape(Bn, S, ML_HEADS, ML_HEAD_DIM),
                             vb.astype(F32).reshape(Bn, S, ML_HEADS, ML_HEAD_DIM),
                             ib + igate_b[i], fb + fgate_b[i])
        hb = rms_norm(hb, mnorm_g[i].reshape(ML_HEADS, ML_HEAD_DIM)).reshape(Bn, S, ML_WIDTH)
        yb = hb * jax.nn.sigmoid(ob)
        yc = spatial_gating(jax.nn.gelu(uc), jax.nn.gelu(vc), sgu_ln_g[i], sgu_ln_b[i], sgu_w[i], sgu_b[i])
        yc = rms_norm(yc, sgu_out_g[i])
        h = h + jnp.concatenate([ya, yb, yc], axis=-1) @ w_out[i]
        h = h + hierarchical_moe(rms_norm(h, norm2_g[i]), router_gw[i], router_gb[i], router_ew[i], router_eb[i],
                                 w_gate[i], w_up[i], w_down[i])
        gate = jax.nn.sigmoid(rms_norm(h, ple_norm_g[i]) @ ple_gate_w[i])
        h = h + gate * (p[i] @ ple_proj_w[i]).astype(F32)
    return rms_norm(h, final_g).astype(x.dtype)
```

```python
import functools

import jax
import jax.numpy as jnp
import numpy as np
from jax import lax
from jax.experimental import pallas as pl
from jax.experimental.pallas import tpu as pltpu

F32 = jnp.float32
BF16 = jnp.bfloat16
I32 = jnp.int32

D_MODEL = 1024
P_DIM = 256
EPS = 1e-6
LANES = 128
SB_HEAD_DIM = 64
SB_WIDTH = 384
SB_BLOCK = 128
ML_HEADS = 4
ML_HEAD_DIM = 96
ML_WIDTH = 384
ML_PAD = ML_HEADS * LANES
ML_CHUNK = 128
CONV_K = 4
SGU_WIDTH = 256
SGU_GROUPS = 4
SGU_GROUP_DIM = 64
SGU_BLOCK = 128
STREAM_CHUNK = 64
N_GROUPS = 4
EXPERTS_PER_GROUP = 8
N_EXPERTS = 32
D_EXPERT = 256
ROUTER_ROWS = 40
CHUNK_ROWS = 16
EXPERT_BLOCK_CHUNKS = 32
EXPERT_BLOCK = CHUNK_ROWS * EXPERT_BLOCK_CHUNKS

COL_A = 0
COL_BQK = 3 * SB_WIDTH
COL_BVO = COL_BQK + 2 * ML_PAD
COL_C = COL_BVO + 2 * ML_PAD
COL_G = COL_C + 2 * SGU_WIDTH
N_Z = COL_G + LANES

VMEM_LIMIT = 56 * 1024 * 1024


def _cparams(sem, vmem=VMEM_LIMIT):
    return pltpu.CompilerParams(dimension_semantics=sem, vmem_limit_bytes=vmem)


def _rms(x, g):
    return x * lax.rsqrt(jnp.mean(x * x, axis=-1, keepdims=True) + EPS) * g


def _split_bf16(x):
    hi = x.astype(BF16)
    lo = (x - hi.astype(F32)).astype(BF16)
    return hi, lo


def _dot(a, b):
    return jnp.dot(a, b, preferred_element_type=F32)


def _dot_nt(a, b):
    return lax.dot_general(a, b, (((1,), (1,)), ((), ())), preferred_element_type=F32)


def _dot_tn(a, b):
    return lax.dot_general(a, b, (((0,), (0,)), ((), ())), preferred_element_type=F32)


def _dot_split_lhs(x, m):
    hi, lo = _split_bf16(x)
    return _dot(hi, m) + _dot(lo, m)


def _dot_split_rhs(m, x):
    hi, lo = _split_bf16(x)
    return _dot(m, hi) + _dot(m, lo)


def _in_proj_kernel(x_ref, g_ref, w_ref, za_ref, zqk_ref, zvo_ref, zc_ref, zg_ref):
    hn = _rms(x_ref[...], g_ref[...]).astype(BF16)
    za_ref[...] = _dot(hn, w_ref[:, COL_A:COL_BQK]).astype(BF16)
    zqk_ref[...] = _dot(hn, w_ref[:, COL_BQK:COL_BVO])
    zvo_ref[...] = _dot(hn, w_ref[:, COL_BVO:COL_C]).astype(BF16)
    zc_ref[...] = _dot(hn, w_ref[:, COL_C:COL_G]).astype(BF16)
    zg_ref[...] = _dot(hn, w_ref[:, COL_G:N_Z])


def _in_proj(h, g, w, tm):
    t = h.shape[0]
    row = lambda n: pl.BlockSpec((tm, n), lambda i: (i, 0))
    full = lambda a: pl.BlockSpec(a.shape, lambda i: (0,) * a.ndim)
    return pl.pallas_call(
        _in_proj_kernel,
        grid=(t // tm,),
        in_specs=[row(D_MODEL), full(g), full(w)],
        out_specs=[row(3 * SB_WIDTH), row(2 * ML_PAD), row(2 * ML_PAD), row(2 * SGU_WIDTH), row(LANES)],
        out_shape=[jax.ShapeDtypeStruct((t, 3 * SB_WIDTH), BF16),
                   jax.ShapeDtypeStruct((t, 2 * ML_PAD), F32),
                   jax.ShapeDtypeStruct((t, 2 * ML_PAD), BF16),
                   jax.ShapeDtypeStruct((t, 2 * SGU_WIDTH), BF16),
                   jax.ShapeDtypeStruct((t, LANES), F32)],
        compiler_params=_cparams(("parallel",)),
        name="in_proj",
    )(h, g, w)


def _sb_kernel(q_ref, k_ref, v_ref, o_ref, acc_ref, car_ref):
    n = pl.program_id(2)
    blk = SB_BLOCK
    q2 = q_ref[...]
    lane = lax.broadcasted_iota(I32, (blk, LANES), 1)
    row = lax.broadcasted_iota(I32, (blk, LANES), 0)
    zero = jnp.zeros_like(q2)
    qh = (jnp.where(lane < SB_HEAD_DIM, q2, zero), jnp.where(lane >= SB_HEAD_DIM, q2, zero))
    mj = lax.broadcasted_iota(I32, (blk, 2 * blk), 0)
    ms = lax.broadcasted_iota(I32, (blk, 2 * blk), 1)
    suffix = jnp.where((mj > ms) | (ms >= blk), 1.0, 0.0).astype(BF16)
    diag_mask = lane < row

    acc_ref[...] = jnp.zeros_like(acc_ref)
    car_ref[...] = jnp.zeros_like(car_ref)

    def block(j, mask):
        start = pl.multiple_of(j * blk, blk)
        kj = k_ref[pl.ds(start, blk), :]
        vj = v_ref[pl.ds(start, blk), :]
        for hh in range(2):
            z = _dot_nt(qh[hh], kj)
            log1m = -(jnp.maximum(z, 0.0) + jnp.log1p(jnp.exp(-jnp.abs(z))))
            if mask is not None:
                log1m = jnp.where(mask, log1m, 0.0)
            ca = _dot_split_lhs(log1m, suffix)
            after = ca[:, :blk] + car_ref[hh]
            a = jnp.exp(z + log1m + after)
            if mask is not None:
                a = jnp.where(mask, a, 0.0)
            acc_ref[hh] += _dot(a.astype(BF16), vj)
            car_ref[hh] += ca[:, blk:]

    block(n, diag_mask)

    def body(i, c):
        block(n - 1 - i, None)
        return c

    lax.fori_loop(0, n, body, 0)
    o_ref[...] = jnp.where(lane < SB_HEAD_DIM, acc_ref[0], acc_ref[1]).astype(o_ref.dtype)


def _sb_attn(za, batch, seq):
    za3 = za.reshape(batch, seq, 3 * SB_WIDTH)
    pairs = SB_WIDTH // LANES
    return pl.pallas_call(
        _sb_kernel,
        grid=(batch, pairs, seq // SB_BLOCK),
        in_specs=[pl.BlockSpec((None, SB_BLOCK, LANES), lambda b, p, n: (b, n, p)),
                  pl.BlockSpec((None, seq, LANES), lambda b, p, n: (b, 0, pairs + p)),
                  pl.BlockSpec((None, seq, LANES), lambda b, p, n: (b, 0, 2 * pairs + p))],
        out_specs=pl.BlockSpec((None, SB_BLOCK, LANES), lambda b, p, n: (b, n, p)),
        out_shape=jax.ShapeDtypeStruct((batch, seq, SB_WIDTH), BF16),
        scratch_shapes=[pltpu.VMEM((2, SB_BLOCK, LANES), F32),
                        pltpu.VMEM((2, SB_BLOCK, LANES), F32)],
        compiler_params=_cparams(("parallel", "parallel", "arbitrary")),
        name="sb_attn",
    )(za3, za3, za3)


def _mlstm_kernel(qk_ref, vo_ref, g_ref, cw_ref, cb_ref, gb_ref, mg_ref, o_ref,
                  q_s, k_s, c_s, n_s, m_s):
    seq = qk_ref.shape[0]
    L = ML_CHUNK
    cw = cw_ref[...]
    cb = cb_ref[...]
    rows = lax.broadcasted_iota(I32, (L, 2 * ML_PAD), 0)
    for c in range(seq // L):
        r0 = c * L
        y = qk_ref[r0:r0 + L, :] * cw[CONV_K - 1:CONV_K, :] + cb
        for d in range(1, CONV_K):
            if c == 0:
                xd = jnp.where(rows >= d, pltpu.roll(qk_ref[0:L, :], d, axis=0), 0.0)
            else:
                xd = qk_ref[r0 - d:r0 - d + L, :]
            y = y + xd * cw[CONV_K - 1 - d:CONV_K - d, :]
        y = y * jax.nn.sigmoid(y)
        q_s[r0:r0 + L, :] = y[:, :ML_PAD].astype(BF16)
        k_s[r0:r0 + L, :] = (y[:, ML_PAD:] * (ML_HEAD_DIM ** -0.5)).astype(BF16)

    c_s[...] = jnp.zeros_like(c_s)
    n_s[...] = jnp.zeros_like(n_s)
    m_s[...] = jnp.zeros_like(m_s)

    ti = lax.broadcasted_iota(I32, (L, L), 0)
    si = lax.broadcasted_iota(I32, (L, L), 1)
    causal = si <= ti
    tril = jnp.where(causal, 1.0, 0.0).astype(BF16)
    lane = lax.broadcasted_iota(I32, (L, LANES), 1)
    gb = gb_ref[...]
    mg = mg_ref[...]

    def chunk(c, carry):
        r0 = pl.multiple_of(c * L, L)
        g = g_ref[pl.ds(r0, L), :] + gb
        logf = jnp.minimum(g, 0.0) - jnp.log1p(jnp.exp(-jnp.abs(g)))
        gl = jnp.where(lane < ML_HEADS, g, logf)
        cum = _dot_split_rhs(tril, logf)
        gl_t = jnp.transpose(gl)
        cum_t = jnp.transpose(cum)
        for h in range(ML_HEADS):
            cs = slice(h * LANES, (h + 1) * LANES)
            qc = q_s[pl.ds(r0, L), cs]
            kc = k_s[pl.ds(r0, L), cs]
            vc = vo_ref[pl.ds(r0, L), cs]
            oc = vo_ref[pl.ds(r0, L), ML_PAD + h * LANES:ML_PAD + (h + 1) * LANES].astype(F32)
            icol = gl[:, h:h + 1]
            irow = gl_t[h:h + 1, :]
            bcol = cum[:, ML_HEADS + h:ML_HEADS + h + 1]
            brow = cum_t[ML_HEADS + h:ML_HEADS + h + 1, :]
            btot = bcol[L - 1:L, :]
            m_in = m_s[h][0:1, 0:1]
            cmat = c_s[h]
            nrow = n_s[h][0:1, :]
            dmat = jnp.where(causal, bcol - brow + irow, -jnp.inf)
            m_inter = bcol + m_in
            m_t = jnp.maximum(m_inter, jnp.max(dmat, axis=1, keepdims=True))
            wd = jnp.where(causal, jnp.exp(dmat - m_t), 0.0)
            s = _dot_nt(qc, kc) * wd
            inter = jnp.exp(m_inter - m_t)
            num = _dot(s.astype(BF16), vc) + inter * _dot(qc, cmat.astype(BF16))
            qn = jnp.sum(qc.astype(F32) * nrow, axis=1, keepdims=True)
            den = jnp.sum(s, axis=1, keepdims=True) + inter * qn
            den = jnp.maximum(jnp.abs(den), jnp.exp(-m_t))
            hh = num / den
            ms = jnp.sum(hh * hh, axis=1, keepdims=True) * (1.0 / ML_HEAD_DIM)
            hh = hh * lax.rsqrt(ms + EPS) * mg[:, cs]
            o_ref[pl.ds(r0, L), cs] = (hh * jax.nn.sigmoid(oc)).astype(o_ref.dtype)
            acol = btot - bcol + icol
            m_new = jnp.maximum(btot + m_in, jnp.max(acol, axis=0, keepdims=True))
            decay = jnp.exp(btot + m_in - m_new)
            kw = kc.astype(F32) * jnp.exp(acol - m_new)
            c_s[h] = decay * cmat + _dot_tn(kw.astype(BF16), vc)
            n_s[h] = jnp.broadcast_to(decay * nrow + jnp.sum(kw, axis=0, keepdims=True), (8, LANES))
            m_s[h] = jnp.broadcast_to(m_new, (8, LANES))
        return carry

    lax.fori_loop(0, seq // L, chunk, 0)


def _mlstm(zqk, zvo, zg, cw, cb, gb, mg, batch, seq):
    full = lambda a: pl.BlockSpec(a.shape, lambda b: (0,) * a.ndim)
    seq_blk = lambda n: pl.BlockSpec((None, seq, n), lambda b: (b, 0, 0))
    return pl.pallas_call(
        _mlstm_kernel,
        grid=(batch,),
        in_specs=[seq_blk(2 * ML_PAD), seq_blk(2 * ML_PAD), seq_blk(LANES),
                  full(cw), full(cb), full(gb), full(mg)],
        out_specs=seq_blk(ML_PAD),
        out_shape=jax.ShapeDtypeStruct((batch, seq, ML_PAD), BF16),
        scratch_shapes=[pltpu.VMEM((seq, ML_PAD), BF16), pltpu.VMEM((seq, ML_PAD), BF16),
                        pltpu.VMEM((ML_HEADS, LANES, LANES), F32),
                        pltpu.VMEM((ML_HEADS, 8, LANES), F32),
                        pltpu.VMEM((ML_HEADS, 8, LANES), F32)],
        compiler_params=_cparams(("parallel",)),
        name="mlstm",
    )(zqk.reshape(batch, seq, 2 * ML_PAD), zvo.reshape(batch, seq, 2 * ML_PAD),
      zg.reshape(batch, seq, LANES), cw, cb, gb, mg)


def _sgu_kernel(z_ref, lg_ref, lb_ref, w_ref, b_ref, og_ref, o_ref):
    rows = z_ref.shape[0]
    W = SGU_WIDTH
    gi = lax.broadcasted_iota(I32, (W, W), 0) // SGU_GROUP_DIM
    gj = lax.broadcasted_iota(I32, (W, W), 1) // SGU_GROUP_DIM
    avg = jnp.where(gi == gj, 1.0 / SGU_GROUP_DIM, 0.0).astype(BF16)
    ti = lax.broadcasted_iota(I32, (SGU_BLOCK, SGU_BLOCK), 0) // STREAM_CHUNK
    si = lax.broadcasted_iota(I32, (SGU_BLOCK, SGU_BLOCK), 1) // STREAM_CHUNK
    chunk_causal = si <= ti
    lane_group = lax.broadcasted_iota(I32, (SGU_BLOCK, W), 1) // SGU_GROUP_DIM
    for r in range(rows // SGU_BLOCK):
        rs = slice(r * SGU_BLOCK, (r + 1) * SGU_BLOCK)
        u = jax.nn.gelu(z_ref[rs, :W].astype(F32))
        v = jax.nn.gelu(z_ref[rs, W:].astype(F32))
        mu = _dot_split_lhs(v, avg)
        vc = v - mu
        var = _dot_split_lhs(vc * vc, avg)
        vn = (vc * lax.rsqrt(var + EPS) * lg_ref[...] + lb_ref[...]).astype(BF16)
        mixed = b_ref[...]
        for g in range(SGU_GROUPS):
            wg = jnp.where(chunk_causal, w_ref[g], 0.0).astype(BF16)
            mixed = mixed + _dot(wg, jnp.where(lane_group == g, vn, jnp.zeros_like(vn)))
        o_ref[rs, :] = _rms(u * mixed, og_ref[...]).astype(o_ref.dtype)


def _sgu(zc, lg, lb, w, bias, og, batch, seq, rows):
    full = lambda a: pl.BlockSpec(a.shape, lambda b, r: (0,) * a.ndim)
    return pl.pallas_call(
        _sgu_kernel,
        grid=(batch, seq // rows),
        in_specs=[pl.BlockSpec((None, rows, 2 * SGU_WIDTH), lambda b, r: (b, r, 0)),
                  full(lg), full(lb), full(w), full(bias), full(og)],
        out_specs=pl.BlockSpec((None, rows, SGU_WIDTH), lambda b, r: (b, r, 0)),
        out_shape=jax.ShapeDtypeStruct((batch, seq, SGU_WIDTH), BF16),
        compiler_params=_cparams(("parallel", "parallel")),
        name="sgu",
    )(zc.reshape(batch, seq, 2 * SGU_WIDTH), lg, lb, w, bias, og)


def _dispatch_rows(tm):
    return 2 * tm + N_EXPERTS * (CHUNK_ROWS - 1) + (N_EXPERTS * (CHUNK_ROWS - 1)) % CHUNK_ROWS


def _mix_out_kernel(h_ref, ya_ref, yb_ref, yc_ref, ag_ref, wa_ref, wb_ref, wc_ref, n2_ref,
                    rwh_ref, rwl_ref, rb_ref, h1_ref, xs_ref, meta_ref, cnt_ref):
    tm = h_ref.shape[0]
    rr = xs_ref.shape[0]
    ya = _rms(ya_ref[...].astype(F32), ag_ref[...]).astype(BF16)
    h1 = h_ref[...] + _dot(ya, wa_ref[...]) + _dot(yb_ref[...], wb_ref[...]) + _dot(yc_ref[...], wc_ref[...])
    h1_ref[...] = h1
    xn = _rms(h1, n2_ref[...])
    x_hi, x_lo = _split_bf16(xn)
    logits = (_dot_nt(rwh_ref[...], x_hi) + _dot_nt(rwl_ref[...], x_hi) + _dot_nt(rwh_ref[...], x_lo)
              + rb_ref[...])
    e_log = logits[:N_EXPERTS]
    g_log = logits[N_EXPERTS:N_EXPERTS + N_GROUPS]
    g_iota = lax.broadcasted_iota(I32, (N_GROUPS, tm), 0)
    g_max = jnp.max(g_log, axis=0, keepdims=True)
    g_top = jnp.min(jnp.where(g_log == g_max, g_iota, N_GROUPS), axis=0, keepdims=True)
    g_w = 1.0 / jnp.sum(jnp.exp(g_log - g_max), axis=0, keepdims=True)
    e_iota = lax.broadcasted_iota(I32, (N_EXPERTS, tm), 0)
    val = jnp.where(e_iota // EXPERTS_PER_GROUP == g_top, e_log, -jnp.inf)
    m1 = jnp.max(val, axis=0, keepdims=True)
    i1 = jnp.min(jnp.where(val == m1, e_iota, N_EXPERTS), axis=0, keepdims=True)
    val2 = jnp.where(e_iota == i1, -jnp.inf, val)
    m2 = jnp.max(val2, axis=0, keepdims=True)
    i2 = jnp.min(jnp.where(val2 == m2, e_iota, N_EXPERTS), axis=0, keepdims=True)
    e2 = jnp.exp(m2 - m1)
    w0 = g_w / (1.0 + e2)
    w1 = g_w * e2 / (1.0 + e2)
    oh0 = e_iota == i1
    oh1 = e_iota == i2
    oh = jnp.concatenate([jnp.where(oh0, 1.0, 0.0), jnp.where(oh1, 1.0, 0.0)], axis=0).astype(BF16)
    ta = lax.broadcasted_iota(I32, (tm, tm), 0)
    tb = lax.broadcasted_iota(I32, (tm, tm), 1)
    before = jnp.where(ta < tb, 1.0, 0.0).astype(BF16)
    rank = _dot(oh, before)
    c0 = jnp.sum(jnp.where(oh0, 1.0, 0.0), axis=1, keepdims=True)
    c1 = jnp.sum(jnp.where(oh1, 1.0, 0.0), axis=1, keepdims=True)
    nchunk = jnp.floor((c0 + c1 + (CHUNK_ROWS - 1)) * (1.0 / CHUNK_ROWS))
    nchunk_b = jnp.broadcast_to(nchunk, (N_EXPERTS, LANES))
    ea = lax.broadcasted_iota(I32, (N_EXPERTS, N_EXPERTS), 0)
    eb = lax.broadcasted_iota(I32, (N_EXPERTS, N_EXPERTS), 1)
    lower = jnp.where(eb < ea, 1.0, 0.0).astype(BF16)
    loc = CHUNK_ROWS * _dot(lower, nchunk_b.astype(BF16))[:, 0:1]
    dest0 = jnp.sum(jnp.where(oh0, loc + rank[:N_EXPERTS], 0.0), axis=0, keepdims=True)
    dest1 = jnp.sum(jnp.where(oh1, loc + c0 + rank[N_EXPERTS:], 0.0), axis=0, keepdims=True)
    r_iota = lax.broadcasted_iota(I32, (rr, tm), 0)
    d0 = dest0.astype(I32)
    d1 = dest1.astype(I32)
    perm = jnp.where((r_iota == d0) | (r_iota == d1), 1.0, 0.0).astype(BF16)
    xs_ref[...] = _dot(perm, x_hi).astype(BF16)
    cnt_ref[...] = nchunk_b
    m_iota = lax.broadcasted_iota(I32, (8, tm), 0)
    meta = jnp.where(m_iota == 0, dest0, jnp.where(m_iota == 1, dest1, jnp.where(m_iota == 2, w0,
                     jnp.where(m_iota == 3, w1, 0.0))))
    meta = jnp.concatenate([meta, jnp.zeros((LANES - 8, tm), F32)], axis=0)
    meta_ref[...] = jnp.transpose(meta)


def _mix_out(h, ya, yb, yc, ag, wa, wb, wc, n2, rwh, rwl, rb, tm):
    t = h.shape[0]
    nt = t // tm
    rr = _dispatch_rows(tm)
    row = lambda n: pl.BlockSpec((tm, n), lambda i: (i, 0))
    full = lambda a: pl.BlockSpec(a.shape, lambda i: (0,) * a.ndim)
    return pl.pallas_call(
        _mix_out_kernel,
        grid=(nt,),
        in_specs=[row(D_MODEL), row(SB_WIDTH), row(ML_PAD), row(SGU_WIDTH), full(ag), full(wa), full(wb),
                  full(wc), full(n2), full(rwh), full(rwl), full(rb)],
        out_specs=[row(D_MODEL), pl.BlockSpec((rr, D_MODEL), lambda i: (i, 0)), row(LANES),
                   pl.BlockSpec((None, N_EXPERTS, LANES), lambda i: (i, 0, 0))],
        out_shape=[jax.ShapeDtypeStruct((t, D_MODEL), F32),
                   jax.ShapeDtypeStruct((nt * rr, D_MODEL), BF16),
                   jax.ShapeDtypeStruct((t, LANES), F32),
                   jax.ShapeDtypeStruct((nt, N_EXPERTS, LANES), F32)],
        compiler_params=_cparams(("parallel",)),
        name="mix_out",
    )(h, ya, yb, yc, ag, wa, wb, wc, n2, rwh, rwl, rb)


def _chunk_tables(cnt, rr, n_blocks_max):
    nt = cnt.shape[0]
    loc = (jnp.cumsum(cnt, axis=1) - cnt) * CHUNK_ROWS
    cnt_e = cnt.T
    cum_e = jnp.cumsum(cnt_e, axis=1)
    total = cum_e[:, -1]
    nblk = (total + EXPERT_BLOCK_CHUNKS - 1) // EXPERT_BLOCK_CHUNKS
    bend = jnp.cumsum(nblk)
    bstart = bend - nblk
    n_blocks = bend[-1]
    i = jnp.arange(n_blocks_max, dtype=I32)
    be = jnp.minimum(jnp.searchsorted(bend, i, side="right"), N_EXPERTS - 1).astype(I32)
    q = (i - bstart[be])[:, None] * EXPERT_BLOCK_CHUNKS + jnp.arange(EXPERT_BLOCK_CHUNKS, dtype=I32)[None, :]
    valid = (q < total[be][:, None]) & (i < n_blocks)[:, None]
    cum_b = cum_e[be]
    j = jnp.minimum(jnp.sum(cum_b[:, None, :] <= q[:, :, None], axis=-1), nt - 1).astype(I32)
    prev = jnp.take_along_axis(cum_b - cnt_e[be], j, axis=1)
    src = j * rr + loc[j, be[:, None]] + (q - prev) * CHUNK_ROWS
    src = jnp.where(valid, src, -1).astype(I32)
    return be, src.reshape(-1), n_blocks.reshape(1).astype(I32)


def _experts_kernel(be_ref, src_ref, nb_ref, xs_ref, y0_ref, wg_ref, wu_ref, wd_ref, ys_ref,
                    xbuf, ybuf, sem):
    del be_ref, y0_ref
    i = pl.program_id(0)

    def x_copy(s):
        src = pl.multiple_of(src_ref[i * EXPERT_BLOCK_CHUNKS + s], CHUNK_ROWS)
        return pltpu.make_async_copy(xs_ref.at[pl.ds(src, CHUNK_ROWS), :],
                                     xbuf.at[pl.ds(s * CHUNK_ROWS, CHUNK_ROWS), :], sem.at[0])

    def y_copy(s):
        src = pl.multiple_of(src_ref[i * EXPERT_BLOCK_CHUNKS + s], CHUNK_ROWS)
        return pltpu.make_async_copy(ybuf.at[pl.ds(s * CHUNK_ROWS, CHUNK_ROWS), :],
                                     ys_ref.at[pl.ds(src, CHUNK_ROWS), :], sem.at[1])

    def for_slots(fn_valid, fn_empty=None):
        for s in range(EXPERT_BLOCK_CHUNKS):
            ok = src_ref[i * EXPERT_BLOCK_CHUNKS + s] >= 0

            @pl.when(ok)
            def _():
                fn_valid(s)

            if fn_empty is not None:
                @pl.when(jnp.logical_not(ok))
                def _():
                    fn_empty(s)

    @pl.when(i < nb_ref[0])
    def _():
        def zero_slot(s):
            xbuf[s * CHUNK_ROWS:(s + 1) * CHUNK_ROWS, :] = jnp.zeros((CHUNK_ROWS, D_MODEL), BF16)

        for_slots(lambda s: x_copy(s).start(), zero_slot)
        for_slots(lambda s: x_copy(s).wait())
        x = xbuf[...]
        gate = _dot(x, wg_ref[...])
        up = _dot(x, wu_ref[...])
        mid = (gate * jax.nn.sigmoid(gate) * up).astype(BF16)
        ybuf[...] = _dot(mid, wd_ref[...]).astype(BF16)
        for_slots(lambda s: y_copy(s).start())
        for_slots(lambda s: y_copy(s).wait())


def _experts(be, src, nb, xs, wg, wu, wd):
    n_blocks_max = be.shape[0]
    y0 = jnp.zeros(xs.shape, BF16)
    wspec = lambda a: pl.BlockSpec((None,) + a.shape[1:], lambda i, be, src, nb: (be[i], 0, 0))
    grid_spec = pltpu.PrefetchScalarGridSpec(
        num_scalar_prefetch=3,
        grid=(n_blocks_max,),
        in_specs=[pl.BlockSpec(memory_space=pl.ANY), pl.BlockSpec(memory_space=pl.ANY),
                  wspec(wg), wspec(wu), wspec(wd)],
        out_specs=pl.BlockSpec(memory_space=pl.ANY),
        scratch_shapes=[pltpu.VMEM((EXPERT_BLOCK, D_MODEL), BF16), pltpu.VMEM((EXPERT_BLOCK, D_MODEL), BF16),
                        pltpu.SemaphoreType.DMA((2,))],
    )
    return pl.pallas_call(
        _experts_kernel,
        grid_spec=grid_spec,
        out_shape=jax.ShapeDtypeStruct(xs.shape, BF16),
        input_output_aliases={4: 0},
        compiler_params=_cparams(("arbitrary",)),
        name="experts",
    )(be, src, nb, xs, y0, wg, wu, wd)


def _combine_kernel(h_ref, ys_ref, meta_ref, p_ref, pg_ref, gw_ref, pw_ref, fg_ref, o_ref, *, final):
    tm = h_ref.shape[0]
    rr = ys_ref.shape[0]
    meta = meta_ref[...]
    d0 = meta[:, 0:1].astype(I32)
    d1 = meta[:, 1:2].astype(I32)
    r_iota = lax.broadcasted_iota(I32, (tm, rr), 1)
    unperm = (jnp.where(r_iota == d0, meta[:, 2:3], 0.0) + jnp.where(r_iota == d1, meta[:, 3:4], 0.0)).astype(BF16)
    h2 = h_ref[...] + _dot(unperm, ys_ref[...])
    gate = jax.nn.sigmoid(_dot(_rms(h2, pg_ref[...]).astype(BF16), gw_ref[...]))
    h3 = h2 + gate * _dot(p_ref[...].astype(BF16), pw_ref[...])
    if final:
        h3 = _rms(h3, fg_ref[...])
    o_ref[...] = h3


def _combine(h1, ys, meta, p, pg, gw, pw, fg, tm, final):
    t = h1.shape[0]
    rr = ys.shape[0] // (t // tm)
    row = lambda n: pl.BlockSpec((tm, n), lambda i: (i, 0))
    full = lambda a: pl.BlockSpec(a.shape, lambda i: (0,) * a.ndim)
    return pl.pallas_call(
        functools.partial(_combine_kernel, final=final),
        grid=(t // tm,),
        in_specs=[row(D_MODEL), pl.BlockSpec((rr, D_MODEL), lambda i: (i, 0)), row(LANES), row(P_DIM),
                  full(pg), full(gw), full(pw), full(fg)],
        out_specs=row(D_MODEL),
        out_shape=jax.ShapeDtypeStruct((t, D_MODEL), F32),
        compiler_params=_cparams(("parallel",)),
        name="combine",
    )(h1, ys, meta, p, pg, gw, pw, fg)


def _pad_heads(a, axis):
    shape = a.shape
    a = a.reshape(shape[:axis] + (ML_HEADS, ML_HEAD_DIM) + shape[axis + 1:])
    pad = [(0, 0)] * a.ndim
    pad[axis + 1] = (0, LANES - ML_HEAD_DIM)
    a = jnp.pad(a, pad)
    return a.reshape(shape[:axis] + (ML_PAD,) + shape[axis + 1:])


def _layer_params(i, w_in, conv_w, conv_b, igate_b, fgate_b, mnorm_g, sgu_b, w_out, router_gw, router_gb,
                  router_ew, router_eb):
    w = w_in[i]
    s, m = SB_WIDTH, ML_WIDTH
    a_q, a_k, a_v = w[:, 0:s] * (SB_HEAD_DIM ** -0.5), w[:, s:2 * s], w[:, 2 * s:3 * s]
    o = 3 * s
    b_q, b_k, b_v, b_o = (w[:, o + k * m:o + (k + 1) * m] for k in range(4))
    o = o + 4 * m
    gates = w[:, o:o + 2 * ML_HEADS]
    c_uv = w[:, o + 2 * ML_HEADS:]
    w_r = jnp.concatenate([a_q, a_k, a_v, _pad_heads(b_q, 1), _pad_heads(b_k, 1), _pad_heads(b_v, 1),
                           _pad_heads(b_o, 1), c_uv,
                           jnp.pad(gates, ((0, 0), (0, LANES - 2 * ML_HEADS)))], axis=1).astype(BF16)
    cw = jnp.concatenate([_pad_heads(conv_w[i][:, :m], 1), _pad_heads(conv_w[i][:, m:], 1)], axis=1)
    cb = jnp.concatenate([_pad_heads(conv_b[i][:m], 0), _pad_heads(conv_b[i][m:], 0)])[None, :]
    gb = jnp.pad(jnp.concatenate([igate_b[i], fgate_b[i]]), (0, LANES - 2 * ML_HEADS))[None, :]
    mg = _pad_heads(mnorm_g[i], 0)[None, :]
    sgu_bias = jnp.repeat(sgu_b[i].T, SGU_GROUP_DIM, axis=1)
    wo = w_out[i]
    wa = wo[:s].astype(BF16)
    wb = _pad_heads(wo[s:s + m], 0).astype(BF16)
    wc = wo[s + m:].astype(BF16)
    rw = jnp.concatenate([router_ew[i].T, router_gw[i].T,
                          jnp.zeros((ROUTER_ROWS - N_EXPERTS - N_GROUPS, D_MODEL), F32)], axis=0)
    rwh = rw.astype(BF16)
    rwl = (rw - rwh.astype(F32)).astype(BF16)
    rb = jnp.concatenate([router_eb[i], router_gb[i],
                          jnp.zeros((ROUTER_ROWS - N_EXPERTS - N_GROUPS,), F32)])[:, None]
    return w_r, cw, cb, gb, mg, sgu_bias, wa, wb, wc, rwh, rwl, rb


def kernel(x, p, norm1_g, w_in, conv_w, conv_b, igate_b, fgate_b, mnorm_g, sb_out_g, sgu_ln_g, sgu_ln_b, sgu_w,
           sgu_b, sgu_out_g, w_out, norm2_g, router_gw, router_gb, router_ew, router_eb, w_gate, w_up, w_down,
           ple_norm_g, ple_gate_w, ple_proj_w, final_g, *, tile=512):
    batch, seq, d = x.shape
    depth = w_in.shape[0]
    t = batch * seq
    tm = min(tile, t)
    nt = t // tm
    rr = _dispatch_rows(tm)
    n_blocks_max = (nt * rr // CHUNK_ROWS) // EXPERT_BLOCK_CHUNKS + N_EXPERTS
    h = x.astype(F32).reshape(t, d)
    for i in range(depth):
        (w_r, cw, cb, gb, mg, sgu_bias, wa, wb, wc, rwh, rwl, rb) = _layer_params(
            i, w_in, conv_w, conv_b, igate_b, fgate_b, mnorm_g, sgu_b, w_out, router_gw, router_gb,
            router_ew, router_eb)
        za, zqk, zvo, zc, zg = _in_proj(h, norm1_g[i][None, :], w_r, tm)
        ya = _sb_attn(za, batch, seq).reshape(t, SB_WIDTH)
        yb = _mlstm(zqk, zvo, zg, cw, cb, gb, mg, batch, seq).reshape(t, ML_PAD)
        yc = _sgu(zc, sgu_ln_g[i][None, :], sgu_ln_b[i][None, :], sgu_w[i], sgu_bias, sgu_out_g[i][None, :],
                  batch, seq, min(seq, 512)).reshape(t, SGU_WIDTH)
        h1, xs, meta, cnt = _mix_out(h, ya, yb, yc, sb_out_g[i][None, :], wa, wb, wc, norm2_g[i][None, :],
                                     rwh, rwl, rb, tm)
        be, src, nb = _chunk_tables(cnt[:, :, 0].astype(I32), rr, n_blocks_max)
        ys = _experts(be, src, nb, xs, w_gate[i].astype(BF16), w_up[i].astype(BF16), w_down[i].astype(BF16))
        h = _combine(h1, ys, meta, p[i].reshape(t, P_DIM), ple_norm_g[i][None, :], ple_gate_w[i].astype(BF16),
                     ple_proj_w[i].astype(BF16), final_g[None, :], tm, i == depth - 1)
    return h.reshape(batch, seq, d).astype(x.dtype)
```

```python
import functools

import jax
import jax.numpy as jnp
import numpy as np
from jax import lax
from jax.experimental import pallas as pl
from jax.experimental.pallas import tpu as pltpu

F32 = jnp.float32
BF16 = jnp.bfloat16
I32 = jnp.int32

D_MODEL = 1024
P_DIM = 256
EPS = 1e-6
LANES = 128
SB_HEAD_DIM = 64
SB_WIDTH = 384
SB_BLOCK = 128
SB_QTILE = 512
LOG2_E = 1.4426950408889634
ML_HEADS = 4
ML_HEAD_DIM = 96
ML_WIDTH = 384
ML_PAD = ML_HEADS * LANES
ML_CHUNK = 128
CONV_K = 4
SGU_WIDTH = 256
SGU_GROUPS = 4
SGU_GROUP_DIM = 64
SGU_BLOCK = 128
STREAM_CHUNK = 64
N_GROUPS = 4
EXPERTS_PER_GROUP = 8
N_EXPERTS = 32
D_EXPERT = 256
ROUTER_ROWS = 40
CHUNK_ROWS = 16
EXPERT_BLOCK_CHUNKS = 32
EXPERT_BLOCK = CHUNK_ROWS * EXPERT_BLOCK_CHUNKS

COL_A = 0
COL_BQK = 3 * SB_WIDTH
COL_BVO = COL_BQK + 2 * ML_PAD
COL_C = COL_BVO + 2 * ML_PAD
COL_G = COL_C + 2 * SGU_WIDTH
N_Z = COL_G + LANES

VMEM_LIMIT = 56 * 1024 * 1024


def _cparams(sem, vmem=VMEM_LIMIT):
    return pltpu.CompilerParams(dimension_semantics=sem, vmem_limit_bytes=vmem)


def _rms(x, g):
    return x * lax.rsqrt(jnp.mean(x * x, axis=-1, keepdims=True) + EPS) * g


def _split_bf16(x):
    hi = x.astype(BF16)
    lo = (x - hi.astype(F32)).astype(BF16)
    return hi, lo


def _dot(a, b):
    return jnp.dot(a, b, preferred_element_type=F32)


def _dot_nt(a, b):
    return lax.dot_general(a, b, (((1,), (1,)), ((), ())), preferred_element_type=F32)


def _dot_tn(a, b):
    return lax.dot_general(a, b, (((0,), (0,)), ((), ())), preferred_element_type=F32)


def _dot_split_lhs(x, m):
    hi, lo = _split_bf16(x)
    return _dot(hi, m) + _dot(lo, m)


def _dot_split_rhs(m, x):
    hi, lo = _split_bf16(x)
    return _dot(m, hi) + _dot(m, lo)


def _in_proj_kernel(x_ref, g_ref, w_ref, za_ref, zqk_ref, zvo_ref, zc_ref, zg_ref):
    hn = _rms(x_ref[...], g_ref[...]).astype(BF16)
    za_ref[...] = _dot(hn, w_ref[:, COL_A:COL_BQK]).astype(BF16)
    zqk_ref[...] = _dot(hn, w_ref[:, COL_BQK:COL_BVO])
    zvo_ref[...] = _dot(hn, w_ref[:, COL_BVO:COL_C]).astype(BF16)
    zc_ref[...] = _dot(hn, w_ref[:, COL_C:COL_G]).astype(BF16)
    zg_ref[...] = _dot(hn, w_ref[:, COL_G:N_Z])


def _in_proj(h, g, w, tm):
    t = h.shape[0]
    row = lambda n: pl.BlockSpec((tm, n), lambda i: (i, 0))
    full = lambda a: pl.BlockSpec(a.shape, lambda i: (0,) * a.ndim)
    return pl.pallas_call(
        _in_proj_kernel,
        grid=(t // tm,),
        in_specs=[row(D_MODEL), full(g), full(w)],
        out_specs=[row(3 * SB_WIDTH), row(2 * ML_PAD), row(2 * ML_PAD), row(2 * SGU_WIDTH), row(LANES)],
        out_shape=[jax.ShapeDtypeStruct((t, 3 * SB_WIDTH), BF16),
                   jax.ShapeDtypeStruct((t, 2 * ML_PAD), F32),
                   jax.ShapeDtypeStruct((t, 2 * ML_PAD), BF16),
                   jax.ShapeDtypeStruct((t, 2 * SGU_WIDTH), BF16),
                   jax.ShapeDtypeStruct((t, LANES), F32)],
        compiler_params=_cparams(("parallel",)),
        name="in_proj",
    )(h, g, w)


def _sb_kernel(q_ref, k_ref, v_ref, o_ref, acc_ref, car_ref, lsig_ref, after_ref, tot_ref):
    n = pl.program_id(2)
    blk = SB_BLOCK
    tq = q_ref.shape[0]
    sub = tq // blk
    lane_k = lax.broadcasted_iota(I32, (blk, LANES), 1)
    head0 = lane_k < SB_HEAD_DIM
    mj = lax.broadcasted_iota(I32, (2 * blk, 2 * blk), 0) % blk
    ms = lax.broadcasted_iota(I32, (2 * blk, 2 * blk), 1)
    suffix = jnp.where((mj > ms) | (ms >= blk), 1.0, 0.0).astype(BF16)

    acc_ref[...] = jnp.zeros_like(acc_ref)
    car_ref[...] = jnp.zeros_like(car_ref)

    def per_head(x):
        zero = jnp.zeros_like(x)
        return jnp.concatenate([jnp.where(head0, x, zero), jnp.where(head0, zero, x)], axis=0)

    def stage1(j, r0, masked, buf):
        start = pl.multiple_of(j * blk, blk)
        z = _dot_nt(q_ref[r0:, :], per_head(k_ref[pl.ds(start, blk), :]))
        sign = jnp.uint32(0x80000000)
        neg_abs = lax.bitcast_convert_type(lax.bitcast_convert_type(z, jnp.uint32) | sign, F32)
        e = jnp.exp2(neg_abs)
        log_sig = jnp.minimum(z, 0.0) - jnp.log2(1.0 + e)
        log1m = log_sig - z
        if masked:
            rows = tq - r0
            mask = (lax.broadcasted_iota(I32, (rows, 2 * blk), 1) % blk
                    < lax.broadcasted_iota(I32, (rows, 2 * blk), 0))
            log1m = jnp.where(mask, log1m, 0.0)
            log_sig = jnp.where(mask, log_sig, -jnp.inf)
        lsig_ref[buf, r0:, :] = log_sig
        hi, lo = _split_bf16(log1m)
        for hh in range(2):
            cs = slice(hh * blk, (hh + 1) * blk)
            ca = _dot(jnp.concatenate([hi[:, cs], lo[:, cs]], axis=1), suffix)
            after_ref[buf, r0:, cs] = ca[:, :blk]
            tot_ref[buf, r0:, cs] = ca[:, blk:]

    def stage2(j, r0, buf):
        start = pl.multiple_of(j * blk, blk)
        arg = lsig_ref[buf, r0:, :] + after_ref[buf, r0:, :] + car_ref[r0:, :]
        acc_ref[r0:, :] += _dot(jnp.exp2(arg).astype(BF16), per_head(v_ref[pl.ds(start, blk), :]))
        car_ref[r0:, :] += tot_ref[buf, r0:, :]

    for kj in reversed(range(sub)):
        stage1(n * sub + kj, kj * blk, True, kj % 2)
        stage2(n * sub + kj, kj * blk, kj % 2)

    top = n * sub - 1

    @pl.when(n > 0)
    def _():
        stage1(top, 0, False, 0)

    def body(i, c):
        j0 = top - 2 * i
        stage1(j0 - 1, 0, False, 1)
        stage2(j0, 0, 0)
        stage1(jnp.maximum(j0 - 2, 0), 0, False, 0)
        stage2(j0 - 1, 0, 1)
        return c

    lax.fori_loop(0, n * (sub // 2), body, 0)
    o_ref[...] = acc_ref[...].astype(o_ref.dtype)


def _sb_attn(za, batch, seq):
    za3 = za.reshape(batch, seq, 3 * SB_WIDTH)
    pairs = SB_WIDTH // LANES
    tq = min(SB_QTILE, seq)
    return pl.pallas_call(
        _sb_kernel,
        grid=(batch, pairs, seq // tq),
        in_specs=[pl.BlockSpec((None, tq, LANES), lambda b, p, n: (b, n, p)),
                  pl.BlockSpec((None, seq, LANES), lambda b, p, n: (b, 0, pairs + p)),
                  pl.BlockSpec((None, seq, LANES), lambda b, p, n: (b, 0, 2 * pairs + p))],
        out_specs=pl.BlockSpec((None, tq, LANES), lambda b, p, n: (b, n, p)),
        out_shape=jax.ShapeDtypeStruct((batch, seq, SB_WIDTH), BF16),
        scratch_shapes=[pltpu.VMEM((tq, LANES), F32),
                        pltpu.VMEM((tq, 2 * LANES), F32),
                        pltpu.VMEM((2, tq, 2 * LANES), F32),
                        pltpu.VMEM((2, tq, 2 * LANES), F32),
                        pltpu.VMEM((2, tq, 2 * LANES), F32)],
        compiler_params=_cparams(("parallel", "parallel", "arbitrary")),
        name="sb_attn",
    )(za3, za3, za3)


def _mlstm_kernel(qk_ref, vo_ref, g_ref, cw_ref, cb_ref, gb_ref, mg_ref, o_ref,
                  q_s, k_s, c_s, n_s, m_s):
    seq = qk_ref.shape[0]
    L = ML_CHUNK
    cw = cw_ref[...]
    cb = cb_ref[...]
    rows = lax.broadcasted_iota(I32, (L, 2 * ML_PAD), 0)
    for c in range(seq // L):
        r0 = c * L
        y = qk_ref[r0:r0 + L, :] * cw[CONV_K - 1:CONV_K, :] + cb
        for d in range(1, CONV_K):
            if c == 0:
                xd = jnp.where(rows >= d, pltpu.roll(qk_ref[0:L, :], d, axis=0), 0.0)
            else:
                xd = qk_ref[r0 - d:r0 - d + L, :]
            y = y + xd * cw[CONV_K - 1 - d:CONV_K - d, :]
        y = y * jax.nn.sigmoid(y)
        q_s[r0:r0 + L, :] = y[:, :ML_PAD].astype(BF16)
        k_s[r0:r0 + L, :] = (y[:, ML_PAD:] * (ML_HEAD_DIM ** -0.5)).astype(BF16)

    c_s[...] = jnp.zeros_like(c_s)
    n_s[...] = jnp.zeros_like(n_s)
    m_s[...] = jnp.zeros_like(m_s)

    ti = lax.broadcasted_iota(I32, (L, L), 0)
    si = lax.broadcasted_iota(I32, (L, L), 1)
    causal = si <= ti
    tril = jnp.where(causal, 1.0, 0.0).astype(BF16)
    lane = lax.broadcasted_iota(I32, (L, LANES), 1)
    gb = gb_ref[...]
    mg = mg_ref[...]

    def chunk(c, carry):
        r0 = pl.multiple_of(c * L, L)
        g = g_ref[pl.ds(r0, L), :] + gb
        logf = jnp.minimum(g, 0.0) - jnp.log1p(jnp.exp(-jnp.abs(g)))
        gl = jnp.where(lane < ML_HEADS, g, logf)
        cum = _dot_split_rhs(tril, logf)
        gl_t = jnp.transpose(gl)
        cum_t = jnp.transpose(cum)
        for h in range(ML_HEADS):
            cs = slice(h * LANES, (h + 1) * LANES)
            qc = q_s[pl.ds(r0, L), cs]
            kc = k_s[pl.ds(r0, L), cs]
            vc = vo_ref[pl.ds(r0, L), cs]
            oc = vo_ref[pl.ds(r0, L), ML_PAD + h * LANES:ML_PAD + (h + 1) * LANES].astype(F32)
            icol = gl[:, h:h + 1]
            irow = gl_t[h:h + 1, :]
            bcol = cum[:, ML_HEADS + h:ML_HEADS + h + 1]
            brow = cum_t[ML_HEADS + h:ML_HEADS + h + 1, :]
            btot = bcol[L - 1:L, :]
            m_in = m_s[h][0:1, 0:1]
            cmat = c_s[h]
            nrow = n_s[h][0:1, :]
            dmat = jnp.where(causal, bcol - brow + irow, -jnp.inf)
            m_inter = bcol + m_in
            m_t = jnp.maximum(m_inter, jnp.max(dmat, axis=1, keepdims=True))
            wd = jnp.where(causal, jnp.exp(dmat - m_t), 0.0)
            s = _dot_nt(qc, kc) * wd
            inter = jnp.exp(m_inter - m_t)
            num = _dot(s.astype(BF16), vc) + inter * _dot(qc, cmat.astype(BF16))
            qn = jnp.sum(qc.astype(F32) * nrow, axis=1, keepdims=True)
            den = jnp.sum(s, axis=1, keepdims=True) + inter * qn
            den = jnp.maximum(jnp.abs(den), jnp.exp(-m_t))
            hh = num / den
            ms = jnp.sum(hh * hh, axis=1, keepdims=True) * (1.0 / ML_HEAD_DIM)
            hh = hh * lax.rsqrt(ms + EPS) * mg[:, cs]
            o_ref[pl.ds(r0, L), cs] = (hh * jax.nn.sigmoid(oc)).astype(o_ref.dtype)
            acol = btot - bcol + icol
            m_new = jnp.maximum(btot + m_in, jnp.max(acol, axis=0, keepdims=True))
            decay = jnp.exp(btot + m_in - m_new)
            kw = kc.astype(F32) * jnp.exp(acol - m_new)
            c_s[h] = decay * cmat + _dot_tn(kw.astype(BF16), vc)
            n_s[h] = jnp.broadcast_to(decay * nrow + jnp.sum(kw, axis=0, keepdims=True), (8, LANES))
            m_s[h] = jnp.broadcast_to(m_new, (8, LANES))
        return carry

    lax.fori_loop(0, seq // L, chunk, 0)


def _mlstm(zqk, zvo, zg, cw, cb, gb, mg, batch, seq):
    full = lambda a: pl.BlockSpec(a.shape, lambda b: (0,) * a.ndim)
    seq_blk = lambda n: pl.BlockSpec((None, seq, n), lambda b: (b, 0, 0))
    return pl.pallas_call(
        _mlstm_kernel,
        grid=(batch,),
        in_specs=[seq_blk(2 * ML_PAD), seq_blk(2 * ML_PAD), seq_blk(LANES),
                  full(cw), full(cb), full(gb), full(mg)],
        out_specs=seq_blk(ML_PAD),
        out_shape=jax.ShapeDtypeStruct((batch, seq, ML_PAD), BF16),
        scratch_shapes=[pltpu.VMEM((seq, ML_PAD), BF16), pltpu.VMEM((seq, ML_PAD), BF16),
                        pltpu.VMEM((ML_HEADS, LANES, LANES), F32),
                        pltpu.VMEM((ML_HEADS, 8, LANES), F32),
                        pltpu.VMEM((ML_HEADS, 8, LANES), F32)],
        compiler_params=_cparams(("parallel",)),
        name="mlstm",
    )(zqk.reshape(batch, seq, 2 * ML_PAD), zvo.reshape(batch, seq, 2 * ML_PAD),
      zg.reshape(batch, seq, LANES), cw, cb, gb, mg)


def _sgu_kernel(z_ref, lg_ref, lb_ref, w_ref, b_ref, og_ref, o_ref):
    rows = z_ref.shape[0]
    W = SGU_WIDTH
    gi = lax.broadcasted_iota(I32, (W, W), 0) // SGU_GROUP_DIM
    gj = lax.broadcasted_iota(I32, (W, W), 1) // SGU_GROUP_DIM
    avg = jnp.where(gi == gj, 1.0 / SGU_GROUP_DIM, 0.0).astype(BF16)
    ti = lax.broadcasted_iota(I32, (SGU_BLOCK, SGU_BLOCK), 0) // STREAM_CHUNK
    si = lax.broadcasted_iota(I32, (SGU_BLOCK, SGU_BLOCK), 1) // STREAM_CHUNK
    chunk_causal = si <= ti
    lane_group = lax.broadcasted_iota(I32, (SGU_BLOCK, W), 1) // SGU_GROUP_DIM
    for r in range(rows // SGU_BLOCK):
        rs = slice(r * SGU_BLOCK, (r + 1) * SGU_BLOCK)
        u = jax.nn.gelu(z_ref[rs, :W].astype(F32))
        v = jax.nn.gelu(z_ref[rs, W:].astype(F32))
        mu = _dot_split_lhs(v, avg)
        vc = v - mu
        var = _dot_split_lhs(vc * vc, avg)
        vn = (vc * lax.rsqrt(var + EPS) * lg_ref[...] + lb_ref[...]).astype(BF16)
        mixed = b_ref[...]
        for g in range(SGU_GROUPS):
            wg = jnp.where(chunk_causal, w_ref[g], 0.0).astype(BF16)
            mixed = mixed + _dot(wg, jnp.where(lane_group == g, vn, jnp.zeros_like(vn)))
        o_ref[rs, :] = _rms(u * mixed, og_ref[...]).astype(o_ref.dtype)


def _sgu(zc, lg, lb, w, bias, og, batch, seq, rows):
    full = lambda a: pl.BlockSpec(a.shape, lambda b, r: (0,) * a.ndim)
    return pl.pallas_call(
        _sgu_kernel,
        grid=(batch, seq // rows),
        in_specs=[pl.BlockSpec((None, rows, 2 * SGU_WIDTH), lambda b, r: (b, r, 0)),
                  full(lg), full(lb), full(w), full(bias), full(og)],
        out_specs=pl.BlockSpec((None, rows, SGU_WIDTH), lambda b, r: (b, r, 0)),
        out_shape=jax.ShapeDtypeStruct((batch, seq, SGU_WIDTH), BF16),
        compiler_params=_cparams(("parallel", "parallel")),
        name="sgu",
    )(zc.reshape(batch, seq, 2 * SGU_WIDTH), lg, lb, w, bias, og)


def _dispatch_rows(tm):
    return 2 * tm + N_EXPERTS * (CHUNK_ROWS - 1) + (N_EXPERTS * (CHUNK_ROWS - 1)) % CHUNK_ROWS


def _mix_out_kernel(h_ref, ya_ref, yb_ref, yc_ref, ag_ref, wa_ref, wb_ref, wc_ref, n2_ref,
                    rwh_ref, rwl_ref, rb_ref, h1_ref, xs_ref, meta_ref, cnt_ref):
    tm = h_ref.shape[0]
    rr = xs_ref.shape[0]
    ya = _rms(ya_ref[...].astype(F32), ag_ref[...]).astype(BF16)
    h1 = h_ref[...] + _dot(ya, wa_ref[...]) + _dot(yb_ref[...], wb_ref[...]) + _dot(yc_ref[...], wc_ref[...])
    h1_ref[...] = h1
    xn = _rms(h1, n2_ref[...])
    x_hi, x_lo = _split_bf16(xn)
    logits = (_dot_nt(rwh_ref[...], x_hi) + _dot_nt(rwl_ref[...], x_hi) + _dot_nt(rwh_ref[...], x_lo)
              + rb_ref[...])
    e_log = logits[:N_EXPERTS]
    g_log = logits[N_EXPERTS:N_EXPERTS + N_GROUPS]
    g_iota = lax.broadcasted_iota(I32, (N_GROUPS, tm), 0)
    g_max = jnp.max(g_log, axis=0, keepdims=True)
    g_top = jnp.min(jnp.where(g_log == g_max, g_iota, N_GROUPS), axis=0, keepdims=True)
    g_w = 1.0 / jnp.sum(jnp.exp(g_log - g_max), axis=0, keepdims=True)
    e_iota = lax.broadcasted_iota(I32, (N_EXPERTS, tm), 0)
    val = jnp.where(e_iota // EXPERTS_PER_GROUP == g_top, e_log, -jnp.inf)
    m1 = jnp.max(val, axis=0, keepdims=True)
    i1 = jnp.min(jnp.where(val == m1, e_iota, N_EXPERTS), axis=0, keepdims=True)
    val2 = jnp.where(e_iota == i1, -jnp.inf, val)
    m2 = jnp.max(val2, axis=0, keepdims=True)
    i2 = jnp.min(jnp.where(val2 == m2, e_iota, N_EXPERTS), axis=0, keepdims=True)
    e2 = jnp.exp(m2 - m1)
    w0 = g_w / (1.0 + e2)
    w1 = g_w * e2 / (1.0 + e2)
    oh0 = e_iota == i1
    oh1 = e_iota == i2
    oh = jnp.concatenate([jnp.where(oh0, 1.0, 0.0), jnp.where(oh1, 1.0, 0.0)], axis=0).astype(BF16)
    ta = lax.broadcasted_iota(I32, (tm, tm), 0)
    tb = lax.broadcasted_iota(I32, (tm, tm), 1)
    before = jnp.where(ta < tb, 1.0, 0.0).astype(BF16)
    rank = _dot(oh, before)
    c0 = jnp.sum(jnp.where(oh0, 1.0, 0.0), axis=1, keepdims=True)
    c1 = jnp.sum(jnp.where(oh1, 1.0, 0.0), axis=1, keepdims=True)
    nchunk = jnp.floor((c0 + c1 + (CHUNK_ROWS - 1)) * (1.0 / CHUNK_ROWS))
    nchunk_b = jnp.broadcast_to(nchunk, (N_EXPERTS, LANES))
    ea = lax.broadcasted_iota(I32, (N_EXPERTS, N_EXPERTS), 0)
    eb = lax.broadcasted_iota(I32, (N_EXPERTS, N_EXPERTS), 1)
    lower = jnp.where(eb < ea, 1.0, 0.0).astype(BF16)
    loc = CHUNK_ROWS * _dot(lower, nchunk_b.astype(BF16))[:, 0:1]
    dest0 = jnp.sum(jnp.where(oh0, loc + rank[:N_EXPERTS], 0.0), axis=0, keepdims=True)
    dest1 = jnp.sum(jnp.where(oh1, loc + c0 + rank[N_EXPERTS:], 0.0), axis=0, keepdims=True)
    r_iota = lax.broadcasted_iota(I32, (rr, tm), 0)
    d0 = dest0.astype(I32)
    d1 = dest1.astype(I32)
    perm = jnp.where((r_iota == d0) | (r_iota == d1), 1.0, 0.0).astype(BF16)
    xs_ref[...] = _dot(perm, x_hi).astype(BF16)
    cnt_ref[...] = nchunk_b
    m_iota = lax.broadcasted_iota(I32, (8, tm), 0)
    meta = jnp.where(m_iota == 0, dest0, jnp.where(m_iota == 1, dest1, jnp.where(m_iota == 2, w0,
                     jnp.where(m_iota == 3, w1, 0.0))))
    meta = jnp.concatenate([meta, jnp.zeros((LANES - 8, tm), F32)], axis=0)
    meta_ref[...] = jnp.transpose(meta)


def _mix_out(h, ya, yb, yc, ag, wa, wb, wc, n2, rwh, rwl, rb, tm):
    t = h.shape[0]
    nt = t // tm
    rr = _dispatch_rows(tm)
    row = lambda n: pl.BlockSpec((tm, n), lambda i: (i, 0))
    full = lambda a: pl.BlockSpec(a.shape, lambda i: (0,) * a.ndim)
    return pl.pallas_call(
        _mix_out_kernel,
        grid=(nt,),
        in_specs=[row(D_MODEL), row(SB_WIDTH), row(ML_PAD), row(SGU_WIDTH), full(ag), full(wa), full(wb),
                  full(wc), full(n2), full(rwh), full(rwl), full(rb)],
        out_specs=[row(D_MODEL), pl.BlockSpec((rr, D_MODEL), lambda i: (i, 0)), row(LANES),
                   pl.BlockSpec((None, N_EXPERTS, LANES), lambda i: (i, 0, 0))],
        out_shape=[jax.ShapeDtypeStruct((t, D_MODEL), F32),
                   jax.ShapeDtypeStruct((nt * rr, D_MODEL), BF16),
                   jax.ShapeDtypeStruct((t, LANES), F32),
                   jax.ShapeDtypeStruct((nt, N_EXPERTS, LANES), F32)],
        compiler_params=_cparams(("parallel",)),
        name="mix_out",
    )(h, ya, yb, yc, ag, wa, wb, wc, n2, rwh, rwl, rb)


def _chunk_tables(cnt, rr, n_blocks_max):
    nt = cnt.shape[0]
    ne = N_EXPERTS + 1
    tail = rr // CHUNK_ROWS - jnp.sum(cnt, axis=1)
    cnt = jnp.concatenate([cnt, tail[:, None]], axis=1)
    loc = (jnp.cumsum(cnt, axis=1) - cnt) * CHUNK_ROWS
    cnt_e = cnt.T
    cum_e = jnp.cumsum(cnt_e, axis=1)
    total = cum_e[:, -1]
    nblk = (total + EXPERT_BLOCK_CHUNKS - 1) // EXPERT_BLOCK_CHUNKS
    bend = jnp.cumsum(nblk)
    bstart = bend - nblk
    n_blocks = bend[-1]
    i = jnp.arange(n_blocks_max, dtype=I32)
    be = jnp.minimum(jnp.sum(bend[None, :] <= i[:, None], axis=1), ne - 1).astype(I32)
    oh_b = be[:, None] == jnp.arange(ne, dtype=I32)[None, :]
    pick = lambda v: jnp.sum(jnp.where(oh_b, v[None, :], 0), axis=1)
    pick2 = lambda m: jnp.sum(jnp.where(oh_b[:, :, None], m[None, :, :], 0), axis=1)
    q = (i - pick(bstart))[:, None] * EXPERT_BLOCK_CHUNKS + jnp.arange(EXPERT_BLOCK_CHUNKS, dtype=I32)[None, :]
    valid = (q < pick(total)[:, None]) & (i < n_blocks)[:, None]
    cum_b = pick2(cum_e)
    base = jnp.arange(nt, dtype=I32)[None, :] * rr + loc.T - (cum_e - cnt_e) * CHUNK_ROWS
    base_b = pick2(base)
    j = jnp.minimum(jnp.sum(cum_b[:, None, :] <= q[:, :, None], axis=-1), nt - 1)
    oh_j = j[:, :, None] == jnp.arange(nt, dtype=I32)[None, None, :]
    src = jnp.sum(jnp.where(oh_j, base_b[:, None, :], 0), axis=-1) + q * CHUNK_ROWS
    src = jnp.where(valid, src, -1).astype(I32)
    return be, src.reshape(-1), n_blocks.reshape(1).astype(I32)


def _experts_kernel(be_ref, src_ref, nb_ref, xs_ref, wg_ref, wu_ref, wd_ref, ys_ref,
                    xbuf, ybuf, wgb, wub, wdb, sem_in, sem_out):
    i = pl.program_id(0)
    nb = nb_ref[0]
    slot = lax.rem(i, 2)

    def is_compute(b):
        return be_ref[b] < N_EXPERTS

    def for_chunks(b, fn_valid, fn_empty=None):
        def body(s, c):
            src = src_ref[b * EXPERT_BLOCK_CHUNKS + s]
            row = pl.multiple_of(s * CHUNK_ROWS, CHUNK_ROWS)

            @pl.when(src >= 0)
            def _():
                fn_valid(row, pl.multiple_of(src, CHUNK_ROWS))

            if fn_empty is not None:
                @pl.when(src < 0)
                def _():
                    fn_empty(row)
            return c

        lax.fori_loop(0, EXPERT_BLOCK_CHUNKS, body, 0)

    def x_copy(sl, row, src):
        return pltpu.make_async_copy(xs_ref.at[pl.ds(src, CHUNK_ROWS), :],
                                     xbuf.at[sl, pl.ds(row, CHUNK_ROWS), :], sem_in.at[sl])

    def y_copy(sl, row, src):
        return pltpu.make_async_copy(ybuf.at[sl, pl.ds(row, CHUNK_ROWS), :],
                                     ys_ref.at[pl.ds(src, CHUNK_ROWS), :], sem_out.at[sl])

    def start_loads(b, sl):
        def zero_rows(row):
            xbuf[sl, pl.ds(row, CHUNK_ROWS), :] = jnp.zeros((CHUNK_ROWS, D_MODEL), BF16)

        @pl.when(is_compute(b))
        def _():
            for_chunks(b, lambda row, src: x_copy(sl, row, src).start(), zero_rows)

    def wait_loads(b, sl):
        @pl.when(is_compute(b))
        def _():
            for_chunks(b, lambda row, src: x_copy(sl, row, src).wait())

    def start_stores(b, sl):
        for_chunks(b, lambda row, src: y_copy(sl, row, src).start())

    def wait_stores(b, sl):
        for_chunks(b, lambda row, src: y_copy(sl, row, src).wait())

    @pl.when(i < nb)
    def _():
        @pl.when(i == 0)
        def _():
            start_loads(0, 0)

        @pl.when(i + 1 < nb)
        def _():
            start_loads(i + 1, 1 - slot)

        wait_loads(i, slot)

        @pl.when(i >= 2)
        def _():
            wait_stores(i - 2, slot)

        @pl.when(is_compute(i))
        def _():
            @pl.when((i == 0) | (be_ref[i] != be_ref[jnp.maximum(i - 1, 0)]))
            def _():
                wgb[...] = wg_ref[...].astype(BF16)
                wub[...] = wu_ref[...].astype(BF16)
                wdb[...] = wd_ref[...].astype(BF16)

            x = xbuf[slot]
            gate = _dot(x, wgb[...])
            up = _dot(x, wub[...])
            mid = (gate * jax.nn.sigmoid(gate) * up).astype(BF16)
            ybuf[slot] = _dot(mid, wdb[...]).astype(BF16)

        @pl.when(jnp.logical_not(is_compute(i)))
        def _():
            ybuf[slot] = jnp.zeros((EXPERT_BLOCK, D_MODEL), BF16)

        start_stores(i, slot)

        @pl.when(i == nb - 1)
        def _():
            @pl.when(i >= 1)
            def _():
                wait_stores(i - 1, 1 - slot)

            wait_stores(i, slot)


def _experts(be, src, nb, xs, wg, wu, wd, layer):
    n_blocks_max = be.shape[0]
    wspec = lambda a: pl.BlockSpec(
        (None,) + a.shape[1:],
        lambda i, be, src, nb: (layer * N_EXPERTS + jnp.minimum(be[i], N_EXPERTS - 1), 0, 0))
    grid_spec = pltpu.PrefetchScalarGridSpec(
        num_scalar_prefetch=3,
        grid=(n_blocks_max,),
        in_specs=[pl.BlockSpec(memory_space=pl.ANY), wspec(wg), wspec(wu), wspec(wd)],
        out_specs=pl.BlockSpec(memory_space=pl.ANY),
        scratch_shapes=[pltpu.VMEM((2, EXPERT_BLOCK, D_MODEL), BF16), pltpu.VMEM((2, EXPERT_BLOCK, D_MODEL), BF16),
                        pltpu.VMEM(wg.shape[1:], BF16), pltpu.VMEM(wu.shape[1:], BF16),
                        pltpu.VMEM(wd.shape[1:], BF16),
                        pltpu.SemaphoreType.DMA((2,)), pltpu.SemaphoreType.DMA((2,))],
    )
    return pl.pallas_call(
        _experts_kernel,
        grid_spec=grid_spec,
        out_shape=jax.ShapeDtypeStruct(xs.shape, BF16),
        compiler_params=_cparams(("arbitrary",)),
        name="experts",
    )(be, src, nb, xs, wg, wu, wd)


def _combine_kernel(h_ref, ys_ref, meta_ref, p_ref, pg_ref, gw_ref, pw_ref, fg_ref, o_ref, *, final):
    tm = h_ref.shape[0]
    rr = ys_ref.shape[0]
    meta = meta_ref[...]
    d0 = meta[:, 0:1].astype(I32)
    d1 = meta[:, 1:2].astype(I32)
    r_iota = lax.broadcasted_iota(I32, (tm, rr), 1)
    unperm = (jnp.where(r_iota == d0, meta[:, 2:3], 0.0) + jnp.where(r_iota == d1, meta[:, 3:4], 0.0)).astype(BF16)
    h2 = h_ref[...] + _dot(unperm, ys_ref[...])
    gate = jax.nn.sigmoid(_dot(_rms(h2, pg_ref[...]).astype(BF16), gw_ref[...]))
    h3 = h2 + gate * _dot(p_ref[...].astype(BF16), pw_ref[...])
    if final:
        h3 = _rms(h3, fg_ref[...])
    o_ref[...] = h3


def _combine(h1, ys, meta, p, layer, pg, gw, pw, fg, tm, final):
    t = h1.shape[0]
    nt = t // tm
    rr = ys.shape[0] // nt
    row = lambda n: pl.BlockSpec((tm, n), lambda i: (i, 0))
    full = lambda a: pl.BlockSpec(a.shape, lambda i: (0,) * a.ndim)
    return pl.pallas_call(
        functools.partial(_combine_kernel, final=final),
        grid=(nt,),
        in_specs=[row(D_MODEL), pl.BlockSpec((rr, D_MODEL), lambda i: (i, 0)), row(LANES),
                  pl.BlockSpec((tm, P_DIM), lambda i: (layer * nt + i, 0)),
                  full(pg), full(gw), full(pw), full(fg)],
        out_specs=row(D_MODEL),
        out_shape=jax.ShapeDtypeStruct((t, D_MODEL), F32),
        compiler_params=_cparams(("parallel",)),
        name="combine",
    )(h1, ys, meta, p, pg, gw, pw, fg)


def _pad_heads(a, axis):
    shape = a.shape
    a = a.reshape(shape[:axis] + (ML_HEADS, ML_HEAD_DIM) + shape[axis + 1:])
    pad = [(0, 0)] * a.ndim
    pad[axis + 1] = (0, LANES - ML_HEAD_DIM)
    a = jnp.pad(a, pad)
    return a.reshape(shape[:axis] + (ML_PAD,) + shape[axis + 1:])


def _layer_params(i, w_in, conv_w, conv_b, igate_b, fgate_b, mnorm_g, sgu_b, w_out, router_gw, router_gb,
                  router_ew, router_eb):
    w = w_in[i]
    s, m = SB_WIDTH, ML_WIDTH
    a_q, a_k, a_v = w[:, 0:s] * (SB_HEAD_DIM ** -0.5 * LOG2_E), w[:, s:2 * s], w[:, 2 * s:3 * s]
    o = 3 * s
    b_q, b_k, b_v, b_o = (w[:, o + k * m:o + (k + 1) * m] for k in range(4))
    o = o + 4 * m
    gates = w[:, o:o + 2 * ML_HEADS]
    c_uv = w[:, o + 2 * ML_HEADS:]
    w_r = jnp.concatenate([a_q, a_k, a_v, _pad_heads(b_q, 1), _pad_heads(b_k, 1), _pad_heads(b_v, 1),
                           _pad_heads(b_o, 1), c_uv,
                           jnp.pad(gates, ((0, 0), (0, LANES - 2 * ML_HEADS)))], axis=1).astype(BF16)
    cw = jnp.concatenate([_pad_heads(conv_w[i][:, :m], 1), _pad_heads(conv_w[i][:, m:], 1)], axis=1)
    cb = jnp.concatenate([_pad_heads(conv_b[i][:m], 0), _pad_heads(conv_b[i][m:], 0)])[None, :]
    gb = jnp.pad(jnp.concatenate([igate_b[i], fgate_b[i]]), (0, LANES - 2 * ML_HEADS))[None, :]
    mg = _pad_heads(mnorm_g[i], 0)[None, :]
    sgu_bias = jnp.repeat(sgu_b[i].T, SGU_GROUP_DIM, axis=1)
    wo = w_out[i]
    wa = wo[:s].astype(BF16)
    wb = _pad_heads(wo[s:s + m], 0).astype(BF16)
    wc = wo[s + m:].astype(BF16)
    rw = jnp.concatenate([router_ew[i].T, router_gw[i].T,
                          jnp.zeros((ROUTER_ROWS - N_EXPERTS - N_GROUPS, D_MODEL), F32)], axis=0)
    rwh = rw.astype(BF16)
    rwl = (rw - rwh.astype(F32)).astype(BF16)
    rb = jnp.concatenate([router_eb[i], router_gb[i],
                          jnp.zeros((ROUTER_ROWS - N_EXPERTS - N_GROUPS,), F32)])[:, None]
    return w_r, cw, cb, gb, mg, sgu_bias, wa, wb, wc, rwh, rwl, rb


def kernel(x, p, norm1_g, w_in, conv_w, conv_b, igate_b, fgate_b, mnorm_g, sb_out_g, sgu_ln_g, sgu_ln_b, sgu_w,
           sgu_b, sgu_out_g, w_out, norm2_g, router_gw, router_gb, router_ew, router_eb, w_gate, w_up, w_down,
           ple_norm_g, ple_gate_w, ple_proj_w, final_g, *, tile=512):
    batch, seq, d = x.shape
    depth = w_in.shape[0]
    t = batch * seq
    tm = min(tile, t)
    nt = t // tm
    rr = _dispatch_rows(tm)
    n_blocks_max = (nt * rr // CHUNK_ROWS) // EXPERT_BLOCK_CHUNKS + N_EXPERTS + 1
    h = x.astype(F32).reshape(t, d)
    p2 = p.reshape(depth * t, P_DIM)
    wg_all = w_gate.reshape((depth * N_EXPERTS,) + w_gate.shape[2:])
    wu_all = w_up.reshape((depth * N_EXPERTS,) + w_up.shape[2:])
    wd_all = w_down.reshape((depth * N_EXPERTS,) + w_down.shape[2:])
    for i in range(depth):
        (w_r, cw, cb, gb, mg, sgu_bias, wa, wb, wc, rwh, rwl, rb) = _layer_params(
            i, w_in, conv_w, conv_b, igate_b, fgate_b, mnorm_g, sgu_b, w_out, router_gw, router_gb,
            router_ew, router_eb)
        za, zqk, zvo, zc, zg = _in_proj(h, norm1_g[i][None, :], w_r, tm)
        ya = _sb_attn(za, batch, seq).reshape(t, SB_WIDTH)
        yb = _mlstm(zqk, zvo, zg, cw, cb, gb, mg, batch, seq).reshape(t, ML_PAD)
        yc = _sgu(zc, sgu_ln_g[i][None, :], sgu_ln_b[i][None, :], sgu_w[i], sgu_bias, sgu_out_g[i][None, :],
                  batch, seq, min(seq, 512)).reshape(t, SGU_WIDTH)
        h1, xs, meta, cnt = _mix_out(h, ya, yb, yc, sb_out_g[i][None, :], wa, wb, wc, norm2_g[i][None, :],
                                     rwh, rwl, rb, tm)
        be, src, nb = _chunk_tables(cnt[:, :, 0].astype(I32), rr, n_blocks_max)
        ys = _experts(be, src, nb, xs, wg_all, wu_all, wd_all, i)
        h = _combine(h1, ys, meta, p2, i, ple_norm_g[i][None, :], ple_gate_w[i].astype(BF16),
                     ple_proj_w[i].astype(BF16), final_g[None, :], tm, i == depth - 1)
    return h.reshape(batch, seq, d).astype(x.dtype)
```

```python
import functools

import jax
import jax.numpy as jnp
import numpy as np
from jax import lax
from jax.experimental import pallas as pl
from jax.experimental.pallas import tpu as pltpu

F32 = jnp.float32
BF16 = jnp.bfloat16
I32 = jnp.int32

D_MODEL = 1024
P_DIM = 256
EPS = 1e-6
LANES = 128
SB_HEAD_DIM = 64
SB_WIDTH = 384
SB_BLOCK = 128
SB_QTILE = 512
LOG2_E = 1.4426950408889634
ML_HEADS = 4
ML_HEAD_DIM = 96
ML_WIDTH = 384
ML_PAD = ML_HEADS * LANES
ML_CHUNK = 128
CONV_K = 4
SGU_WIDTH = 256
SGU_GROUPS = 4
SGU_GROUP_DIM = 64
SGU_BLOCK = 128
STREAM_CHUNK = 64
N_GROUPS = 4
EXPERTS_PER_GROUP = 8
N_EXPERTS = 32
D_EXPERT = 256
ROUTER_ROWS = 40
CHUNK_ROWS = 16
EXPERT_BLOCK_CHUNKS = 32
EXPERT_BLOCK = CHUNK_ROWS * EXPERT_BLOCK_CHUNKS

COL_A = 0
COL_BQK = 3 * SB_WIDTH
COL_BVO = COL_BQK + 2 * ML_PAD
COL_C = COL_BVO + 2 * ML_PAD
COL_G = COL_C + 2 * SGU_WIDTH
N_Z = COL_G + LANES

VMEM_LIMIT = 56 * 1024 * 1024


def _cparams(sem, vmem=VMEM_LIMIT):
    return pltpu.CompilerParams(dimension_semantics=sem, vmem_limit_bytes=vmem)


def _rms(x, g):
    return x * lax.rsqrt(jnp.mean(x * x, axis=-1, keepdims=True) + EPS) * g


def _sigmoid(x):
    return 0.5 * jnp.tanh(0.5 * x) + 0.5


def _split_bf16(x):
    hi = x.astype(BF16)
    lo = (x - hi.astype(F32)).astype(BF16)
    return hi, lo


def _dot(a, b):
    return jnp.dot(a, b, preferred_element_type=F32)


def _dot_nt(a, b):
    return lax.dot_general(a, b, (((1,), (1,)), ((), ())), preferred_element_type=F32)


def _dot_tn(a, b):
    return lax.dot_general(a, b, (((0,), (0,)), ((), ())), preferred_element_type=F32)


def _dot_split_lhs(x, m):
    hi, lo = _split_bf16(x)
    return _dot(hi, m) + _dot(lo, m)


def _dot_split_rhs(m, x):
    hi, lo = _split_bf16(x)
    return _dot(m, hi) + _dot(m, lo)


def _in_proj_kernel(x_ref, g_ref, w_ref, za_ref, zqk_ref, zvo_ref, zc_ref, zg_ref):
    hn = _rms(x_ref[...], g_ref[...]).astype(BF16)
    za_ref[...] = _dot(hn, w_ref[:, COL_A:COL_BQK]).astype(BF16)
    zqk_ref[...] = _dot(hn, w_ref[:, COL_BQK:COL_BVO])
    zvo_ref[...] = _dot(hn, w_ref[:, COL_BVO:COL_C]).astype(BF16)
    zc_ref[...] = _dot(hn, w_ref[:, COL_C:COL_G]).astype(BF16)
    zg_ref[...] = _dot(hn, w_ref[:, COL_G:N_Z])


def _in_proj(h, g, w, tm):
    t = h.shape[0]
    row = lambda n: pl.BlockSpec((tm, n), lambda i: (i, 0))
    full = lambda a: pl.BlockSpec(a.shape, lambda i: (0,) * a.ndim)
    return pl.pallas_call(
        _in_proj_kernel,
        grid=(t // tm,),
        in_specs=[row(D_MODEL), full(g), full(w)],
        out_specs=[row(3 * SB_WIDTH), row(2 * ML_PAD), row(2 * ML_PAD), row(2 * SGU_WIDTH), row(LANES)],
        out_shape=[jax.ShapeDtypeStruct((t, 3 * SB_WIDTH), BF16),
                   jax.ShapeDtypeStruct((t, 2 * ML_PAD), F32),
                   jax.ShapeDtypeStruct((t, 2 * ML_PAD), BF16),
                   jax.ShapeDtypeStruct((t, 2 * SGU_WIDTH), BF16),
                   jax.ShapeDtypeStruct((t, LANES), F32)],
        compiler_params=_cparams(("parallel",)),
        name="in_proj",
    )(h, g, w)


def _sb_kernel(q_ref, k_ref, v_ref, o_ref, acc_ref, car_ref, lsig_ref, after_ref, tot_ref):
    n = pl.program_id(2)
    blk = SB_BLOCK
    tq = q_ref.shape[0]
    sub = tq // blk
    lane_k = lax.broadcasted_iota(I32, (blk, LANES), 1)
    head0 = lane_k < SB_HEAD_DIM
    mj = lax.broadcasted_iota(I32, (2 * blk, 2 * blk), 0) % blk
    ms = lax.broadcasted_iota(I32, (2 * blk, 2 * blk), 1)
    suffix = jnp.where((mj > ms) | (ms >= blk), 1.0, 0.0).astype(BF16)

    acc_ref[...] = jnp.zeros_like(acc_ref)
    car_ref[...] = jnp.zeros_like(car_ref)

    def per_head(x):
        zero = jnp.zeros_like(x)
        return jnp.concatenate([jnp.where(head0, x, zero), jnp.where(head0, zero, x)], axis=0)

    def stage1(j, r0, masked, buf):
        start = pl.multiple_of(j * blk, blk)
        z = _dot_nt(q_ref[r0:, :], per_head(k_ref[pl.ds(start, blk), :]))
        sign = jnp.uint32(0x80000000)
        neg_abs = lax.bitcast_convert_type(lax.bitcast_convert_type(z, jnp.uint32) | sign, F32)
        e = jnp.exp2(neg_abs)
        log_sig = jnp.minimum(z, 0.0) - jnp.log2(1.0 + e)
        log1m = log_sig - z
        if masked:
            rows = tq - r0
            mask = (lax.broadcasted_iota(I32, (rows, 2 * blk), 1) % blk
                    < lax.broadcasted_iota(I32, (rows, 2 * blk), 0))
            log1m = jnp.where(mask, log1m, 0.0)
            log_sig = jnp.where(mask, log_sig, -jnp.inf)
        lsig_ref[buf, r0:, :] = log_sig
        hi, lo = _split_bf16(log1m)
        for hh in range(2):
            cs = slice(hh * blk, (hh + 1) * blk)
            ca = _dot(jnp.concatenate([hi[:, cs], lo[:, cs]], axis=1), suffix)
            after_ref[buf, r0:, cs] = ca[:, :blk]
            tot_ref[buf, r0:, cs] = ca[:, blk:]

    def stage2(j, r0, buf):
        start = pl.multiple_of(j * blk, blk)
        arg = lsig_ref[buf, r0:, :] + after_ref[buf, r0:, :] + car_ref[r0:, :]
        acc_ref[r0:, :] += _dot(jnp.exp2(arg).astype(BF16), per_head(v_ref[pl.ds(start, blk), :]))
        car_ref[r0:, :] += tot_ref[buf, r0:, :]

    for kj in reversed(range(sub)):
        stage1(n * sub + kj, kj * blk, True, kj % 2)
        stage2(n * sub + kj, kj * blk, kj % 2)

    top = n * sub - 1

    @pl.when(n > 0)
    def _():
        stage1(top, 0, False, 0)

    def body(i, c):
        j0 = top - 2 * i
        stage1(j0 - 1, 0, False, 1)
        stage2(j0, 0, 0)
        stage1(jnp.maximum(j0 - 2, 0), 0, False, 0)
        stage2(j0 - 1, 0, 1)
        return c

    lax.fori_loop(0, n * (sub // 2), body, 0)
    o_ref[...] = acc_ref[...].astype(o_ref.dtype)


def _sb_attn(za, batch, seq):
    za3 = za.reshape(batch, seq, 3 * SB_WIDTH)
    pairs = SB_WIDTH // LANES
    tq = min(SB_QTILE, seq)
    return pl.pallas_call(
        _sb_kernel,
        grid=(batch, pairs, seq // tq),
        in_specs=[pl.BlockSpec((None, tq, LANES), lambda b, p, n: (b, n, p)),
                  pl.BlockSpec((None, seq, LANES), lambda b, p, n: (b, 0, pairs + p)),
                  pl.BlockSpec((None, seq, LANES), lambda b, p, n: (b, 0, 2 * pairs + p))],
        out_specs=pl.BlockSpec((None, tq, LANES), lambda b, p, n: (b, n, p)),
        out_shape=jax.ShapeDtypeStruct((batch, seq, SB_WIDTH), BF16),
        scratch_shapes=[pltpu.VMEM((tq, LANES), F32),
                        pltpu.VMEM((tq, 2 * LANES), F32),
                        pltpu.VMEM((2, tq, 2 * LANES), F32),
                        pltpu.VMEM((2, tq, 2 * LANES), F32),
                        pltpu.VMEM((2, tq, 2 * LANES), F32)],
        compiler_params=_cparams(("parallel", "parallel", "arbitrary")),
        name="sb_attn",
    )(za3, za3, za3)


def _mlstm_kernel(qk_ref, vo_ref, g_ref, cw_ref, cb_ref, gb_ref, mg_ref, o_ref,
                  q_s, k_s, vt_s, cum_s, gmb_s, pm_s, gmbt_s, ct_s, m_s):
    seq = qk_ref.shape[0]
    L = ML_CHUNK
    cw = cw_ref[...]
    cb = cb_ref[...]
    rows = lax.broadcasted_iota(I32, (L, 2 * ML_PAD), 0)

    def conv_silu(r0, taps):
        y = taps[0] * cw[CONV_K - 1:CONV_K, :] + cb
        for d in range(1, CONV_K):
            y = y + taps[d] * cw[CONV_K - 1 - d:CONV_K - d, :]
        y = y * _sigmoid(y)
        q_s[pl.ds(r0, L), :] = y[:, :ML_PAD].astype(BF16)
        k_s[pl.ds(r0, L), :] = (y[:, ML_PAD:] * (ML_HEAD_DIM ** -0.5)).astype(BF16)

    ct_s[...] = jnp.zeros_like(ct_s)
    m_s[...] = jnp.zeros_like(m_s)

    ti = lax.broadcasted_iota(I32, (L, L), 0)
    si = lax.broadcasted_iota(I32, (L, L), 1)
    causal = si <= ti
    tril = jnp.where(causal, 1.0, 0.0).astype(BF16)
    lane = lax.broadcasted_iota(I32, (L, LANES), 1)
    row = lax.broadcasted_iota(I32, (L, LANES), 0)
    ones2 = jnp.full((2 * LANES, LANES), 1.0, BF16)
    gb = gb_ref[...]
    mg = mg_ref[...]
    norm_lane = LANES - 1

    def gates(c):
        r0 = pl.multiple_of(c * L, L)
        g = g_ref[pl.ds(r0, L), :] + gb
        logf = jnp.minimum(g, 0.0) - jnp.log1p(jnp.exp(-jnp.abs(g)))
        cum = pltpu.roll(_dot_split_rhs(tril, logf), LANES - ML_HEADS, axis=1)
        gmb = g - cum
        pm = gmb
        step = 1
        while step < L:
            pm = jnp.maximum(pm, jnp.where(row >= step, pltpu.roll(pm, step, axis=0), -jnp.inf))
            step *= 2
        cum_s[pl.ds(r0, L), :] = cum
        gmb_s[pl.ds(r0, L), :] = gmb
        pm_s[pl.ds(r0, L), :] = pm
        gmbt_s[c] = jnp.transpose(gmb)
        for h in range(ML_HEADS):
            v = vo_ref[pl.ds(r0, L), h * LANES:(h + 1) * LANES].astype(F32)
            vt_s[h, c] = jnp.transpose(jnp.where(lane == norm_lane, 1.0, v)).astype(BF16)

    x0 = qk_ref[0:L, :]
    conv_silu(0, [x0] + [jnp.where(rows >= d, pltpu.roll(x0, d, axis=0), 0.0) for d in range(1, CONV_K)])

    def conv_chunk(c, carry):
        r0 = pl.multiple_of(c * L, L)
        xe = qk_ref[pl.ds(r0 - 8, L + 8), :]
        conv_silu(r0, [xe[8 - d:8 - d + L, :] for d in range(CONV_K)])
        return carry

    lax.fori_loop(1, seq // L, conv_chunk, 0)

    def gate_chunk(c, carry):
        gates(c)
        return carry

    lax.fori_loop(0, seq // L, gate_chunk, 0)

    heads = range(ML_HEADS)

    def chunk(c, carry):
        r0 = pl.multiple_of(c * L, L)
        cum = cum_s[pl.ds(r0, L), :]
        gmb = gmb_s[pl.ds(r0, L), :]
        pm = pm_s[pl.ds(r0, L), :]
        gmb_t = gmbt_s[c]
        qc = [q_s[pl.ds(r0, L), h * LANES:(h + 1) * LANES] for h in heads]
        kc = [k_s[pl.ds(r0, L), h * LANES:(h + 1) * LANES] for h in heads]
        vt = [vt_s[h, c] for h in heads]
        m_in = [m_s[h][0:1, 0:1] for h in heads]
        ct = [ct_s[h] for h in heads]
        s_raw = [_dot_nt(qc[h], kc[h]) for h in heads]
        mm = [jnp.maximum(m_in[h], pm[:, h:h + 1]) for h in heads]
        inter = [jnp.exp(m_in[h] - mm[h]) for h in heads]
        num = []
        for h in heads:
            wd = jnp.where(causal, jnp.exp(gmb_t[h:h + 1, :] - mm[h]), 0.0)
            lhs = jnp.concatenate([(s_raw[h] * wd).astype(BF16),
                                   (qc[h].astype(F32) * inter[h]).astype(BF16)], axis=1)
            rhs_t = jnp.concatenate([vt[h], ct[h].astype(BF16)], axis=1)
            num.append(_dot_nt(lhs, rhs_t))
        hh = []
        ms = []
        for h in heads:
            den = jnp.maximum(jnp.abs(num[h][:, norm_lane:]), jnp.exp(-(cum[:, h:h + 1] + mm[h])))
            x = jnp.where(lane == norm_lane, 0.0, num[h] * (1.0 / den))
            hi, lo = _split_bf16(x * x)
            hh.append(x)
            ms.append(_dot(jnp.concatenate([hi, lo], axis=1), ones2) * (1.0 / ML_HEAD_DIM))
        for h in heads:
            cs = slice(h * LANES, (h + 1) * LANES)
            oc = vo_ref[pl.ds(r0, L), ML_PAD + h * LANES:ML_PAD + (h + 1) * LANES].astype(F32)
            o_ref[pl.ds(r0, L), cs] = (hh[h] * lax.rsqrt(ms[h] + EPS) * mg[:, cs]
                                       * _sigmoid(oc)).astype(o_ref.dtype)
        for h in heads:
            mm_last = mm[h][L - 1:L, :]
            kw = (kc[h].astype(F32) * jnp.exp(gmb[:, h:h + 1] - mm_last)).astype(BF16)
            ct_s[h] = inter[h][L - 1:L, :] * ct[h] + _dot(vt[h], kw)
            m_s[h] = jnp.broadcast_to(cum[L - 1:L, h:h + 1] + mm_last, (8, LANES))
        return carry

    lax.fori_loop(0, seq // L, chunk, 0)


def _mlstm(zqk, zvo, zg, cw, cb, gb, mg, batch, seq):
    full = lambda a: pl.BlockSpec(a.shape, lambda b: (0,) * a.ndim)
    seq_blk = lambda n: pl.BlockSpec((None, seq, n), lambda b: (b, 0, 0))
    return pl.pallas_call(
        _mlstm_kernel,
        grid=(batch,),
        in_specs=[seq_blk(2 * ML_PAD), seq_blk(2 * ML_PAD), seq_blk(LANES),
                  full(cw), full(cb), full(gb), full(mg)],
        out_specs=seq_blk(ML_PAD),
        out_shape=jax.ShapeDtypeStruct((batch, seq, ML_PAD), BF16),
        scratch_shapes=[pltpu.VMEM((seq, ML_PAD), BF16), pltpu.VMEM((seq, ML_PAD), BF16),
                        pltpu.VMEM((ML_HEADS, seq // ML_CHUNK, LANES, ML_CHUNK), BF16),
                        pltpu.VMEM((seq, LANES), F32), pltpu.VMEM((seq, LANES), F32),
                        pltpu.VMEM((seq, LANES), F32),
                        pltpu.VMEM((seq // ML_CHUNK, LANES, ML_CHUNK), F32),
                        pltpu.VMEM((ML_HEADS, LANES, LANES), F32),
                        pltpu.VMEM((ML_HEADS, 8, LANES), F32)],
        compiler_params=_cparams(("parallel",)),
        name="mlstm",
    )(zqk.reshape(batch, seq, 2 * ML_PAD), zvo.reshape(batch, seq, 2 * ML_PAD),
      zg.reshape(batch, seq, LANES), cw, cb, gb, mg)


def _sgu_kernel(z_ref, lg_ref, lb_ref, w_ref, b_ref, og_ref, o_ref):
    rows = z_ref.shape[0]
    W = SGU_WIDTH
    gi = lax.broadcasted_iota(I32, (W, W), 0) // SGU_GROUP_DIM
    gj = lax.broadcasted_iota(I32, (W, W), 1) // SGU_GROUP_DIM
    avg = jnp.where(gi == gj, 1.0 / SGU_GROUP_DIM, 0.0).astype(BF16)
    ti = lax.broadcasted_iota(I32, (SGU_BLOCK, SGU_BLOCK), 0) // STREAM_CHUNK
    si = lax.broadcasted_iota(I32, (SGU_BLOCK, SGU_BLOCK), 1) // STREAM_CHUNK
    chunk_causal = si <= ti
    lane_group = lax.broadcasted_iota(I32, (SGU_BLOCK, W), 1) // SGU_GROUP_DIM
    for r in range(rows // SGU_BLOCK):
        rs = slice(r * SGU_BLOCK, (r + 1) * SGU_BLOCK)
        u = jax.nn.gelu(z_ref[rs, :W].astype(F32))
        v = jax.nn.gelu(z_ref[rs, W:].astype(F32))
        mu = _dot_split_lhs(v, avg)
        vc = v - mu
        var = _dot_split_lhs(vc * vc, avg)
        vn = (vc * lax.rsqrt(var + EPS) * lg_ref[...] + lb_ref[...]).astype(BF16)
        mixed = b_ref[...]
        for g in range(SGU_GROUPS):
            wg = jnp.where(chunk_causal, w_ref[g], 0.0).astype(BF16)
            mixed = mixed + _dot(wg, jnp.where(lane_group == g, vn, jnp.zeros_like(vn)))
        o_ref[rs, :] = _rms(u * mixed, og_ref[...]).astype(o_ref.dtype)


def _sgu(zc, lg, lb, w, bias, og, batch, seq, rows):
    full = lambda a: pl.BlockSpec(a.shape, lambda b, r: (0,) * a.ndim)
    return pl.pallas_call(
        _sgu_kernel,
        grid=(batch, seq // rows),
        in_specs=[pl.BlockSpec((None, rows, 2 * SGU_WIDTH), lambda b, r: (b, r, 0)),
                  full(lg), full(lb), full(w), full(bias), full(og)],
        out_specs=pl.BlockSpec((None, rows, SGU_WIDTH), lambda b, r: (b, r, 0)),
        out_shape=jax.ShapeDtypeStruct((batch, seq, SGU_WIDTH), BF16),
        compiler_params=_cparams(("parallel", "parallel")),
        name="sgu",
    )(zc.reshape(batch, seq, 2 * SGU_WIDTH), lg, lb, w, bias, og)


def _dispatch_rows(tm):
    return 2 * tm + N_EXPERTS * (CHUNK_ROWS - 1) + (N_EXPERTS * (CHUNK_ROWS - 1)) % CHUNK_ROWS


def _mix_out_kernel(h_ref, ya_ref, yb_ref, yc_ref, ag_ref, wa_ref, wb_ref, wc_ref, n2_ref,
                    rwh_ref, rwl_ref, rb_ref, h1_ref, xs_ref, meta_ref, cnt_ref):
    tm = h_ref.shape[0]
    rr = xs_ref.shape[0]
    ya = _rms(ya_ref[...].astype(F32), ag_ref[...]).astype(BF16)
    h1 = h_ref[...] + _dot(ya, wa_ref[...]) + _dot(yb_ref[...], wb_ref[...]) + _dot(yc_ref[...], wc_ref[...])
    h1_ref[...] = h1
    xn = _rms(h1, n2_ref[...])
    x_hi, x_lo = _split_bf16(xn)
    logits = (_dot_nt(rwh_ref[...], x_hi) + _dot_nt(rwl_ref[...], x_hi) + _dot_nt(rwh_ref[...], x_lo)
              + rb_ref[...])
    e_log = logits[:N_EXPERTS]
    g_log = logits[N_EXPERTS:N_EXPERTS + N_GROUPS]
    g_iota = lax.broadcasted_iota(I32, (N_GROUPS, tm), 0)
    g_max = jnp.max(g_log, axis=0, keepdims=True)
    g_top = jnp.min(jnp.where(g_log == g_max, g_iota, N_GROUPS), axis=0, keepdims=True)
    g_w = 1.0 / jnp.sum(jnp.exp(g_log - g_max), axis=0, keepdims=True)
    e_iota = lax.broadcasted_iota(I32, (N_EXPERTS, tm), 0)
    val = jnp.where(e_iota // EXPERTS_PER_GROUP == g_top, e_log, -jnp.inf)
    m1 = jnp.max(val, axis=0, keepdims=True)
    i1 = jnp.min(jnp.where(val == m1, e_iota, N_EXPERTS), axis=0, keepdims=True)
    val2 = jnp.where(e_iota == i1, -jnp.inf, val)
    m2 = jnp.max(val2, axis=0, keepdims=True)
    i2 = jnp.min(jnp.where(val2 == m2, e_iota, N_EXPERTS), axis=0, keepdims=True)
    e2 = jnp.exp(m2 - m1)
    w0 = g_w / (1.0 + e2)
    w1 = g_w * e2 / (1.0 + e2)
    oh0 = e_iota == i1
    oh1 = e_iota == i2
    oh = jnp.concatenate([jnp.where(oh0, 1.0, 0.0), jnp.where(oh1, 1.0, 0.0)], axis=0).astype(BF16)
    ta = lax.broadcasted_iota(I32, (tm, tm), 0)
    tb = lax.broadcasted_iota(I32, (tm, tm), 1)
    before = jnp.where(ta < tb, 1.0, 0.0).astype(BF16)
    rank = _dot(oh, before)
    c0 = jnp.sum(jnp.where(oh0, 1.0, 0.0), axis=1, keepdims=True)
    c1 = jnp.sum(jnp.where(oh1, 1.0, 0.0), axis=1, keepdims=True)
    nchunk = jnp.floor((c0 + c1 + (CHUNK_ROWS - 1)) * (1.0 / CHUNK_ROWS))
    nchunk_b = jnp.broadcast_to(nchunk, (N_EXPERTS, LANES))
    ea = lax.broadcasted_iota(I32, (N_EXPERTS, N_EXPERTS), 0)
    eb = lax.broadcasted_iota(I32, (N_EXPERTS, N_EXPERTS), 1)
    lower = jnp.where(eb < ea, 1.0, 0.0).astype(BF16)
    loc = CHUNK_ROWS * _dot(lower, nchunk_b.astype(BF16))[:, 0:1]
    dest0 = jnp.sum(jnp.where(oh0, loc + rank[:N_EXPERTS], 0.0), axis=0, keepdims=True)
    dest1 = jnp.sum(jnp.where(oh1, loc + c0 + rank[N_EXPERTS:], 0.0), axis=0, keepdims=True)
    r_iota = lax.broadcasted_iota(I32, (rr, tm), 0)
    d0 = dest0.astype(I32)
    d1 = dest1.astype(I32)
    perm = jnp.where(r_iota == d0, 1.0, jnp.where(r_iota == d1, 1.0, 0.0)).astype(BF16)
    xs_ref[...] = _dot(perm, x_hi).astype(BF16)
    cnt_ref[...] = nchunk_b
    m_iota = lax.broadcasted_iota(I32, (8, tm), 0)
    meta = jnp.where(m_iota == 0, dest0, jnp.where(m_iota == 1, dest1, jnp.where(m_iota == 2, w0,
                     jnp.where(m_iota == 3, w1, 0.0))))
    meta = jnp.concatenate([meta, jnp.zeros((LANES - 8, tm), F32)], axis=0)
    meta_ref[...] = jnp.transpose(meta)


def _mix_out(h, ya, yb, yc, ag, wa, wb, wc, n2, rwh, rwl, rb, tm):
    t = h.shape[0]
    nt = t // tm
    rr = _dispatch_rows(tm)
    row = lambda n: pl.BlockSpec((tm, n), lambda i: (i, 0))
    full = lambda a: pl.BlockSpec(a.shape, lambda i: (0,) * a.ndim)
    return pl.pallas_call(
        _mix_out_kernel,
        grid=(nt,),
        in_specs=[row(D_MODEL), row(SB_WIDTH), row(ML_PAD), row(SGU_WIDTH), full(ag), full(wa), full(wb),
                  full(wc), full(n2), full(rwh), full(rwl), full(rb)],
        out_specs=[row(D_MODEL), pl.BlockSpec((rr, D_MODEL), lambda i: (i, 0)), row(LANES),
                   pl.BlockSpec((None, N_EXPERTS, LANES), lambda i: (i, 0, 0))],
        out_shape=[jax.ShapeDtypeStruct((t, D_MODEL), F32),
                   jax.ShapeDtypeStruct((nt * rr, D_MODEL), BF16),
                   jax.ShapeDtypeStruct((t, LANES), F32),
                   jax.ShapeDtypeStruct((nt, N_EXPERTS, LANES), F32)],
        compiler_params=_cparams(("parallel",)),
        name="mix_out",
    )(h, ya, yb, yc, ag, wa, wb, wc, n2, rwh, rwl, rb)


def _chunk_tables(cnt, rr, n_blocks_max):
    nt = cnt.shape[0]
    ne = N_EXPERTS + 1
    tail = rr // CHUNK_ROWS - jnp.sum(cnt, axis=1)
    cnt = jnp.concatenate([cnt, tail[:, None]], axis=1)
    loc = (jnp.cumsum(cnt, axis=1) - cnt) * CHUNK_ROWS
    cnt_e = cnt.T
    cum_e = jnp.cumsum(cnt_e, axis=1)
    total = cum_e[:, -1]
    nblk = (total + EXPERT_BLOCK_CHUNKS - 1) // EXPERT_BLOCK_CHUNKS
    bend = jnp.cumsum(nblk)
    bstart = bend - nblk
    n_blocks = bend[-1]
    i = jnp.arange(n_blocks_max, dtype=I32)
    be = jnp.minimum(jnp.sum(bend[None, :] <= i[:, None], axis=1), ne - 1).astype(I32)
    oh_b = be[:, None] == jnp.arange(ne, dtype=I32)[None, :]
    pick = lambda v: jnp.sum(jnp.where(oh_b, v[None, :], 0), axis=1)
    pick2 = lambda m: jnp.sum(jnp.where(oh_b[:, :, None], m[None, :, :], 0), axis=1)
    q = (i - pick(bstart))[:, None] * EXPERT_BLOCK_CHUNKS + jnp.arange(EXPERT_BLOCK_CHUNKS, dtype=I32)[None, :]
    valid = (q < pick(total)[:, None]) & (i < n_blocks)[:, None]
    cum_b = pick2(cum_e)
    base = jnp.arange(nt, dtype=I32)[None, :] * rr + loc.T - (cum_e - cnt_e) * CHUNK_ROWS
    base_b = pick2(base)
    j = jnp.minimum(jnp.sum(cum_b[:, None, :] <= q[:, :, None], axis=-1), nt - 1)
    oh_j = j[:, :, None] == jnp.arange(nt, dtype=I32)[None, None, :]
    row = jnp.sum(jnp.where(oh_j, base_b[:, None, :], 0), axis=-1) + q * CHUNK_ROWS
    spare = nt * rr + ((i % 2)[:, None] * EXPERT_BLOCK_CHUNKS
                       + jnp.arange(EXPERT_BLOCK_CHUNKS, dtype=I32)[None, :]) * CHUNK_ROWS
    src = jnp.where(valid, row, 0).astype(I32)
    dst = jnp.where(valid, row, spare).astype(I32)
    return be, src.reshape(-1), dst.reshape(-1), n_blocks.reshape(1).astype(I32)


def _experts_kernel(be_ref, src_ref, dst_ref, nb_ref, xs_ref, wg_ref, wu_ref, wd_ref, ys_ref,
                    xbuf, ybuf, wgb, wub, wdb, sem_in, sem_out):
    i = pl.program_id(0)
    nb = nb_ref[0]
    slot = lax.rem(i, 2)

    def is_compute(b):
        return be_ref[b] < N_EXPERTS

    def start_loads(b, sl):
        @pl.when(is_compute(b))
        def _():
            for s in range(EXPERT_BLOCK_CHUNKS):
                src = pl.multiple_of(src_ref[b * EXPERT_BLOCK_CHUNKS + s], CHUNK_ROWS)
                pltpu.make_async_copy(xs_ref.at[pl.ds(src, CHUNK_ROWS), :],
                                      xbuf.at[sl, s * CHUNK_ROWS:(s + 1) * CHUNK_ROWS, :], sem_in.at[sl]).start()

    def wait_loads(b, sl):
        @pl.when(is_compute(b))
        def _():
            pltpu.make_async_copy(xs_ref.at[0:EXPERT_BLOCK, :], xbuf.at[sl], sem_in.at[sl]).wait()

    def start_stores(b, sl):
        for s in range(EXPERT_BLOCK_CHUNKS):
            dst = pl.multiple_of(dst_ref[b * EXPERT_BLOCK_CHUNKS + s], CHUNK_ROWS)
            pltpu.make_async_copy(ybuf.at[sl, s * CHUNK_ROWS:(s + 1) * CHUNK_ROWS, :],
                                  ys_ref.at[pl.ds(dst, CHUNK_ROWS), :], sem_out.at[sl]).start()

    def wait_stores(sl):
        pltpu.make_async_copy(ybuf.at[sl], ys_ref.at[0:EXPERT_BLOCK, :], sem_out.at[sl]).wait()

    @pl.when(i == 0)
    def _():
        ybuf[...] = jnp.zeros(ybuf.shape, BF16)
        real_rows = ys_ref.shape[0] - 2 * EXPERT_BLOCK
        for par in range(2):
            fill = pltpu.make_async_copy(
                ybuf.at[par], ys_ref.at[real_rows + par * EXPERT_BLOCK:real_rows + (par + 1) * EXPERT_BLOCK, :],
                sem_out.at[par])
            fill.start()
            fill.wait()

    @pl.when(i < nb)
    def _():
        @pl.when(i == 0)
        def _():
            start_loads(0, 0)

        @pl.when(i + 1 < nb)
        def _():
            start_loads(i + 1, 1 - slot)

        wait_loads(i, slot)

        @pl.when(i >= 2)
        def _():
            wait_stores(slot)

        @pl.when(is_compute(i))
        def _():
            @pl.when((i == 0) | (be_ref[i] != be_ref[jnp.maximum(i - 1, 0)]))
            def _():
                wgb[...] = wg_ref[...].astype(BF16)
                wub[...] = wu_ref[...].astype(BF16)
                wdb[...] = wd_ref[...].astype(BF16)

            x = xbuf[slot]
            gate = _dot(x, wgb[...])
            up = _dot(x, wub[...])
            mid = (gate * _sigmoid(gate) * up).astype(BF16)
            ybuf[slot] = _dot(mid, wdb[...]).astype(BF16)

        @pl.when(jnp.logical_not(is_compute(i)))
        def _():
            ybuf[slot] = jnp.zeros((EXPERT_BLOCK, D_MODEL), BF16)

        start_stores(i, slot)

        @pl.when(i == nb - 1)
        def _():
            @pl.when(i >= 1)
            def _():
                wait_stores(1 - slot)

            wait_stores(slot)


def _experts(be, src, dst, nb, xs, wg, wu, wd, layer):
    n_blocks_max = be.shape[0]
    wspec = lambda a: pl.BlockSpec(
        (None,) + a.shape[1:],
        lambda i, be, src, dst, nb: (layer * N_EXPERTS + jnp.minimum(be[i], N_EXPERTS - 1), 0, 0))
    grid_spec = pltpu.PrefetchScalarGridSpec(
        num_scalar_prefetch=4,
        grid=(n_blocks_max,),
        in_specs=[pl.BlockSpec(memory_space=pl.ANY), wspec(wg), wspec(wu), wspec(wd)],
        out_specs=pl.BlockSpec(memory_space=pl.ANY),
        scratch_shapes=[pltpu.VMEM((2, EXPERT_BLOCK, D_MODEL), BF16), pltpu.VMEM((2, EXPERT_BLOCK, D_MODEL), BF16),
                        pltpu.VMEM(wg.shape[1:], BF16), pltpu.VMEM(wu.shape[1:], BF16),
                        pltpu.VMEM(wd.shape[1:], BF16),
                        pltpu.SemaphoreType.DMA((2,)), pltpu.SemaphoreType.DMA((2,))],
    )
    return pl.pallas_call(
        _experts_kernel,
        grid_spec=grid_spec,
        out_shape=jax.ShapeDtypeStruct((xs.shape[0] + 2 * EXPERT_BLOCK, D_MODEL), BF16),
        compiler_params=_cparams(("arbitrary",)),
        name="experts",
    )(be, src, dst, nb, xs, wg, wu, wd)


def _combine_kernel(h_ref, ys_ref, meta_ref, p_ref, pg_ref, gw_ref, pw_ref, fg_ref, o_ref, *, final):
    tm = h_ref.shape[0]
    rr = ys_ref.shape[0]
    meta = meta_ref[...]
    d0 = meta[:, 0:1].astype(I32)
    d1 = meta[:, 1:2].astype(I32)
    r_iota = lax.broadcasted_iota(I32, (tm, rr), 1)
    unperm = (jnp.where(r_iota == d0, meta[:, 2:3], 0.0) + jnp.where(r_iota == d1, meta[:, 3:4], 0.0)).astype(BF16)
    h2 = h_ref[...] + _dot(unperm, ys_ref[...])
    gate = _sigmoid(_dot(_rms(h2, pg_ref[...]).astype(BF16), gw_ref[...]))
    h3 = h2 + gate * _dot(p_ref[...].astype(BF16), pw_ref[...])
    if final:
        h3 = _rms(h3, fg_ref[...])
    o_ref[...] = h3


def _combine(h1, ys, meta, p, layer, pg, gw, pw, fg, tm, final):
    t = h1.shape[0]
    nt = t // tm
    rr = _dispatch_rows(tm)
    row = lambda n: pl.BlockSpec((tm, n), lambda i: (i, 0))
    full = lambda a: pl.BlockSpec(a.shape, lambda i: (0,) * a.ndim)
    return pl.pallas_call(
        functools.partial(_combine_kernel, final=final),
        grid=(nt,),
        in_specs=[row(D_MODEL), pl.BlockSpec((rr, D_MODEL), lambda i: (i, 0)), row(LANES),
                  pl.BlockSpec((tm, P_DIM), lambda i: (layer * nt + i, 0)),
                  full(pg), full(gw), full(pw), full(fg)],
        out_specs=row(D_MODEL),
        out_shape=jax.ShapeDtypeStruct((t, D_MODEL), F32),
        compiler_params=_cparams(("parallel",)),
        name="combine",
    )(h1, ys, meta, p, pg, gw, pw, fg)


def _pad_heads(a, axis):
    shape = a.shape
    a = a.reshape(shape[:axis] + (ML_HEADS, ML_HEAD_DIM) + shape[axis + 1:])
    pad = [(0, 0)] * a.ndim
    pad[axis + 1] = (0, LANES - ML_HEAD_DIM)
    a = jnp.pad(a, pad)
    return a.reshape(shape[:axis] + (ML_PAD,) + shape[axis + 1:])


def _layer_params(i, w_in, conv_w, conv_b, igate_b, fgate_b, mnorm_g, sgu_b, w_out, router_gw, router_gb,
                  router_ew, router_eb):
    w = w_in[i]
    s, m = SB_WIDTH, ML_WIDTH
    a_q, a_k, a_v = w[:, 0:s] * (SB_HEAD_DIM ** -0.5 * LOG2_E), w[:, s:2 * s], w[:, 2 * s:3 * s]
    o = 3 * s
    b_q, b_k, b_v, b_o = (w[:, o + k * m:o + (k + 1) * m] for k in range(4))
    o = o + 4 * m
    gates = w[:, o:o + 2 * ML_HEADS]
    c_uv = w[:, o + 2 * ML_HEADS:]
    w_r = jnp.concatenate([a_q, a_k, a_v, _pad_heads(b_q, 1), _pad_heads(b_k, 1), _pad_heads(b_v, 1),
                           _pad_heads(b_o, 1), c_uv,
                           jnp.pad(gates, ((0, 0), (0, LANES - 2 * ML_HEADS)))], axis=1).astype(BF16)
    cw = jnp.concatenate([_pad_heads(conv_w[i][:, :m], 1), _pad_heads(conv_w[i][:, m:], 1)], axis=1)
    cb = jnp.concatenate([_pad_heads(conv_b[i][:m], 0), _pad_heads(conv_b[i][m:], 0)])[None, :]
    gb = jnp.pad(jnp.concatenate([igate_b[i], fgate_b[i]]), (0, LANES - 2 * ML_HEADS))[None, :]
    mg = _pad_heads(mnorm_g[i], 0)[None, :]
    sgu_bias = jnp.repeat(sgu_b[i].T, SGU_GROUP_DIM, axis=1)
    wo = w_out[i]
    wa = wo[:s].astype(BF16)
    wb = _pad_heads(wo[s:s + m], 0).astype(BF16)
    wc = wo[s + m:].astype(BF16)
    rw = jnp.concatenate([router_ew[i].T, router_gw[i].T,
                          jnp.zeros((ROUTER_ROWS - N_EXPERTS - N_GROUPS, D_MODEL), F32)], axis=0)
    rwh = rw.astype(BF16)
    rwl = (rw - rwh.astype(F32)).astype(BF16)
    rb = jnp.concatenate([router_eb[i], router_gb[i],
                          jnp.zeros((ROUTER_ROWS - N_EXPERTS - N_GROUPS,), F32)])[:, None]
    return w_r, cw, cb, gb, mg, sgu_bias, wa, wb, wc, rwh, rwl, rb


def kernel(x, p, norm1_g, w_in, conv_w, conv_b, igate_b, fgate_b, mnorm_g, sb_out_g, sgu_ln_g, sgu_ln_b, sgu_w,
           sgu_b, sgu_out_g, w_out, norm2_g, router_gw, router_gb, router_ew, router_eb, w_gate, w_up, w_down,
           ple_norm_g, ple_gate_w, ple_proj_w, final_g, *, tile=512):
    batch, seq, d = x.shape
    depth = w_in.shape[0]
    t = batch * seq
    tm = min(tile, t)
    nt = t // tm
    rr = _dispatch_rows(tm)
    n_blocks_max = (nt * rr // CHUNK_ROWS) // EXPERT_BLOCK_CHUNKS + N_EXPERTS + 1
    h = x.astype(F32).reshape(t, d)
    p2 = p.reshape(depth * t, P_DIM)
    wg_all = w_gate.reshape((depth * N_EXPERTS,) + w_gate.shape[2:])
    wu_all = w_up.reshape((depth * N_EXPERTS,) + w_up.shape[2:])
    wd_all = w_down.reshape((depth * N_EXPERTS,) + w_down.shape[2:])
    for i in range(depth):
        (w_r, cw, cb, gb, mg, sgu_bias, wa, wb, wc, rwh, rwl, rb) = _layer_params(
            i, w_in, conv_w, conv_b, igate_b, fgate_b, mnorm_g, sgu_b, w_out, router_gw, router_gb,
            router_ew, router_eb)
        za, zqk, zvo, zc, zg = _in_proj(h, norm1_g[i][None, :], w_r, tm)
        ya = _sb_attn(za, batch, seq).reshape(t, SB_WIDTH)
        yb = _mlstm(zqk, zvo, zg, cw, cb, gb, mg, batch, seq).reshape(t, ML_PAD)
        yc = _sgu(zc, sgu_ln_g[i][None, :], sgu_ln_b[i][None, :], sgu_w[i], sgu_bias, sgu_out_g[i][None, :],
                  batch, seq, min(seq, 512)).reshape(t, SGU_WIDTH)
        h1, xs, meta, cnt = _mix_out(h, ya, yb, yc, sb_out_g[i][None, :], wa, wb, wc, norm2_g[i][None, :],
                                     rwh, rwl, rb, tm)
        be, src, dst, nb = _chunk_tables(cnt[:, :, 0].astype(I32), rr, n_blocks_max)
        ys = _experts(be, src, dst, nb, xs, wg_all, wu_all, wd_all, i)
        h = _combine(h1, ys, meta, p2, i, ple_norm_g[i][None, :], ple_gate_w[i].astype(BF16),
                     ple_proj_w[i].astype(BF16), final_g[None, :], tm, i == depth - 1)
    return h.reshape(batch, seq, d).astype(x.dtype)
```

```python
import functools

import jax
import jax.numpy as jnp
import numpy as np
from jax import lax
from jax.experimental import pallas as pl
from jax.experimental.pallas import tpu as pltpu

F32 = jnp.float32
BF16 = jnp.bfloat16
I32 = jnp.int32

D_MODEL = 1024
P_DIM = 256
EPS = 1e-6
LANES = 128
SB_HEAD_DIM = 64
SB_WIDTH = 384
SB_BLOCK = 128
SB_QTILE = 512
LOG2_E = 1.4426950408889634
ML_HEADS = 4
ML_HEAD_DIM = 96
ML_WIDTH = 384
ML_PAD = ML_HEADS * LANES
ML_CHUNK = 128
CONV_K = 4
SGU_WIDTH = 256
SGU_GROUPS = 4
SGU_GROUP_DIM = 64
SGU_BLOCK = 128
STREAM_CHUNK = 64
N_GROUPS = 4
EXPERTS_PER_GROUP = 8
N_EXPERTS = 32
D_EXPERT = 256
ROUTER_ROWS = 40
CHUNK_ROWS = 16
EXPERT_BLOCK_CHUNKS = 32
EXPERT_BLOCK = CHUNK_ROWS * EXPERT_BLOCK_CHUNKS

COL_A = 0
COL_BQK = 3 * SB_WIDTH
COL_BVO = COL_BQK + 2 * ML_PAD
COL_C = COL_BVO + 2 * ML_PAD
COL_G = COL_C + 2 * SGU_WIDTH
N_Z = COL_G + LANES

VMEM_LIMIT = 56 * 1024 * 1024


def _cparams(sem, vmem=VMEM_LIMIT):
    return pltpu.CompilerParams(dimension_semantics=sem, vmem_limit_bytes=vmem)


def _rms(x, g):
    return x * lax.rsqrt(jnp.mean(x * x, axis=-1, keepdims=True) + EPS) * g


def _sigmoid(x):
    return 0.5 * jnp.tanh(0.5 * x) + 0.5


def _split_bf16(x):
    hi = x.astype(BF16)
    lo = (x - hi.astype(F32)).astype(BF16)
    return hi, lo


def _dot(a, b):
    return jnp.dot(a, b, preferred_element_type=F32)


def _dot_nt(a, b):
    return lax.dot_general(a, b, (((1,), (1,)), ((), ())), preferred_element_type=F32)


def _dot_tn(a, b):
    return lax.dot_general(a, b, (((0,), (0,)), ((), ())), preferred_element_type=F32)


def _dot_split_lhs(x, m):
    hi, lo = _split_bf16(x)
    return _dot(hi, m) + _dot(lo, m)


def _dot_split_rhs(m, x):
    hi, lo = _split_bf16(x)
    return _dot(m, hi) + _dot(m, lo)


def _in_proj_kernel(x_ref, g_ref, w_ref, za_ref, zqk_ref, zvo_ref, zc_ref, zg_ref):
    hn = _rms(x_ref[...], g_ref[...]).astype(BF16)
    za_ref[...] = _dot(hn, w_ref[:, COL_A:COL_BQK]).astype(BF16)
    zqk_ref[...] = _dot(hn, w_ref[:, COL_BQK:COL_BVO])
    zvo_ref[...] = _dot(hn, w_ref[:, COL_BVO:COL_C]).astype(BF16)
    zc_ref[...] = _dot(hn, w_ref[:, COL_C:COL_G]).astype(BF16)
    zg_ref[...] = _dot(hn, w_ref[:, COL_G:N_Z])


def _in_proj(h, g, w, tm):
    t = h.shape[0]
    row = lambda n: pl.BlockSpec((tm, n), lambda i: (i, 0))
    full = lambda a: pl.BlockSpec(a.shape, lambda i: (0,) * a.ndim)
    return pl.pallas_call(
        _in_proj_kernel,
        grid=(t // tm,),
        in_specs=[row(D_MODEL), full(g), full(w)],
        out_specs=[row(3 * SB_WIDTH), row(2 * ML_PAD), row(2 * ML_PAD), row(2 * SGU_WIDTH), row(LANES)],
        out_shape=[jax.ShapeDtypeStruct((t, 3 * SB_WIDTH), BF16),
                   jax.ShapeDtypeStruct((t, 2 * ML_PAD), F32),
                   jax.ShapeDtypeStruct((t, 2 * ML_PAD), BF16),
                   jax.ShapeDtypeStruct((t, 2 * SGU_WIDTH), BF16),
                   jax.ShapeDtypeStruct((t, LANES), F32)],
        compiler_params=_cparams(("parallel",)),
        name="in_proj",
    )(h, g, w)


def _sb_kernel(q_ref, k_ref, v_ref, o_ref, acc_ref, car_ref, lsig_ref, after_ref, tot_ref):
    n = pl.program_id(2)
    blk = SB_BLOCK
    tq = q_ref.shape[0]
    sub = tq // blk
    lane_k = lax.broadcasted_iota(I32, (blk, LANES), 1)
    head0 = lane_k < SB_HEAD_DIM
    mj = lax.broadcasted_iota(I32, (2 * blk, 2 * blk), 0) % blk
    ms = lax.broadcasted_iota(I32, (2 * blk, 2 * blk), 1)
    suffix = jnp.where((mj > ms) | (ms >= blk), 1.0, 0.0).astype(BF16)

    acc_ref[...] = jnp.zeros_like(acc_ref)
    car_ref[...] = jnp.zeros_like(car_ref)

    def per_head(x):
        zero = jnp.zeros_like(x)
        return jnp.concatenate([jnp.where(head0, x, zero), jnp.where(head0, zero, x)], axis=0)

    def stage1(j, r0, masked, buf):
        start = pl.multiple_of(j * blk, blk)
        z = _dot_nt(q_ref[r0:, :], per_head(k_ref[pl.ds(start, blk), :]))
        sign = jnp.uint32(0x80000000)
        neg_abs = lax.bitcast_convert_type(lax.bitcast_convert_type(z, jnp.uint32) | sign, F32)
        e = jnp.exp2(neg_abs)
        log_sig = jnp.minimum(z, 0.0) - jnp.log2(1.0 + e)
        log1m = log_sig - z
        if masked:
            mask = (lax.broadcasted_iota(I32, (blk, 2 * blk), 1) % blk
                    < lax.broadcasted_iota(I32, (blk, 2 * blk), 0))
            top_l = jnp.where(mask, log1m[:blk], 0.0)
            top_s = jnp.where(mask, log_sig[:blk], -jnp.inf)
            if tq - r0 == blk:
                log1m, log_sig = top_l, top_s
            else:
                log1m = jnp.concatenate([top_l, log1m[blk:]], axis=0)
                log_sig = jnp.concatenate([top_s, log_sig[blk:]], axis=0)
        lsig_ref[buf, r0:, :] = log_sig
        hi, lo = _split_bf16(log1m)
        for hh in range(2):
            cs = slice(hh * blk, (hh + 1) * blk)
            ca = _dot(jnp.concatenate([hi[:, cs], lo[:, cs]], axis=1), suffix)
            after_ref[buf, r0:, cs] = ca[:, :blk]
            tot_ref[buf, r0:, cs] = ca[:, blk:]

    def stage2(j, r0, buf):
        start = pl.multiple_of(j * blk, blk)
        arg = lsig_ref[buf, r0:, :] + after_ref[buf, r0:, :] + car_ref[r0:, :]
        acc_ref[r0:, :] += _dot(jnp.exp2(arg).astype(BF16), per_head(v_ref[pl.ds(start, blk), :]))
        car_ref[r0:, :] += tot_ref[buf, r0:, :]

    for kj in reversed(range(sub)):
        stage1(n * sub + kj, kj * blk, True, kj % 2)
        stage2(n * sub + kj, kj * blk, kj % 2)

    top = n * sub - 1

    @pl.when(n > 0)
    def _():
        stage1(top, 0, False, 0)

    unroll = 4

    def body(i, c):
        j0 = top - unroll * i
        for u in range(unroll):
            nxt = j0 - u - 1
            if u == unroll - 1:
                nxt = jnp.maximum(nxt, 0)
            stage1(nxt, 0, False, (u + 1) % 2)
            stage2(j0 - u, 0, u % 2)
        return c

    lax.fori_loop(0, n * (sub // unroll), body, 0)
    o_ref[...] = acc_ref[...].astype(o_ref.dtype)


def _sb_attn(za, batch, seq):
    za3 = za.reshape(batch, seq, 3 * SB_WIDTH)
    pairs = SB_WIDTH // LANES
    tq = min(SB_QTILE, seq)
    return pl.pallas_call(
        _sb_kernel,
        grid=(batch, pairs, seq // tq),
        in_specs=[pl.BlockSpec((None, tq, LANES), lambda b, p, n: (b, n, p)),
                  pl.BlockSpec((None, seq, LANES), lambda b, p, n: (b, 0, pairs + p)),
                  pl.BlockSpec((None, seq, LANES), lambda b, p, n: (b, 0, 2 * pairs + p))],
        out_specs=pl.BlockSpec((None, tq, LANES), lambda b, p, n: (b, n, p)),
        out_shape=jax.ShapeDtypeStruct((batch, seq, SB_WIDTH), BF16),
        scratch_shapes=[pltpu.VMEM((tq, LANES), F32),
                        pltpu.VMEM((tq, 2 * LANES), F32),
                        pltpu.VMEM((2, tq, 2 * LANES), F32),
                        pltpu.VMEM((2, tq, 2 * LANES), F32),
                        pltpu.VMEM((2, tq, 2 * LANES), F32)],
        compiler_params=_cparams(("parallel", "parallel", "arbitrary")),
        name="sb_attn",
    )(za3, za3, za3)


def _mlstm_kernel(qk_ref, vo_ref, g_ref, cw_ref, cb_ref, gb_ref, mg_ref, o_ref,
                  q_s, k_s, vt_s, cum_s, gmb_s, pm_s, gmbt_s, ct_s, m_s):
    seq = qk_ref.shape[0]
    L = ML_CHUNK
    cw = cw_ref[...]
    cb = cb_ref[...]
    rows = lax.broadcasted_iota(I32, (L, 2 * ML_PAD), 0)

    def conv_silu(r0, taps):
        y = taps[0] * cw[CONV_K - 1:CONV_K, :] + cb
        for d in range(1, CONV_K):
            y = y + taps[d] * cw[CONV_K - 1 - d:CONV_K - d, :]
        y = y * _sigmoid(y)
        q_s[pl.ds(r0, L), :] = y[:, :ML_PAD].astype(BF16)
        k_s[pl.ds(r0, L), :] = (y[:, ML_PAD:] * (ML_HEAD_DIM ** -0.5)).astype(BF16)

    ct_s[...] = jnp.zeros_like(ct_s)
    m_s[...] = jnp.zeros_like(m_s)

    ti = lax.broadcasted_iota(I32, (L, L), 0)
    si = lax.broadcasted_iota(I32, (L, L), 1)
    causal = si <= ti
    tril = jnp.where(causal, 1.0, 0.0).astype(BF16)
    lane = lax.broadcasted_iota(I32, (L, LANES), 1)
    row = lax.broadcasted_iota(I32, (L, LANES), 0)
    ones2 = jnp.full((2 * LANES, LANES), 1.0, BF16)
    gb = gb_ref[...]
    mg = mg_ref[...]
    norm_lane = LANES - 1

    def gates(c):
        r0 = pl.multiple_of(c * L, L)
        g = g_ref[pl.ds(r0, L), :] + gb
        logf = jnp.minimum(g, 0.0) - jnp.log1p(jnp.exp(-jnp.abs(g)))
        cum = pltpu.roll(_dot_split_rhs(tril, logf), LANES - ML_HEADS, axis=1)
        gmb = g - cum
        pm = gmb
        step = 1
        while step < L:
            pm = jnp.maximum(pm, jnp.where(row >= step, pltpu.roll(pm, step, axis=0), -jnp.inf))
            step *= 2
        cum_s[pl.ds(r0, L), :] = cum
        gmb_s[pl.ds(r0, L), :] = gmb
        pm_s[pl.ds(r0, L), :] = pm
        gmbt_s[c] = jnp.transpose(gmb)
        for h in range(ML_HEADS):
            v = vo_ref[pl.ds(r0, L), h * LANES:(h + 1) * LANES].astype(F32)
            vt_s[h, c] = jnp.transpose(jnp.where(lane == norm_lane, 1.0, v)).astype(BF16)

    x0 = qk_ref[0:L, :]
    conv_silu(0, [x0] + [jnp.where(rows >= d, pltpu.roll(x0, d, axis=0), 0.0) for d in range(1, CONV_K)])

    def conv_chunk(c, carry):
        r0 = pl.multiple_of(c * L, L)
        xe = qk_ref[pl.ds(r0 - 8, L + 8), :]
        conv_silu(r0, [xe[8 - d:8 - d + L, :] for d in range(CONV_K)])
        return carry

    lax.fori_loop(1, seq // L, conv_chunk, 0)

    def gate_chunk(c, carry):
        gates(2 * c)
        gates(2 * c + 1)
        return carry

    lax.fori_loop(0, seq // (2 * L), gate_chunk, 0)

    heads = range(ML_HEADS)

    def chunk(c, carry):
        r0 = pl.multiple_of(c * L, L)
        cum = cum_s[pl.ds(r0, L), :]
        gmb = gmb_s[pl.ds(r0, L), :]
        pm = pm_s[pl.ds(r0, L), :]
        gmb_t = gmbt_s[c]
        qc = [q_s[pl.ds(r0, L), h * LANES:(h + 1) * LANES] for h in heads]
        kc = [k_s[pl.ds(r0, L), h * LANES:(h + 1) * LANES] for h in heads]
        vt = [vt_s[h, c] for h in heads]
        m_in = [m_s[h][0:1, 0:1] for h in heads]
        ct = [ct_s[h] for h in heads]
        s_raw = [_dot_nt(qc[h], kc[h]) for h in heads]
        mm = [jnp.maximum(m_in[h], pm[:, h:h + 1]) for h in heads]
        inter = [jnp.exp(m_in[h] - mm[h]) for h in heads]
        num = []
        for h in heads:
            wd = jnp.where(causal, jnp.exp(gmb_t[h:h + 1, :] - mm[h]), 0.0)
            lhs = jnp.concatenate([(s_raw[h] * wd).astype(BF16),
                                   (qc[h].astype(F32) * inter[h]).astype(BF16)], axis=1)
            rhs_t = jnp.concatenate([vt[h], ct[h].astype(BF16)], axis=1)
            num.append(_dot_nt(lhs, rhs_t))
        hh = []
        ms = []
        for h in heads:
            den = jnp.maximum(jnp.abs(num[h][:, norm_lane:]), jnp.exp(-(cum[:, h:h + 1] + mm[h])))
            x = jnp.where(lane == norm_lane, 0.0, num[h] * (1.0 / den))
            hi, lo = _split_bf16(x * x)
            hh.append(x)
            ms.append(_dot(jnp.concatenate([hi, lo], axis=1), ones2) * (1.0 / ML_HEAD_DIM))
        for h in heads:
            cs = slice(h * LANES, (h + 1) * LANES)
            oc = vo_ref[pl.ds(r0, L), ML_PAD + h * LANES:ML_PAD + (h + 1) * LANES].astype(F32)
            o_ref[pl.ds(r0, L), cs] = (hh[h] * lax.rsqrt(ms[h] + EPS) * mg[:, cs]
                                       * _sigmoid(oc)).astype(o_ref.dtype)
        for h in heads:
            mm_last = mm[h][L - 1:L, :]
            kw = (kc[h].astype(F32) * jnp.exp(gmb[:, h:h + 1] - mm_last)).astype(BF16)
            ct_s[h] = inter[h][L - 1:L, :] * ct[h] + _dot(vt[h], kw)
            m_s[h] = jnp.broadcast_to(cum[L - 1:L, h:h + 1] + mm_last, (8, LANES))
        return carry

    lax.fori_loop(0, seq // L, chunk, 0)


def _mlstm(zqk, zvo, zg, cw, cb, gb, mg, batch, seq):
    full = lambda a: pl.BlockSpec(a.shape, lambda b: (0,) * a.ndim)
    seq_blk = lambda n: pl.BlockSpec((None, seq, n), lambda b: (b, 0, 0))
    return pl.pallas_call(
        _mlstm_kernel,
        grid=(batch,),
        in_specs=[seq_blk(2 * ML_PAD), seq_blk(2 * ML_PAD), seq_blk(LANES),
                  full(cw), full(cb), full(gb), full(mg)],
        out_specs=seq_blk(ML_PAD),
        out_shape=jax.ShapeDtypeStruct((batch, seq, ML_PAD), BF16),
        scratch_shapes=[pltpu.VMEM((seq, ML_PAD), BF16), pltpu.VMEM((seq, ML_PAD), BF16),
                        pltpu.VMEM((ML_HEADS, seq // ML_CHUNK, LANES, ML_CHUNK), BF16),
                        pltpu.VMEM((seq, LANES), F32), pltpu.VMEM((seq, LANES), F32),
                        pltpu.VMEM((seq, LANES), F32),
                        pltpu.VMEM((seq // ML_CHUNK, LANES, ML_CHUNK), F32),
                        pltpu.VMEM((ML_HEADS, LANES, LANES), F32),
                        pltpu.VMEM((ML_HEADS, 8, LANES), F32)],
        compiler_params=_cparams(("parallel",)),
        name="mlstm",
    )(zqk.reshape(batch, seq, 2 * ML_PAD), zvo.reshape(batch, seq, 2 * ML_PAD),
      zg.reshape(batch, seq, LANES), cw, cb, gb, mg)


def _sgu_kernel(z_ref, lg_ref, lb_ref, w_ref, b_ref, og_ref, o_ref):
    rows = z_ref.shape[0]
    W = SGU_WIDTH
    gi = lax.broadcasted_iota(I32, (W, W), 0) // SGU_GROUP_DIM
    gj = lax.broadcasted_iota(I32, (W, W), 1) // SGU_GROUP_DIM
    avg = jnp.where(gi == gj, 1.0 / SGU_GROUP_DIM, 0.0).astype(BF16)
    ti = lax.broadcasted_iota(I32, (SGU_BLOCK, SGU_BLOCK), 0) // STREAM_CHUNK
    si = lax.broadcasted_iota(I32, (SGU_BLOCK, SGU_BLOCK), 1) // STREAM_CHUNK
    chunk_causal = si <= ti
    lane_group = lax.broadcasted_iota(I32, (SGU_BLOCK, W), 1) // SGU_GROUP_DIM
    for r in range(rows // SGU_BLOCK):
        rs = slice(r * SGU_BLOCK, (r + 1) * SGU_BLOCK)
        u = jax.nn.gelu(z_ref[rs, :W].astype(F32))
        v = jax.nn.gelu(z_ref[rs, W:].astype(F32))
        mu = _dot_split_lhs(v, avg)
        vc = v - mu
        var = _dot_split_lhs(vc * vc, avg)
        vn = (vc * lax.rsqrt(var + EPS) * lg_ref[...] + lb_ref[...]).astype(BF16)
        mixed = b_ref[...]
        for g in range(SGU_GROUPS):
            wg = jnp.where(chunk_causal, w_ref[g], 0.0).astype(BF16)
            mixed = mixed + _dot(wg, jnp.where(lane_group == g, vn, jnp.zeros_like(vn)))
        o_ref[rs, :] = _rms(u * mixed, og_ref[...]).astype(o_ref.dtype)


def _sgu(zc, lg, lb, w, bias, og, batch, seq, rows):
    full = lambda a: pl.BlockSpec(a.shape, lambda b, r: (0,) * a.ndim)
    return pl.pallas_call(
        _sgu_kernel,
        grid=(batch, seq // rows),
        in_specs=[pl.BlockSpec((None, rows, 2 * SGU_WIDTH), lambda b, r: (b, r, 0)),
                  full(lg), full(lb), full(w), full(bias), full(og)],
        out_specs=pl.BlockSpec((None, rows, SGU_WIDTH), lambda b, r: (b, r, 0)),
        out_shape=jax.ShapeDtypeStruct((batch, seq, SGU_WIDTH), BF16),
        compiler_params=_cparams(("parallel", "parallel")),
        name="sgu",
    )(zc.reshape(batch, seq, 2 * SGU_WIDTH), lg, lb, w, bias, og)


def _dispatch_rows(tm):
    return 2 * tm + N_EXPERTS * (CHUNK_ROWS - 1) + (N_EXPERTS * (CHUNK_ROWS - 1)) % CHUNK_ROWS


def _mix_out_kernel(h_ref, ya_ref, yb_ref, yc_ref, ag_ref, wa_ref, wb_ref, wc_ref, n2_ref,
                    rwh_ref, rwl_ref, rb_ref, h1_ref, xs_ref, meta_ref, cnt_ref):
    tm = h_ref.shape[0]
    rr = xs_ref.shape[0]
    ya = _rms(ya_ref[...].astype(F32), ag_ref[...]).astype(BF16)
    h1 = h_ref[...] + _dot(ya, wa_ref[...]) + _dot(yb_ref[...], wb_ref[...]) + _dot(yc_ref[...], wc_ref[...])
    h1_ref[...] = h1
    xn = _rms(h1, n2_ref[...])
    x_hi, x_lo = _split_bf16(xn)
    logits = (_dot_nt(rwh_ref[...], x_hi) + _dot_nt(rwl_ref[...], x_hi) + _dot_nt(rwh_ref[...], x_lo)
              + rb_ref[...])
    e_log = logits[:N_EXPERTS]
    g_log = logits[N_EXPERTS:N_EXPERTS + N_GROUPS]
    g_iota = lax.broadcasted_iota(I32, (N_GROUPS, tm), 0)
    g_max = jnp.max(g_log, axis=0, keepdims=True)
    g_top = jnp.min(jnp.where(g_log == g_max, g_iota, N_GROUPS), axis=0, keepdims=True)
    g_w = 1.0 / jnp.sum(jnp.exp(g_log - g_max), axis=0, keepdims=True)
    e_iota = lax.broadcasted_iota(I32, (N_EXPERTS, tm), 0)
    val = jnp.where(e_iota // EXPERTS_PER_GROUP == g_top, e_log, -jnp.inf)
    m1 = jnp.max(val, axis=0, keepdims=True)
    i1 = jnp.min(jnp.where(val == m1, e_iota, N_EXPERTS), axis=0, keepdims=True)
    val2 = jnp.where(e_iota == i1, -jnp.inf, val)
    m2 = jnp.max(val2, axis=0, keepdims=True)
    i2 = jnp.min(jnp.where(val2 == m2, e_iota, N_EXPERTS), axis=0, keepdims=True)
    e2 = jnp.exp(m2 - m1)
    w0 = g_w / (1.0 + e2)
    w1 = g_w * e2 / (1.0 + e2)
    oh0 = e_iota == i1
    oh1 = e_iota == i2
    oh = jnp.concatenate([jnp.where(oh0, 1.0, 0.0), jnp.where(oh1, 1.0, 0.0)], axis=0).astype(BF16)
    ta = lax.broadcasted_iota(I32, (tm, tm), 0)
    tb = lax.broadcasted_iota(I32, (tm, tm), 1)
    before = jnp.where(ta < tb, 1.0, 0.0).astype(BF16)
    rank = _dot(oh, before)
    c0 = jnp.sum(jnp.where(oh0, 1.0, 0.0), axis=1, keepdims=True)
    c1 = jnp.sum(jnp.where(oh1, 1.0, 0.0), axis=1, keepdims=True)
    nchunk = jnp.floor((c0 + c1 + (CHUNK_ROWS - 1)) * (1.0 / CHUNK_ROWS))
    nchunk_b = jnp.broadcast_to(nchunk, (N_EXPERTS, LANES))
    ea = lax.broadcasted_iota(I32, (N_EXPERTS, N_EXPERTS), 0)
    eb = lax.broadcasted_iota(I32, (N_EXPERTS, N_EXPERTS), 1)
    lower = jnp.where(eb < ea, 1.0, 0.0).astype(BF16)
    loc = CHUNK_ROWS * _dot(lower, nchunk_b.astype(BF16))[:, 0:1]
    dest0 = jnp.sum(jnp.where(oh0, loc + rank[:N_EXPERTS], 0.0), axis=0, keepdims=True)
    dest1 = jnp.sum(jnp.where(oh1, loc + c0 + rank[N_EXPERTS:], 0.0), axis=0, keepdims=True)
    r_iota = lax.broadcasted_iota(I32, (rr, tm), 0)
    d0 = dest0.astype(I32)
    d1 = dest1.astype(I32)
    perm = jnp.where(r_iota == d0, 1.0, jnp.where(r_iota == d1, 1.0, 0.0)).astype(BF16)
    xs_ref[...] = _dot(perm, x_hi).astype(BF16)
    cnt_ref[...] = nchunk_b
    m_iota = lax.broadcasted_iota(I32, (8, tm), 0)
    meta = jnp.where(m_iota == 0, dest0, jnp.where(m_iota == 1, dest1, jnp.where(m_iota == 2, w0,
                     jnp.where(m_iota == 3, w1, 0.0))))
    meta = jnp.concatenate([meta, jnp.zeros((LANES - 8, tm), F32)], axis=0)
    meta_ref[...] = jnp.transpose(meta)


def _mix_out(h, ya, yb, yc, ag, wa, wb, wc, n2, rwh, rwl, rb, tm):
    t = h.shape[0]
    nt = t // tm
    rr = _dispatch_rows(tm)
    row = lambda n: pl.BlockSpec((tm, n), lambda i: (i, 0))
    full = lambda a: pl.BlockSpec(a.shape, lambda i: (0,) * a.ndim)
    return pl.pallas_call(
        _mix_out_kernel,
        grid=(nt,),
        in_specs=[row(D_MODEL), row(SB_WIDTH), row(ML_PAD), row(SGU_WIDTH), full(ag), full(wa), full(wb),
                  full(wc), full(n2), full(rwh), full(rwl), full(rb)],
        out_specs=[row(D_MODEL), pl.BlockSpec((rr, D_MODEL), lambda i: (i, 0)), row(LANES),
                   pl.BlockSpec((None, N_EXPERTS, LANES), lambda i: (i, 0, 0))],
        out_shape=[jax.ShapeDtypeStruct((t, D_MODEL), F32),
                   jax.ShapeDtypeStruct((nt * rr, D_MODEL), BF16),
                   jax.ShapeDtypeStruct((t, LANES), F32),
                   jax.ShapeDtypeStruct((nt, N_EXPERTS, LANES), F32)],
        compiler_params=_cparams(("parallel",)),
        name="mix_out",
    )(h, ya, yb, yc, ag, wa, wb, wc, n2, rwh, rwl, rb)


def _chunk_tables(cnt, rr, n_blocks_max):
    nt = cnt.shape[0]
    ne = N_EXPERTS + 1
    tail = rr // CHUNK_ROWS - jnp.sum(cnt, axis=1)
    cnt = jnp.concatenate([cnt, tail[:, None]], axis=1)
    loc = (jnp.cumsum(cnt, axis=1) - cnt) * CHUNK_ROWS
    cnt_e = cnt.T
    cum_e = jnp.cumsum(cnt_e, axis=1)
    total = cum_e[:, -1]
    nblk = (total + EXPERT_BLOCK_CHUNKS - 1) // EXPERT_BLOCK_CHUNKS
    bend = jnp.cumsum(nblk)
    bstart = bend - nblk
    n_blocks = bend[-1]
    i = jnp.arange(n_blocks_max, dtype=I32)
    be = jnp.minimum(jnp.sum(bend[None, :] <= i[:, None], axis=1), ne - 1).astype(I32)
    oh_b = be[:, None] == jnp.arange(ne, dtype=I32)[None, :]
    pick = lambda v: jnp.sum(jnp.where(oh_b, v[None, :], 0), axis=1)
    pick2 = lambda m: jnp.sum(jnp.where(oh_b[:, :, None], m[None, :, :], 0), axis=1)
    q = (i - pick(bstart))[:, None] * EXPERT_BLOCK_CHUNKS + jnp.arange(EXPERT_BLOCK_CHUNKS, dtype=I32)[None, :]
    valid = (q < pick(total)[:, None]) & (i < n_blocks)[:, None]
    cum_b = pick2(cum_e)
    base = jnp.arange(nt, dtype=I32)[None, :] * rr + loc.T - (cum_e - cnt_e) * CHUNK_ROWS
    base_b = pick2(base)
    j = jnp.minimum(jnp.sum(cum_b[:, None, :] <= q[:, :, None], axis=-1), nt - 1)
    oh_j = j[:, :, None] == jnp.arange(nt, dtype=I32)[None, None, :]
    row = jnp.sum(jnp.where(oh_j, base_b[:, None, :], 0), axis=-1) + q * CHUNK_ROWS
    spare = nt * rr + ((i % 2)[:, None] * EXPERT_BLOCK_CHUNKS
                       + jnp.arange(EXPERT_BLOCK_CHUNKS, dtype=I32)[None, :]) * CHUNK_ROWS
    src = jnp.where(valid, row, 0).astype(I32)
    dst = jnp.where(valid, row, spare).astype(I32)
    return be, src.reshape(-1), dst.reshape(-1), n_blocks.reshape(1).astype(I32)


def _experts_kernel(be_ref, src_ref, dst_ref, nb_ref, xs_ref, wg_ref, wu_ref, wd_ref, ys_ref,
                    xbuf, ybuf, wgb, wub, wdb, sem_in, sem_out):
    i = pl.program_id(0)
    nb = nb_ref[0]
    slot = lax.rem(i, 2)

    def is_compute(b):
        return be_ref[b] < N_EXPERTS

    def start_loads(b, sl):
        for s in range(EXPERT_BLOCK_CHUNKS):
            src = pl.multiple_of(src_ref[b * EXPERT_BLOCK_CHUNKS + s], CHUNK_ROWS)
            pltpu.make_async_copy(xs_ref.at[pl.ds(src, CHUNK_ROWS), :],
                                  xbuf.at[sl, s * CHUNK_ROWS:(s + 1) * CHUNK_ROWS, :], sem_in.at[sl]).start()

    def wait_loads(sl):
        pltpu.make_async_copy(xs_ref.at[0:EXPERT_BLOCK, :], xbuf.at[sl], sem_in.at[sl]).wait()

    def start_stores(b, sl):
        for s in range(EXPERT_BLOCK_CHUNKS):
            dst = pl.multiple_of(dst_ref[b * EXPERT_BLOCK_CHUNKS + s], CHUNK_ROWS)
            pltpu.make_async_copy(ybuf.at[sl, s * CHUNK_ROWS:(s + 1) * CHUNK_ROWS, :],
                                  ys_ref.at[pl.ds(dst, CHUNK_ROWS), :], sem_out.at[sl]).start()

    def wait_stores(sl):
        pltpu.make_async_copy(ybuf.at[sl], ys_ref.at[0:EXPERT_BLOCK, :], sem_out.at[sl]).wait()

    @pl.when(i == 0)
    def _():
        ybuf[...] = jnp.zeros(ybuf.shape, BF16)
        real_rows = ys_ref.shape[0] - 2 * EXPERT_BLOCK
        for par in range(2):
            fill = pltpu.make_async_copy(
                ybuf.at[par], ys_ref.at[real_rows + par * EXPERT_BLOCK:real_rows + (par + 1) * EXPERT_BLOCK, :],
                sem_out.at[par])
            fill.start()
            fill.wait()

        start_loads(0, 0)

    @pl.when(i < nb)
    def _():
        wait_loads(slot)

        @pl.when(i >= 2)
        def _():
            wait_stores(slot)

        nxt = jnp.minimum(i + 1, nb - 1)

        @pl.when(is_compute(i))
        def _():
            @pl.when((i == 0) | (be_ref[i] != be_ref[jnp.maximum(i - 1, 0)]))
            def _():
                wgb[...] = wg_ref[...].astype(BF16)
                wub[...] = wu_ref[...].astype(BF16)
                wdb[...] = wd_ref[...].astype(BF16)

            start_loads(nxt, 1 - slot)
            x = xbuf[slot]
            gate = _dot(x, wgb[...])
            up = _dot(x, wub[...])
            mid = (gate * _sigmoid(gate) * up).astype(BF16)
            ybuf[slot] = _dot(mid, wdb[...]).astype(BF16)
            start_stores(i, slot)

        @pl.when(jnp.logical_not(is_compute(i)))
        def _():
            start_loads(nxt, 1 - slot)
            ybuf[slot] = jnp.zeros((EXPERT_BLOCK, D_MODEL), BF16)
            start_stores(i, slot)

        @pl.when(i == nb - 1)
        def _():
            wait_loads(1 - slot)

            @pl.when(i >= 1)
            def _():
                wait_stores(1 - slot)

            wait_stores(slot)


def _experts(be, src, dst, nb, xs, wg, wu, wd, layer):
    n_blocks_max = be.shape[0]
    wspec = lambda a: pl.BlockSpec(
        (None,) + a.shape[1:],
        lambda i, be, src, dst, nb: (layer * N_EXPERTS + jnp.minimum(be[i], N_EXPERTS - 1), 0, 0))
    grid_spec = pltpu.PrefetchScalarGridSpec(
        num_scalar_prefetch=4,
        grid=(n_blocks_max,),
        in_specs=[pl.BlockSpec(memory_space=pl.ANY), wspec(wg), wspec(wu), wspec(wd)],
        out_specs=pl.BlockSpec(memory_space=pl.ANY),
        scratch_shapes=[pltpu.VMEM((2, EXPERT_BLOCK, D_MODEL), BF16), pltpu.VMEM((2, EXPERT_BLOCK, D_MODEL), BF16),
                        pltpu.VMEM(wg.shape[1:], BF16), pltpu.VMEM(wu.shape[1:], BF16),
                        pltpu.VMEM(wd.shape[1:], BF16),
                        pltpu.SemaphoreType.DMA((2,)), pltpu.SemaphoreType.DMA((2,))],
    )
    return pl.pallas_call(
        _experts_kernel,
        grid_spec=grid_spec,
        out_shape=jax.ShapeDtypeStruct((xs.shape[0] + 2 * EXPERT_BLOCK, D_MODEL), BF16),
        compiler_params=_cparams(("arbitrary",)),
        name="experts",
    )(be, src, dst, nb, xs, wg, wu, wd)


def _combine_kernel(h_ref, ys_ref, meta_ref, p_ref, pg_ref, gw_ref, pw_ref, fg_ref, o_ref, *, final):
    tm = h_ref.shape[0]
    rr = ys_ref.shape[0]
    meta = meta_ref[...]
    d0 = meta[:, 0:1].astype(I32)
    d1 = meta[:, 1:2].astype(I32)
    r_iota = lax.broadcasted_iota(I32, (tm, rr), 1)
    unperm = (jnp.where(r_iota == d0, meta[:, 2:3], 0.0) + jnp.where(r_iota == d1, meta[:, 3:4], 0.0)).astype(BF16)
    h2 = h_ref[...] + _dot(unperm, ys_ref[...])
    gate = _sigmoid(_dot(_rms(h2, pg_ref[...]).astype(BF16), gw_ref[...]))
    h3 = h2 + gate * _dot(p_ref[...].astype(BF16), pw_ref[...])
    if final:
        h3 = _rms(h3, fg_ref[...])
    o_ref[...] = h3


def _combine(h1, ys, meta, p, layer, pg, gw, pw, fg, tm, final):
    t = h1.shape[0]
    nt = t // tm
    rr = _dispatch_rows(tm)
    row = lambda n: pl.BlockSpec((tm, n), lambda i: (i, 0))
    full = lambda a: pl.BlockSpec(a.shape, lambda i: (0,) * a.ndim)
    return pl.pallas_call(
        functools.partial(_combine_kernel, final=final),
        grid=(nt,),
        in_specs=[row(D_MODEL), pl.BlockSpec((rr, D_MODEL), lambda i: (i, 0)), row(LANES),
                  pl.BlockSpec((tm, P_DIM), lambda i: (layer * nt + i, 0)),
                  full(pg), full(gw), full(pw), full(fg)],
        out_specs=row(D_MODEL),
        out_shape=jax.ShapeDtypeStruct((t, D_MODEL), F32),
        compiler_params=_cparams(("parallel",)),
        name="combine",
    )(h1, ys, meta, p, pg, gw, pw, fg)


def _pad_heads(a, axis):
    shape = a.shape
    a = a.reshape(shape[:axis] + (ML_HEADS, ML_HEAD_DIM) + shape[axis + 1:])
    pad = [(0, 0)] * a.ndim
    pad[axis + 1] = (0, LANES - ML_HEAD_DIM)
    a = jnp.pad(a, pad)
    return a.reshape(shape[:axis] + (ML_PAD,) + shape[axis + 1:])


def _layer_params(i, w_in, conv_w, conv_b, igate_b, fgate_b, mnorm_g, sgu_b, w_out, router_gw, router_gb,
                  router_ew, router_eb):
    w = w_in[i]
    s, m = SB_WIDTH, ML_WIDTH
    a_q, a_k, a_v = w[:, 0:s] * (SB_HEAD_DIM ** -0.5 * LOG2_E), w[:, s:2 * s], w[:, 2 * s:3 * s]
    o = 3 * s
    b_q, b_k, b_v, b_o = (w[:, o + k * m:o + (k + 1) * m] for k in range(4))
    o = o + 4 * m
    gates = w[:, o:o + 2 * ML_HEADS]
    c_uv = w[:, o + 2 * ML_HEADS:]
    w_r = jnp.concatenate([a_q, a_k, a_v, _pad_heads(b_q, 1), _pad_heads(b_k, 1), _pad_heads(b_v, 1),
                           _pad_heads(b_o, 1), c_uv,
                           jnp.pad(gates, ((0, 0), (0, LANES - 2 * ML_HEADS)))], axis=1).astype(BF16)
    cw = jnp.concatenate([_pad_heads(conv_w[i][:, :m], 1), _pad_heads(conv_w[i][:, m:], 1)], axis=1)
    cb = jnp.concatenate([_pad_heads(conv_b[i][:m], 0), _pad_heads(conv_b[i][m:], 0)])[None, :]
    gb = jnp.pad(jnp.concatenate([igate_b[i], fgate_b[i]]), (0, LANES - 2 * ML_HEADS))[None, :]
    mg = _pad_heads(mnorm_g[i], 0)[None, :]
    sgu_bias = jnp.repeat(sgu_b[i].T, SGU_GROUP_DIM, axis=1)
    wo = w_out[i]
    wa = wo[:s].astype(BF16)
    wb = _pad_heads(wo[s:s + m], 0).astype(BF16)
    wc = wo[s + m:].astype(BF16)
    rw = jnp.concatenate([router_ew[i].T, router_gw[i].T,
                          jnp.zeros((ROUTER_ROWS - N_EXPERTS - N_GROUPS, D_MODEL), F32)], axis=0)
    rwh = rw.astype(BF16)
    rwl = (rw - rwh.astype(F32)).astype(BF16)
    rb = jnp.concatenate([router_eb[i], router_gb[i],
                          jnp.zeros((ROUTER_ROWS - N_EXPERTS - N_GROUPS,), F32)])[:, None]
    return w_r, cw, cb, gb, mg, sgu_bias, wa, wb, wc, rwh, rwl, rb


def kernel(x, p, norm1_g, w_in, conv_w, conv_b, igate_b, fgate_b, mnorm_g, sb_out_g, sgu_ln_g, sgu_ln_b, sgu_w,
           sgu_b, sgu_out_g, w_out, norm2_g, router_gw, router_gb, router_ew, router_eb, w_gate, w_up, w_down,
           ple_norm_g, ple_gate_w, ple_proj_w, final_g, *, tile=512):
    batch, seq, d = x.shape
    depth = w_in.shape[0]
    t = batch * seq
    tm = min(tile, t)
    nt = t // tm
    rr = _dispatch_rows(tm)
    n_blocks_max = (nt * rr // CHUNK_ROWS) // EXPERT_BLOCK_CHUNKS + N_EXPERTS + 1
    h = x.astype(F32).reshape(t, d)
    p2 = p.reshape(depth * t, P_DIM)
    wg_all = w_gate.reshape((depth * N_EXPERTS,) + w_gate.shape[2:])
    wu_all = w_up.reshape((depth * N_EXPERTS,) + w_up.shape[2:])
    wd_all = w_down.reshape((depth * N_EXPERTS,) + w_down.shape[2:])
    for i in range(depth):
        (w_r, cw, cb, gb, mg, sgu_bias, wa, wb, wc, rwh, rwl, rb) = _layer_params(
            i, w_in, conv_w, conv_b, igate_b, fgate_b, mnorm_g, sgu_b, w_out, router_gw, router_gb,
            router_ew, router_eb)
        za, zqk, zvo, zc, zg = _in_proj(h, norm1_g[i][None, :], w_r, tm)
        ya = _sb_attn(za, batch, seq).reshape(t, SB_WIDTH)
        yb = _mlstm(zqk, zvo, zg, cw, cb, gb, mg, batch, seq).reshape(t, ML_PAD)
        yc = _sgu(zc, sgu_ln_g[i][None, :], sgu_ln_b[i][None, :], sgu_w[i], sgu_bias, sgu_out_g[i][None, :],
                  batch, seq, min(seq, 512)).reshape(t, SGU_WIDTH)
        h1, xs, meta, cnt = _mix_out(h, ya, yb, yc, sb_out_g[i][None, :], wa, wb, wc, norm2_g[i][None, :],
                                     rwh, rwl, rb, tm)
        be, src, dst, nb = _chunk_tables(cnt[:, :, 0].astype(I32), rr, n_blocks_max)
        ys = _experts(be, src, dst, nb, xs, wg_all, wu_all, wd_all, i)
        h = _combine(h1, ys, meta, p2, i, ple_norm_g[i][None, :], ple_gate_w[i].astype(BF16),
                     ple_proj_w[i].astype(BF16), final_g[None, :], tm, i == depth - 1)
    return h.reshape(batch, seq, d).astype(x.dtype)
```

```python
import functools

import jax
import jax.numpy as jnp
import numpy as np
from jax import lax
from jax.experimental import pallas as pl
from jax.experimental.pallas import tpu as pltpu

F32 = jnp.float32
BF16 = jnp.bfloat16
I32 = jnp.int32

D_MODEL = 1024
P_DIM = 256
EPS = 1e-6
LANES = 128
SB_HEAD_DIM = 64
SB_WIDTH = 384
SB_BLOCK = 128
SB_QTILE = 512
LOG2_E = 1.4426950408889634
ML_HEADS = 4
ML_HEAD_DIM = 96
ML_WIDTH = 384
ML_PAD = ML_HEADS * LANES
ML_CHUNK = 128
CONV_K = 4
SGU_WIDTH = 256
SGU_GROUPS = 4
SGU_GROUP_DIM = 64
SGU_BLOCK = 128
STREAM_CHUNK = 64
N_GROUPS = 4
EXPERTS_PER_GROUP = 8
N_EXPERTS = 32
D_EXPERT = 256
ROUTER_ROWS = 40
CHUNK_ROWS = 16
EXPERT_BLOCK_CHUNKS = 32
EXPERT_BLOCK = CHUNK_ROWS * EXPERT_BLOCK_CHUNKS

COL_A = 0
COL_BQK = 3 * SB_WIDTH
COL_BVO = COL_BQK + 2 * ML_PAD
COL_C = COL_BVO + 2 * ML_PAD
COL_G = COL_C + 2 * SGU_WIDTH
N_Z = COL_G + LANES

VMEM_LIMIT = 56 * 1024 * 1024


def _cparams(sem, vmem=VMEM_LIMIT):
    return pltpu.CompilerParams(dimension_semantics=sem, vmem_limit_bytes=vmem)


def _rms(x, g):
    return x * lax.rsqrt(jnp.mean(x * x, axis=-1, keepdims=True) + EPS) * g


def _sigmoid(x):
    return 0.5 * jnp.tanh(0.5 * x) + 0.5


def _split_bf16(x):
    hi = x.astype(BF16)
    lo = (x - hi.astype(F32)).astype(BF16)
    return hi, lo


def _dot(a, b):
    return jnp.dot(a, b, preferred_element_type=F32)


def _dot_nt(a, b):
    return lax.dot_general(a, b, (((1,), (1,)), ((), ())), preferred_element_type=F32)


def _dot_tn(a, b):
    return lax.dot_general(a, b, (((0,), (0,)), ((), ())), preferred_element_type=F32)


def _dot_split_lhs(x, m):
    hi, lo = _split_bf16(x)
    return _dot(hi, m) + _dot(lo, m)


def _dot_split_rhs(m, x):
    hi, lo = _split_bf16(x)
    return _dot(m, hi) + _dot(m, lo)


def _spatial_gating(zc, lg, lb, w_ref, bias, og):
    W = SGU_WIDTH
    gi = lax.broadcasted_iota(I32, (W, W), 0) // SGU_GROUP_DIM
    gj = lax.broadcasted_iota(I32, (W, W), 1) // SGU_GROUP_DIM
    avg = jnp.where(gi == gj, 1.0 / SGU_GROUP_DIM, 0.0).astype(BF16)
    ti = lax.broadcasted_iota(I32, (SGU_BLOCK, SGU_BLOCK), 0) // STREAM_CHUNK
    si = lax.broadcasted_iota(I32, (SGU_BLOCK, SGU_BLOCK), 1) // STREAM_CHUNK
    chunk_causal = si <= ti
    lane_group = lax.broadcasted_iota(I32, (SGU_BLOCK, W), 1) // SGU_GROUP_DIM
    wg = [jnp.where(chunk_causal, w_ref[g], 0.0).astype(BF16) for g in range(SGU_GROUPS)]
    wpair = [jnp.concatenate(wg[g:g + 2], axis=1) for g in range(0, SGU_GROUPS, 2)]
    out = []
    for r in range(zc.shape[0] // SGU_BLOCK):
        rs = slice(r * SGU_BLOCK, (r + 1) * SGU_BLOCK)
        u = jax.nn.gelu(zc[rs, :W])
        v = jax.nn.gelu(zc[rs, W:])
        mu = _dot_split_lhs(v, avg)
        vc = v - mu
        var = _dot_split_lhs(vc * vc, avg)
        vn = (vc * lax.rsqrt(var + EPS) * lg + lb).astype(BF16)
        mixed = bias
        zero = jnp.zeros_like(vn)
        for p, g in enumerate(range(0, SGU_GROUPS, 2)):
            vpair = jnp.concatenate([jnp.where(lane_group == g, vn, zero),
                                     jnp.where(lane_group == g + 1, vn, zero)], axis=0)
            mixed = mixed + _dot(wpair[p], vpair)
        out.append(_rms(u * mixed, og))
    return jnp.concatenate(out, axis=0)


def _in_proj_kernel(x_ref, g_ref, w_ref, lg_ref, lb_ref, sw_ref, sb_ref, og_ref,
                    za_ref, zqk_ref, zvo_ref, yc_ref, zg_ref):
    hn = _rms(x_ref[...], g_ref[...]).astype(BF16)
    za_ref[...] = _dot(hn, w_ref[:, COL_A:COL_BQK]).astype(BF16)
    zqk_ref[...] = _dot(hn, w_ref[:, COL_BQK:COL_BVO])
    zvo_ref[...] = _dot(hn, w_ref[:, COL_BVO:COL_C]).astype(BF16)
    zg_ref[...] = _dot(hn, w_ref[:, COL_G:N_Z])
    zc = _dot(hn, w_ref[:, COL_C:COL_G])
    yc_ref[...] = _spatial_gating(zc, lg_ref[...], lb_ref[...], sw_ref, sb_ref[...], og_ref[...]).astype(BF16)


def _in_proj(h, g, w, lg, lb, sw, sbias, og, tm):
    t = h.shape[0]
    row = lambda n: pl.BlockSpec((tm, n), lambda i: (i, 0))
    full = lambda a: pl.BlockSpec(a.shape, lambda i: (0,) * a.ndim)
    return pl.pallas_call(
        _in_proj_kernel,
        grid=(t // tm,),
        in_specs=[row(D_MODEL), full(g), full(w), full(lg), full(lb), full(sw), full(sbias), full(og)],
        out_specs=[row(3 * SB_WIDTH), row(2 * ML_PAD), row(2 * ML_PAD), row(SGU_WIDTH), row(LANES)],
        out_shape=[jax.ShapeDtypeStruct((t, 3 * SB_WIDTH), BF16),
                   jax.ShapeDtypeStruct((t, 2 * ML_PAD), F32),
                   jax.ShapeDtypeStruct((t, 2 * ML_PAD), BF16),
                   jax.ShapeDtypeStruct((t, SGU_WIDTH), BF16),
                   jax.ShapeDtypeStruct((t, LANES), F32)],
        compiler_params=_cparams(("parallel",)),
        name="in_proj",
    )(h, g, w, lg, lb, sw, sbias, og)


def _sb_kernel(q_ref, k_ref, v_ref, o_ref, acc_ref, car_ref, pre_ref, tot_ref):
    n = pl.program_id(2)
    blk = SB_BLOCK
    tq = q_ref.shape[0]
    sub = tq // blk
    lane_k = lax.broadcasted_iota(I32, (blk, LANES), 1)
    head0 = lane_k < SB_HEAD_DIM
    mj = lax.broadcasted_iota(I32, (2 * blk, 2 * blk), 0) % blk
    ms = lax.broadcasted_iota(I32, (2 * blk, 2 * blk), 1)
    suffix = jnp.where((mj > ms) | (ms >= blk), 1.0, 0.0).astype(BF16)

    def per_head(x):
        zero = jnp.zeros_like(x)
        return jnp.concatenate([jnp.where(head0, x, zero), jnp.where(head0, zero, x)], axis=0)

    def block_at(ref, j):
        return per_head(ref[pl.ds(pl.multiple_of(j * blk, blk), blk), :])

    def stage1(j, r0, masked, buf):
        z = _dot_nt(q_ref[r0:, :], block_at(k_ref, j))
        sign = jnp.uint32(0x80000000)
        neg_abs = lax.bitcast_convert_type(lax.bitcast_convert_type(z, jnp.uint32) | sign, F32)
        e = jnp.exp2(neg_abs)
        log_sig = jnp.minimum(z, 0.0) - jnp.log2(1.0 + e)
        log1m = log_sig - z
        if masked:
            mask = (lax.broadcasted_iota(I32, (blk, 2 * blk), 1) % blk
                    < lax.broadcasted_iota(I32, (blk, 2 * blk), 0))
            top_l = jnp.where(mask, log1m[:blk], 0.0)
            top_s = jnp.where(mask, log_sig[:blk], -jnp.inf)
            if tq - r0 == blk:
                log1m, log_sig = top_l, top_s
            else:
                log1m = jnp.concatenate([top_l, log1m[blk:]], axis=0)
                log_sig = jnp.concatenate([top_s, log_sig[blk:]], axis=0)
        hi, lo = _split_bf16(log1m)
        for hh in range(2):
            cs = slice(hh * blk, (hh + 1) * blk)
            ca = _dot(jnp.concatenate([hi[:, cs], lo[:, cs]], axis=1), suffix)
            pre_ref[buf, r0:, cs] = log_sig[:, cs] + ca[:, :blk]
            tot_ref[buf, r0:, cs] = ca[:, blk:]

    def stage2(j, r0, buf):
        arg = pre_ref[buf, r0:, :] + car_ref[r0:, :]
        acc_ref[r0:, :] += _dot(jnp.exp2(arg).astype(BF16), block_at(v_ref, j))
        car_ref[r0:, :] += tot_ref[buf, r0:, :]

    acc_ref[...] = jnp.zeros_like(acc_ref)
    car_ref[...] = jnp.zeros_like(car_ref)

    for kj in reversed(range(sub)):
        stage1(n * sub + kj, kj * blk, True, kj % 2)
        stage2(n * sub + kj, kj * blk, kj % 2)

    top = n * sub - 1

    @pl.when(n > 0)
    def _():
        stage1(top, 0, False, 0)

    unroll = 4

    def body(i, c):
        j0 = top - unroll * i
        for u in range(unroll):
            nxt = j0 - u - 1
            if u == unroll - 1:
                nxt = jnp.maximum(nxt, 0)
            stage1(nxt, 0, False, (u + 1) % 2)
            stage2(j0 - u, 0, u % 2)
        return c

    lax.fori_loop(0, n * (sub // unroll), body, 0)
    o_ref[...] = acc_ref[...].astype(o_ref.dtype)


def _sb_attn(za, batch, seq):
    za3 = za.reshape(batch, seq, 3 * SB_WIDTH)
    pairs = SB_WIDTH // LANES
    tq = min(SB_QTILE, seq)
    return pl.pallas_call(
        _sb_kernel,
        grid=(batch, pairs, seq // tq),
        in_specs=[pl.BlockSpec((None, tq, LANES), lambda b, p, n: (b, n, p)),
                  pl.BlockSpec((None, seq, LANES), lambda b, p, n: (b, 0, pairs + p)),
                  pl.BlockSpec((None, seq, LANES), lambda b, p, n: (b, 0, 2 * pairs + p))],
        out_specs=pl.BlockSpec((None, tq, LANES), lambda b, p, n: (b, n, p)),
        out_shape=jax.ShapeDtypeStruct((batch, seq, SB_WIDTH), BF16),
        scratch_shapes=[pltpu.VMEM((tq, LANES), F32),
                        pltpu.VMEM((tq, 2 * LANES), F32),
                        pltpu.VMEM((2, tq, 2 * LANES), F32),
                        pltpu.VMEM((2, tq, 2 * LANES), F32)],
        compiler_params=_cparams(("parallel", "parallel", "arbitrary")),
        name="sb_attn",
    )(za3, za3, za3)


def _mlstm_kernel(qk_ref, vo_ref, g_ref, cw_ref, cb_ref, gb_ref, mg_ref, o_ref,
                  q_s, k_s, vt_s, cum_s, gmb_s, pm_s, gmbt_s, ct_s, m_s):
    seq = qk_ref.shape[0]
    L = ML_CHUNK
    cw = cw_ref[...]
    cb = cb_ref[...]
    rows = lax.broadcasted_iota(I32, (L, 2 * ML_PAD), 0)

    def conv_silu(r0, taps):
        y = taps[0] * cw[CONV_K - 1:CONV_K, :] + cb
        for d in range(1, CONV_K):
            y = y + taps[d] * cw[CONV_K - 1 - d:CONV_K - d, :]
        y = y * _sigmoid(y)
        q_s[pl.ds(r0, L), :] = y[:, :ML_PAD].astype(BF16)
        k_s[pl.ds(r0, L), :] = (y[:, ML_PAD:] * (ML_HEAD_DIM ** -0.5)).astype(BF16)

    ct_s[...] = jnp.zeros_like(ct_s)
    m_s[...] = jnp.zeros_like(m_s)

    ti = lax.broadcasted_iota(I32, (L, L), 0)
    si = lax.broadcasted_iota(I32, (L, L), 1)
    causal = si <= ti
    tril = jnp.where(causal, 1.0, 0.0).astype(BF16)
    lane = lax.broadcasted_iota(I32, (L, LANES), 1)
    row = lax.broadcasted_iota(I32, (L, LANES), 0)
    ones2 = jnp.full((2 * LANES, LANES), 1.0, BF16)
    gb = gb_ref[...]
    mg = mg_ref[...]
    norm_lane = LANES - 1

    def gates(c):
        r0 = pl.multiple_of(c * L, L)
        g = g_ref[pl.ds(r0, L), :] + gb
        logf = jnp.minimum(g, 0.0) - jnp.log1p(jnp.exp(-jnp.abs(g)))
        cum = pltpu.roll(_dot_split_rhs(tril, logf), LANES - ML_HEADS, axis=1)
        gmb = g - cum
        pm = gmb
        step = 1
        while step < L:
            pm = jnp.maximum(pm, jnp.where(row >= step, pltpu.roll(pm, step, axis=0), -jnp.inf))
            step *= 2
        cum_s[pl.ds(r0, L), :] = cum
        gmb_s[pl.ds(r0, L), :] = gmb
        pm_s[pl.ds(r0, L), :] = pm
        gmbt_s[c] = jnp.transpose(gmb)
        for h in range(ML_HEADS):
            v = vo_ref[pl.ds(r0, L), h * LANES:(h + 1) * LANES].astype(F32)
            vt_s[h, c] = jnp.transpose(jnp.where(lane == norm_lane, 1.0, v)).astype(BF16)

    x0 = qk_ref[0:L, :]
    conv_silu(0, [x0] + [jnp.where(rows >= d, pltpu.roll(x0, d, axis=0), 0.0) for d in range(1, CONV_K)])

    def conv_chunk(c, carry):
        r0 = pl.multiple_of(c * L, L)
        xe = qk_ref[pl.ds(r0 - 8, L + 8), :]
        conv_silu(r0, [xe[8 - d:8 - d + L, :] for d in range(CONV_K)])
        return carry

    lax.fori_loop(1, seq // L, conv_chunk, 0)

    def gate_chunk(c, carry):
        gates(2 * c)
        gates(2 * c + 1)
        return carry

    lax.fori_loop(0, seq // (2 * L), gate_chunk, 0)

    heads = range(ML_HEADS)

    def chunk(c, carry):
        r0 = pl.multiple_of(c * L, L)
        cum = cum_s[pl.ds(r0, L), :]
        gmb = gmb_s[pl.ds(r0, L), :]
        pm = pm_s[pl.ds(r0, L), :]
        gmb_t = gmbt_s[c]
        qc = [q_s[pl.ds(r0, L), h * LANES:(h + 1) * LANES] for h in heads]
        kc = [k_s[pl.ds(r0, L), h * LANES:(h + 1) * LANES] for h in heads]
        vt = [vt_s[h, c] for h in heads]
        m_in = [m_s[h][0:1, 0:1] for h in heads]
        ct = [ct_s[h] for h in heads]
        s_raw = [_dot_nt(qc[h], kc[h]) for h in heads]
        mm = [jnp.maximum(m_in[h], pm[:, h:h + 1]) for h in heads]
        inter = [jnp.exp(m_in[h] - mm[h]) for h in heads]
        num = []
        for h in heads:
            wd = jnp.where(causal, jnp.exp(gmb_t[h:h + 1, :] - mm[h]), 0.0)
            lhs = jnp.concatenate([(s_raw[h] * wd).astype(BF16),
                                   (qc[h].astype(F32) * inter[h]).astype(BF16)], axis=1)
            rhs_t = jnp.concatenate([vt[h], ct[h].astype(BF16)], axis=1)
            num.append(_dot_nt(lhs, rhs_t))
        hh = []
        ms = []
        for h in heads:
            den = jnp.maximum(jnp.abs(num[h][:, norm_lane:]), jnp.exp(-(cum[:, h:h + 1] + mm[h])))
            x = jnp.where(lane == norm_lane, 0.0, num[h] * (1.0 / den))
            hi, lo = _split_bf16(x * x)
            hh.append(x)
            ms.append(_dot(jnp.concatenate([hi, lo], axis=1), ones2) * (1.0 / ML_HEAD_DIM))
        for h in heads:
            cs = slice(h * LANES, (h + 1) * LANES)
            oc = vo_ref[pl.ds(r0, L), ML_PAD + h * LANES:ML_PAD + (h + 1) * LANES].astype(F32)
            o_ref[pl.ds(r0, L), cs] = (hh[h] * lax.rsqrt(ms[h] + EPS) * mg[:, cs]
                                       * _sigmoid(oc)).astype(o_ref.dtype)
        for h in heads:
            mm_last = mm[h][L - 1:L, :]
            kw = (kc[h].astype(F32) * jnp.exp(gmb[:, h:h + 1] - mm_last)).astype(BF16)
            ct_s[h] = inter[h][L - 1:L, :] * ct[h] + _dot(vt[h], kw)
            m_s[h] = jnp.broadcast_to(cum[L - 1:L, h:h + 1] + mm_last, (8, LANES))
        return carry

    lax.fori_loop(0, seq // L, chunk, 0)


def _mlstm(zqk, zvo, zg, cw, cb, gb, mg, batch, seq):
    full = lambda a: pl.BlockSpec(a.shape, lambda b: (0,) * a.ndim)
    seq_blk = lambda n: pl.BlockSpec((None, seq, n), lambda b: (b, 0, 0))
    return pl.pallas_call(
        _mlstm_kernel,
        grid=(batch,),
        in_specs=[seq_blk(2 * ML_PAD), seq_blk(2 * ML_PAD), seq_blk(LANES),
                  full(cw), full(cb), full(gb), full(mg)],
        out_specs=seq_blk(ML_PAD),
        out_shape=jax.ShapeDtypeStruct((batch, seq, ML_PAD), BF16),
        scratch_shapes=[pltpu.VMEM((seq, ML_PAD), BF16), pltpu.VMEM((seq, ML_PAD), BF16),
                        pltpu.VMEM((ML_HEADS, seq // ML_CHUNK, LANES, ML_CHUNK), BF16),
                        pltpu.VMEM((seq, LANES), F32), pltpu.VMEM((seq, LANES), F32),
                        pltpu.VMEM((seq, LANES), F32),
                        pltpu.VMEM((seq // ML_CHUNK, LANES, ML_CHUNK), F32),
                        pltpu.VMEM((ML_HEADS, LANES, LANES), F32),
                        pltpu.VMEM((ML_HEADS, 8, LANES), F32)],
        compiler_params=_cparams(("parallel",)),
        name="mlstm",
    )(zqk.reshape(batch, seq, 2 * ML_PAD), zvo.reshape(batch, seq, 2 * ML_PAD),
      zg.reshape(batch, seq, LANES), cw, cb, gb, mg)


def _dispatch_rows(tm):
    return 2 * tm + N_EXPERTS * (CHUNK_ROWS - 1) + (N_EXPERTS * (CHUNK_ROWS - 1)) % CHUNK_ROWS


def _mix_out_kernel(h_ref, ya_ref, yb_ref, yc_ref, ag_ref, wa_ref, wb_ref, wc_ref, n2_ref,
                    rwh_ref, rwl_ref, rb_ref, h1_ref, xs_ref, meta_ref, cnt_ref):
    tm = h_ref.shape[0]
    rr = xs_ref.shape[0]
    ya = _rms(ya_ref[...].astype(F32), ag_ref[...]).astype(BF16)
    h1 = h_ref[...] + _dot(ya, wa_ref[...]) + _dot(yb_ref[...], wb_ref[...]) + _dot(yc_ref[...], wc_ref[...])
    h1_ref[...] = h1
    xn = _rms(h1, n2_ref[...])
    x_hi, x_lo = _split_bf16(xn)
    logits = (_dot_nt(rwh_ref[...], x_hi) + _dot_nt(rwl_ref[...], x_hi) + _dot_nt(rwh_ref[...], x_lo)
              + rb_ref[...])
    e_log = logits[:N_EXPERTS]
    g_log = logits[N_EXPERTS:N_EXPERTS + N_GROUPS]
    g_iota = lax.broadcasted_iota(I32, (N_GROUPS, tm), 0)
    g_max = jnp.max(g_log, axis=0, keepdims=True)
    g_top = jnp.min(jnp.where(g_log == g_max, g_iota, N_GROUPS), axis=0, keepdims=True)
    g_w = 1.0 / jnp.sum(jnp.exp(g_log - g_max), axis=0, keepdims=True)
    e_iota = lax.broadcasted_iota(I32, (N_EXPERTS, tm), 0)
    val = jnp.where(e_iota // EXPERTS_PER_GROUP == g_top, e_log, -jnp.inf)
    m1 = jnp.max(val, axis=0, keepdims=True)
    i1 = jnp.min(jnp.where(val == m1, e_iota, N_EXPERTS), axis=0, keepdims=True)
    val2 = jnp.where(e_iota == i1, -jnp.inf, val)
    m2 = jnp.max(val2, axis=0, keepdims=True)
    i2 = jnp.min(jnp.where(val2 == m2, e_iota, N_EXPERTS), axis=0, keepdims=True)
    e2 = jnp.exp(m2 - m1)
    w0 = g_w / (1.0 + e2)
    w1 = g_w * e2 / (1.0 + e2)
    oh0 = e_iota == i1
    oh1 = e_iota == i2
    oh = jnp.concatenate([jnp.where(oh0, 1.0, 0.0), jnp.where(oh1, 1.0, 0.0)], axis=0).astype(BF16)
    ta = lax.broadcasted_iota(I32, (tm, tm), 0)
    tb = lax.broadcasted_iota(I32, (tm, tm), 1)
    before = jnp.where(ta < tb, 1.0, 0.0).astype(BF16)
    rank = _dot(oh, before)
    c0 = jnp.sum(jnp.where(oh0, 1.0, 0.0), axis=1, keepdims=True)
    c1 = jnp.sum(jnp.where(oh1, 1.0, 0.0), axis=1, keepdims=True)
    nchunk = jnp.floor((c0 + c1 + (CHUNK_ROWS - 1)) * (1.0 / CHUNK_ROWS))
    nchunk_b = jnp.broadcast_to(nchunk, (N_EXPERTS, LANES))
    ea = lax.broadcasted_iota(I32, (N_EXPERTS, N_EXPERTS), 0)
    eb = lax.broadcasted_iota(I32, (N_EXPERTS, N_EXPERTS), 1)
    lower = jnp.where(eb < ea, 1.0, 0.0).astype(BF16)
    loc = CHUNK_ROWS * _dot(lower, nchunk_b.astype(BF16))[:, 0:1]
    dest0 = jnp.sum(jnp.where(oh0, loc + rank[:N_EXPERTS], 0.0), axis=0, keepdims=True)
    dest1 = jnp.sum(jnp.where(oh1, loc + c0 + rank[N_EXPERTS:], 0.0), axis=0, keepdims=True)
    r_iota = lax.broadcasted_iota(I32, (rr, tm), 0)
    d0 = dest0.astype(I32)
    d1 = dest1.astype(I32)
    perm = jnp.where(r_iota == d0, 1.0, jnp.where(r_iota == d1, 1.0, 0.0)).astype(BF16)
    xs_ref[...] = _dot(perm, x_hi).astype(BF16)
    cnt_ref[...] = nchunk_b
    m_iota = lax.broadcasted_iota(I32, (8, tm), 0)
    meta = jnp.where(m_iota == 0, dest0, jnp.where(m_iota == 1, dest1, jnp.where(m_iota == 2, w0,
                     jnp.where(m_iota == 3, w1, 0.0))))
    meta = jnp.concatenate([meta, jnp.zeros((LANES - 8, tm), F32)], axis=0)
    meta_ref[...] = jnp.transpose(meta)


def _mix_out(h, ya, yb, yc, ag, wa, wb, wc, n2, rwh, rwl, rb, tm):
    t = h.shape[0]
    nt = t // tm
    rr = _dispatch_rows(tm)
    row = lambda n: pl.BlockSpec((tm, n), lambda i: (i, 0))
    full = lambda a: pl.BlockSpec(a.shape, lambda i: (0,) * a.ndim)
    return pl.pallas_call(
        _mix_out_kernel,
        grid=(nt,),
        in_specs=[row(D_MODEL), row(SB_WIDTH), row(ML_PAD), row(SGU_WIDTH), full(ag), full(wa), full(wb),
                  full(wc), full(n2), full(rwh), full(rwl), full(rb)],
        out_specs=[row(D_MODEL), pl.BlockSpec((rr, D_MODEL), lambda i: (i, 0)), row(LANES),
                   pl.BlockSpec((None, N_EXPERTS, LANES), lambda i: (i, 0, 0))],
        out_shape=[jax.ShapeDtypeStruct((t, D_MODEL), F32),
                   jax.ShapeDtypeStruct((nt * rr, D_MODEL), BF16),
                   jax.ShapeDtypeStruct((t, LANES), F32),
                   jax.ShapeDtypeStruct((nt, N_EXPERTS, LANES), F32)],
        compiler_params=_cparams(("parallel",)),
        name="mix_out",
    )(h, ya, yb, yc, ag, wa, wb, wc, n2, rwh, rwl, rb)


def _chunk_tables(cnt, rr, n_blocks_max):
    nt = cnt.shape[0]
    ne = N_EXPERTS + 1
    tail = rr // CHUNK_ROWS - jnp.sum(cnt, axis=1)
    cnt = jnp.concatenate([cnt, tail[:, None]], axis=1)
    loc = (jnp.cumsum(cnt, axis=1) - cnt) * CHUNK_ROWS
    cnt_e = cnt.T
    cum_e = jnp.cumsum(cnt_e, axis=1)
    total = cum_e[:, -1]
    nblk = (total + EXPERT_BLOCK_CHUNKS - 1) // EXPERT_BLOCK_CHUNKS
    bend = jnp.cumsum(nblk)
    bstart = bend - nblk
    n_blocks = bend[-1]
    i = jnp.arange(n_blocks_max, dtype=I32)
    be = jnp.minimum(jnp.sum(bend[None, :] <= i[:, None], axis=1), ne - 1).astype(I32)
    oh_b = be[:, None] == jnp.arange(ne, dtype=I32)[None, :]
    pick = lambda v: jnp.sum(jnp.where(oh_b, v[None, :], 0), axis=1)
    pick2 = lambda m: jnp.sum(jnp.where(oh_b[:, :, None], m[None, :, :], 0), axis=1)
    q = (i - pick(bstart))[:, None] * EXPERT_BLOCK_CHUNKS + jnp.arange(EXPERT_BLOCK_CHUNKS, dtype=I32)[None, :]
    valid = (q < pick(total)[:, None]) & (i < n_blocks)[:, None]
    cum_b = pick2(cum_e)
    base = jnp.arange(nt, dtype=I32)[None, :] * rr + loc.T - (cum_e - cnt_e) * CHUNK_ROWS
    base_b = pick2(base)
    j = jnp.minimum(jnp.sum(cum_b[:, None, :] <= q[:, :, None], axis=-1), nt - 1)
    oh_j = j[:, :, None] == jnp.arange(nt, dtype=I32)[None, None, :]
    row = jnp.sum(jnp.where(oh_j, base_b[:, None, :], 0), axis=-1) + q * CHUNK_ROWS
    spare = nt * rr + ((i % 2)[:, None] * EXPERT_BLOCK_CHUNKS
                       + jnp.arange(EXPERT_BLOCK_CHUNKS, dtype=I32)[None, :]) * CHUNK_ROWS
    src = jnp.where(valid, row, 0).astype(I32)
    dst = jnp.where(valid, row, spare).astype(I32)
    return be, src.reshape(-1), dst.reshape(-1), n_blocks.reshape(1).astype(I32)


def _experts_kernel(be_ref, src_ref, dst_ref, nb_ref, xs_ref, wg_ref, wu_ref, wd_ref, ys_ref,
                    xbuf, ybuf, wgb, wub, wdb, sem_in, sem_out):
    i = pl.program_id(0)
    nb = nb_ref[0]
    slot = lax.rem(i, 2)

    def is_compute(b):
        return be_ref[b] < N_EXPERTS

    def start_loads(b, sl):
        @pl.when(is_compute(b))
        def _():
            for s in range(EXPERT_BLOCK_CHUNKS):
                src = pl.multiple_of(src_ref[b * EXPERT_BLOCK_CHUNKS + s], CHUNK_ROWS)
                pltpu.make_async_copy(xs_ref.at[pl.ds(src, CHUNK_ROWS), :],
                                      xbuf.at[sl, s * CHUNK_ROWS:(s + 1) * CHUNK_ROWS, :], sem_in.at[sl]).start()

    def wait_loads(b, sl):
        @pl.when(is_compute(b))
        def _():
            pltpu.make_async_copy(xs_ref.at[0:EXPERT_BLOCK, :], xbuf.at[sl], sem_in.at[sl]).wait()

    def start_stores(b, sl):
        for s in range(EXPERT_BLOCK_CHUNKS):
            dst = pl.multiple_of(dst_ref[b * EXPERT_BLOCK_CHUNKS + s], CHUNK_ROWS)
            pltpu.make_async_copy(ybuf.at[sl, s * CHUNK_ROWS:(s + 1) * CHUNK_ROWS, :],
                                  ys_ref.at[pl.ds(dst, CHUNK_ROWS), :], sem_out.at[sl]).start()

    def wait_stores(sl):
        pltpu.make_async_copy(ybuf.at[sl], ys_ref.at[0:EXPERT_BLOCK, :], sem_out.at[sl]).wait()

    @pl.when(i == 0)
    def _():
        ybuf[...] = jnp.zeros(ybuf.shape, BF16)
        real_rows = ys_ref.shape[0] - 2 * EXPERT_BLOCK
        for par in range(2):
            fill = pltpu.make_async_copy(
                ybuf.at[par], ys_ref.at[real_rows + par * EXPERT_BLOCK:real_rows + (par + 1) * EXPERT_BLOCK, :],
                sem_out.at[par])
            fill.start()
            fill.wait()

        start_loads(0, 0)

    @pl.when(i < nb)
    def _():
        @pl.when(i + 1 < nb)
        def _():
            start_loads(i + 1, 1 - slot)

        wait_loads(i, slot)

        @pl.when(i >= 2)
        def _():
            wait_stores(slot)

        @pl.when(is_compute(i))
        def _():
            @pl.when((i == 0) | (be_ref[i] != be_ref[jnp.maximum(i - 1, 0)]))
            def _():
                wgb[...] = wg_ref[...].astype(BF16)
                wub[...] = wu_ref[...].astype(BF16)
                wdb[...] = wd_ref[...].astype(BF16)

            x = xbuf[slot]
            gate = _dot(x, wgb[...])
            up = _dot(x, wub[...])
            mid = (gate * _sigmoid(gate) * up).astype(BF16)
            ybuf[slot] = _dot(mid, wdb[...]).astype(BF16)

        @pl.when(jnp.logical_not(is_compute(i)))
        def _():
            ybuf[slot] = jnp.zeros((EXPERT_BLOCK, D_MODEL), BF16)

        start_stores(i, slot)

        @pl.when(i == nb - 1)
        def _():
            @pl.when(i >= 1)
            def _():
                wait_stores(1 - slot)

            wait_stores(slot)


def _experts(be, src, dst, nb, xs, wg, wu, wd, layer):
    n_blocks_max = be.shape[0]
    wspec = lambda a: pl.BlockSpec(
        (None,) + a.shape[1:],
        lambda i, be, src, dst, nb: (layer * N_EXPERTS + jnp.minimum(be[i], N_EXPERTS - 1), 0, 0))
    grid_spec = pltpu.PrefetchScalarGridSpec(
        num_scalar_prefetch=4,
        grid=(n_blocks_max,),
        in_specs=[pl.BlockSpec(memory_space=pl.ANY), wspec(wg), wspec(wu), wspec(wd)],
        out_specs=pl.BlockSpec(memory_space=pl.ANY),
        scratch_shapes=[pltpu.VMEM((2, EXPERT_BLOCK, D_MODEL), BF16), pltpu.VMEM((2, EXPERT_BLOCK, D_MODEL), BF16),
                        pltpu.VMEM(wg.shape[1:], BF16), pltpu.VMEM(wu.shape[1:], BF16),
                        pltpu.VMEM(wd.shape[1:], BF16),
                        pltpu.SemaphoreType.DMA((2,)), pltpu.SemaphoreType.DMA((2,))],
    )
    return pl.pallas_call(
        _experts_kernel,
        grid_spec=grid_spec,
        out_shape=jax.ShapeDtypeStruct((xs.shape[0] + 2 * EXPERT_BLOCK, D_MODEL), BF16),
        compiler_params=_cparams(("arbitrary",)),
        name="experts",
    )(be, src, dst, nb, xs, wg, wu, wd)


def _combine_kernel(h_ref, ys_ref, meta_ref, p_ref, pg_ref, gw_ref, pw_ref, fg_ref, o_ref, *, final):
    tm = h_ref.shape[0]
    rr = ys_ref.shape[0]
    meta = meta_ref[...]
    d0 = meta[:, 0:1].astype(I32)
    d1 = meta[:, 1:2].astype(I32)
    r_iota = lax.broadcasted_iota(I32, (tm, rr), 1)
    unperm = (jnp.where(r_iota == d0, meta[:, 2:3], 0.0) + jnp.where(r_iota == d1, meta[:, 3:4], 0.0)).astype(BF16)
    h2 = h_ref[...] + _dot(unperm, ys_ref[...])
    gate = _sigmoid(_dot(_rms(h2, pg_ref[...]).astype(BF16), gw_ref[...]))
    h3 = h2 + gate * _dot(p_ref[...].astype(BF16), pw_ref[...])
    if final:
        h3 = _rms(h3, fg_ref[...])
    o_ref[...] = h3


def _combine(h1, ys, meta, p, layer, pg, gw, pw, fg, tm, final):
    t = h1.shape[0]
    nt = t // tm
    rr = _dispatch_rows(tm)
    row = lambda n: pl.BlockSpec((tm, n), lambda i: (i, 0))
    full = lambda a: pl.BlockSpec(a.shape, lambda i: (0,) * a.ndim)
    return pl.pallas_call(
        functools.partial(_combine_kernel, final=final),
        grid=(nt,),
        in_specs=[row(D_MODEL), pl.BlockSpec((rr, D_MODEL), lambda i: (i, 0)), row(LANES),
                  pl.BlockSpec((tm, P_DIM), lambda i: (layer * nt + i, 0)),
                  full(pg), full(gw), full(pw), full(fg)],
        out_specs=row(D_MODEL),
        out_shape=jax.ShapeDtypeStruct((t, D_MODEL), F32),
        compiler_params=_cparams(("parallel",)),
        name="combine",
    )(h1, ys, meta, p, pg, gw, pw, fg)


def _pad_heads(a, axis):
    shape = a.shape
    a = a.reshape(shape[:axis] + (ML_HEADS, ML_HEAD_DIM) + shape[axis + 1:])
    pad = [(0, 0)] * a.ndim
    pad[axis + 1] = (0, LANES - ML_HEAD_DIM)
    a = jnp.pad(a, pad)
    return a.reshape(shape[:axis] + (ML_PAD,) + shape[axis + 1:])


def _layer_params(i, w_in, conv_w, conv_b, igate_b, fgate_b, mnorm_g, sgu_b, w_out, router_gw, router_gb,
                  router_ew, router_eb):
    w = w_in[i]
    s, m = SB_WIDTH, ML_WIDTH
    a_q, a_k, a_v = w[:, 0:s] * (SB_HEAD_DIM ** -0.5 * LOG2_E), w[:, s:2 * s], w[:, 2 * s:3 * s]
    o = 3 * s
    b_q, b_k, b_v, b_o = (w[:, o + k * m:o + (k + 1) * m] for k in range(4))
    o = o + 4 * m
    gates = w[:, o:o + 2 * ML_HEADS]
    c_uv = w[:, o + 2 * ML_HEADS:]
    w_r = jnp.concatenate([a_q, a_k, a_v, _pad_heads(b_q, 1), _pad_heads(b_k, 1), _pad_heads(b_v, 1),
                           _pad_heads(b_o, 1), c_uv,
                           jnp.pad(gates, ((0, 0), (0, LANES - 2 * ML_HEADS)))], axis=1).astype(BF16)
    cw = jnp.concatenate([_pad_heads(conv_w[i][:, :m], 1), _pad_heads(conv_w[i][:, m:], 1)], axis=1)
    cb = jnp.concatenate([_pad_heads(conv_b[i][:m], 0), _pad_heads(conv_b[i][m:], 0)])[None, :]
    gb = jnp.pad(jnp.concatenate([igate_b[i], fgate_b[i]]), (0, LANES - 2 * ML_HEADS))[None, :]
    mg = _pad_heads(mnorm_g[i], 0)[None, :]
    sgu_bias = jnp.repeat(sgu_b[i].T, SGU_GROUP_DIM, axis=1)
    wo = w_out[i]
    wa = wo[:s].astype(BF16)
    wb = _pad_heads(wo[s:s + m], 0).astype(BF16)
    wc = wo[s + m:].astype(BF16)
    rw = jnp.concatenate([router_ew[i].T, router_gw[i].T,
                          jnp.zeros((ROUTER_ROWS - N_EXPERTS - N_GROUPS, D_MODEL), F32)], axis=0)
    rwh = rw.astype(BF16)
    rwl = (rw - rwh.astype(F32)).astype(BF16)
    rb = jnp.concatenate([router_eb[i], router_gb[i],
                          jnp.zeros((ROUTER_ROWS - N_EXPERTS - N_GROUPS,), F32)])[:, None]
    return w_r, cw, cb, gb, mg, sgu_bias, wa, wb, wc, rwh, rwl, rb


def kernel(x, p, norm1_g, w_in, conv_w, conv_b, igate_b, fgate_b, mnorm_g, sb_out_g, sgu_ln_g, sgu_ln_b, sgu_w,
           sgu_b, sgu_out_g, w_out, norm2_g, router_gw, router_gb, router_ew, router_eb, w_gate, w_up, w_down,
           ple_norm_g, ple_gate_w, ple_proj_w, final_g, *, tile=512):
    batch, seq, d = x.shape
    depth = w_in.shape[0]
    t = batch * seq
    tm = min(tile, t)
    nt = t // tm
    rr = _dispatch_rows(tm)
    n_blocks_max = (nt * rr // CHUNK_ROWS) // EXPERT_BLOCK_CHUNKS + N_EXPERTS + 1
    h = x.astype(F32).reshape(t, d)
    p2 = p.reshape(depth * t, P_DIM)
    wg_all = w_gate.reshape((depth * N_EXPERTS,) + w_gate.shape[2:])
    wu_all = w_up.reshape((depth * N_EXPERTS,) + w_up.shape[2:])
    wd_all = w_down.reshape((depth * N_EXPERTS,) + w_down.shape[2:])
    for i in range(depth):
        (w_r, cw, cb, gb, mg, sgu_bias, wa, wb, wc, rwh, rwl, rb) = _layer_params(
            i, w_in, conv_w, conv_b, igate_b, fgate_b, mnorm_g, sgu_b, w_out, router_gw, router_gb,
            router_ew, router_eb)
        za, zqk, zvo, yc, zg = _in_proj(h, norm1_g[i][None, :], w_r, sgu_ln_g[i][None, :], sgu_ln_b[i][None, :],
                                        sgu_w[i], sgu_bias, sgu_out_g[i][None, :], tm)
        ya = _sb_attn(za, batch, seq).reshape(t, SB_WIDTH)
        yb = _mlstm(zqk, zvo, zg, cw, cb, gb, mg, batch, seq).reshape(t, ML_PAD)
        h1, xs, meta, cnt = _mix_out(h, ya, yb, yc, sb_out_g[i][None, :], wa, wb, wc, norm2_g[i][None, :],
                                     rwh, rwl, rb, tm)
        be, src, dst, nb = _chunk_tables(cnt[:, :, 0].astype(I32), rr, n_blocks_max)
        ys = _experts(be, src, dst, nb, xs, wg_all, wu_all, wd_all, i)
        h = _combine(h1, ys, meta, p2, i, ple_norm_g[i][None, :], ple_gate_w[i].astype(BF16),
                     ple_proj_w[i].astype(BF16), final_g[None, :], tm, i == depth - 1)
    return h.reshape(batch, seq, d).astype(x.dtype)
```

```python
import functools

import jax
import jax.numpy as jnp
import numpy as np
from jax import lax
from jax.experimental import pallas as pl
from jax.experimental.pallas import tpu as pltpu

F32 = jnp.float32
BF16 = jnp.bfloat16
I32 = jnp.int32

D_MODEL = 1024
P_DIM = 256
EPS = 1e-6
LANES = 128
SB_HEAD_DIM = 64
SB_WIDTH = 384
SB_BLOCK = 128
SB_QTILE = 512
LOG2_E = 1.4426950408889634
ML_HEADS = 4
ML_HEAD_DIM = 96
ML_WIDTH = 384
ML_PAD = ML_HEADS * LANES
ML_CHUNK = 128
CONV_K = 4
SGU_WIDTH = 256
SGU_GROUPS = 4
SGU_GROUP_DIM = 64
SGU_BLOCK = 128
STREAM_CHUNK = 64
N_GROUPS = 4
EXPERTS_PER_GROUP = 8
N_EXPERTS = 32
D_EXPERT = 256
ROUTER_ROWS = 40
CHUNK_ROWS = 16
EXPERT_BLOCK_CHUNKS = 32
EXPERT_BLOCK = CHUNK_ROWS * EXPERT_BLOCK_CHUNKS

COL_A = 0
COL_BQK = 3 * SB_WIDTH
COL_BVO = COL_BQK + 2 * ML_PAD
COL_C = COL_BVO + 2 * ML_PAD
COL_G = COL_C + 2 * SGU_WIDTH
N_Z = COL_G + LANES

VMEM_LIMIT = 56 * 1024 * 1024


def _cparams(sem, vmem=VMEM_LIMIT):
    return pltpu.CompilerParams(dimension_semantics=sem, vmem_limit_bytes=vmem)


def _rms(x, g):
    return x * lax.rsqrt(jnp.mean(x * x, axis=-1, keepdims=True) + EPS) * g


def _sigmoid(x):
    return 0.5 * jnp.tanh(0.5 * x) + 0.5


def _split_bf16(x):
    hi = x.astype(BF16)
    lo = (x - hi.astype(F32)).astype(BF16)
    return hi, lo


def _dot(a, b):
    return jnp.dot(a, b, preferred_element_type=F32)


def _dot_nt(a, b):
    return lax.dot_general(a, b, (((1,), (1,)), ((), ())), preferred_element_type=F32)


def _dot_tn(a, b):
    return lax.dot_general(a, b, (((0,), (0,)), ((), ())), preferred_element_type=F32)


def _dot_split_lhs(x, m):
    hi, lo = _split_bf16(x)
    return _dot(hi, m) + _dot(lo, m)


def _dot_split_rhs(m, x):
    hi, lo = _split_bf16(x)
    return _dot(m, hi) + _dot(m, lo)


def _spatial_gating(zc, lg, lb, w_ref, bias, og):
    W = SGU_WIDTH
    gi = lax.broadcasted_iota(I32, (W, W), 0) // SGU_GROUP_DIM
    gj = lax.broadcasted_iota(I32, (W, W), 1) // SGU_GROUP_DIM
    avg = jnp.where(gi == gj, 1.0 / SGU_GROUP_DIM, 0.0).astype(BF16)
    ti = lax.broadcasted_iota(I32, (SGU_BLOCK, SGU_BLOCK), 0) // STREAM_CHUNK
    si = lax.broadcasted_iota(I32, (SGU_BLOCK, SGU_BLOCK), 1) // STREAM_CHUNK
    chunk_causal = si <= ti
    lane_group = lax.broadcasted_iota(I32, (SGU_BLOCK, W), 1) // SGU_GROUP_DIM
    wg = [jnp.where(chunk_causal, w_ref[g], 0.0).astype(BF16) for g in range(SGU_GROUPS)]
    wpair = [jnp.concatenate(wg[g:g + 2], axis=1) for g in range(0, SGU_GROUPS, 2)]
    out = []
    for r in range(zc.shape[0] // SGU_BLOCK):
        rs = slice(r * SGU_BLOCK, (r + 1) * SGU_BLOCK)
        u = jax.nn.gelu(zc[rs, :W])
        v = jax.nn.gelu(zc[rs, W:])
        mu = _dot_split_lhs(v, avg)
        vc = v - mu
        var = _dot_split_lhs(vc * vc, avg)
        vn = (vc * lax.rsqrt(var + EPS) * lg + lb).astype(BF16)
        mixed = bias
        zero = jnp.zeros_like(vn)
        for p, g in enumerate(range(0, SGU_GROUPS, 2)):
            vpair = jnp.concatenate([jnp.where(lane_group == g, vn, zero),
                                     jnp.where(lane_group == g + 1, vn, zero)], axis=0)
            mixed = mixed + _dot(wpair[p], vpair)
        out.append(_rms(u * mixed, og))
    return jnp.concatenate(out, axis=0)


def _in_proj_kernel(x_ref, g_ref, w_ref, lg_ref, lb_ref, sw_ref, sb_ref, og_ref,
                    za_ref, zqk_ref, zvo_ref, yc_ref, zg_ref):
    hn = _rms(x_ref[...], g_ref[...]).astype(BF16)
    za_ref[...] = _dot(hn, w_ref[:, COL_A:COL_BQK]).astype(BF16)
    zqk_ref[...] = _dot(hn, w_ref[:, COL_BQK:COL_BVO])
    zvo_ref[...] = _dot(hn, w_ref[:, COL_BVO:COL_C]).astype(BF16)
    zg_ref[...] = _dot(hn, w_ref[:, COL_G:N_Z])
    zc = _dot(hn, w_ref[:, COL_C:COL_G])
    yc_ref[...] = _spatial_gating(zc, lg_ref[...], lb_ref[...], sw_ref, sb_ref[...], og_ref[...]).astype(BF16)


def _in_proj(h, g, w, lg, lb, sw, sbias, og, tm):
    t = h.shape[0]
    row = lambda n: pl.BlockSpec((tm, n), lambda i: (i, 0))
    full = lambda a: pl.BlockSpec(a.shape, lambda i: (0,) * a.ndim)
    resident = pl.BlockSpec(w.shape, lambda i: (0, 0), pipeline_mode=pl.Buffered(1))
    return pl.pallas_call(
        _in_proj_kernel,
        grid=(t // tm,),
        in_specs=[row(D_MODEL), full(g), resident, full(lg), full(lb), full(sw), full(sbias), full(og)],
        out_specs=[row(3 * SB_WIDTH), row(2 * ML_PAD), row(2 * ML_PAD), row(SGU_WIDTH), row(LANES)],
        out_shape=[jax.ShapeDtypeStruct((t, 3 * SB_WIDTH), BF16),
                   jax.ShapeDtypeStruct((t, 2 * ML_PAD), F32),
                   jax.ShapeDtypeStruct((t, 2 * ML_PAD), BF16),
                   jax.ShapeDtypeStruct((t, SGU_WIDTH), BF16),
                   jax.ShapeDtypeStruct((t, LANES), F32)],
        compiler_params=_cparams(("parallel",)),
        name="in_proj",
    )(h, g, w, lg, lb, sw, sbias, og)


def _sb_kernel(q_ref, k_ref, v_ref, o_ref, acc_ref, car_ref, pre_ref, tot_ref):
    n = pl.program_id(2)
    blk = SB_BLOCK
    tq = q_ref.shape[0]
    sub = tq // blk
    lane_k = lax.broadcasted_iota(I32, (blk, LANES), 1)
    head0 = lane_k < SB_HEAD_DIM
    mj = lax.broadcasted_iota(I32, (2 * blk, 2 * blk), 0) % blk
    ms = lax.broadcasted_iota(I32, (2 * blk, 2 * blk), 1)
    suffix = jnp.where((mj > ms) | (ms >= blk), 1.0, 0.0).astype(BF16)

    def per_head(x):
        zero = jnp.zeros_like(x)
        return jnp.concatenate([jnp.where(head0, x, zero), jnp.where(head0, zero, x)], axis=0)

    def block_at(ref, j):
        return per_head(ref[pl.ds(pl.multiple_of(j * blk, blk), blk), :])

    def stage1(j, r0, masked, buf):
        z = _dot_nt(q_ref[r0:, :], block_at(k_ref, j))
        sign = jnp.uint32(0x80000000)
        neg_abs = lax.bitcast_convert_type(lax.bitcast_convert_type(z, jnp.uint32) | sign, F32)
        e = jnp.exp2(neg_abs)
        log_sig = jnp.minimum(z, 0.0) - jnp.log2(1.0 + e)
        log1m = log_sig - z
        if masked:
            mask = (lax.broadcasted_iota(I32, (blk, 2 * blk), 1) % blk
                    < lax.broadcasted_iota(I32, (blk, 2 * blk), 0))
            top_l = jnp.where(mask, log1m[:blk], 0.0)
            top_s = jnp.where(mask, log_sig[:blk], -jnp.inf)
            if tq - r0 == blk:
                log1m, log_sig = top_l, top_s
            else:
                log1m = jnp.concatenate([top_l, log1m[blk:]], axis=0)
                log_sig = jnp.concatenate([top_s, log_sig[blk:]], axis=0)
        hi, lo = _split_bf16(log1m)
        for hh in range(2):
            cs = slice(hh * blk, (hh + 1) * blk)
            ca = _dot(jnp.concatenate([hi[:, cs], lo[:, cs]], axis=1), suffix)
            pre_ref[buf, r0:, cs] = log_sig[:, cs] + ca[:, :blk]
            tot_ref[buf, r0:, cs] = ca[:, blk:]

    def stage2(j, r0, buf):
        arg = pre_ref[buf, r0:, :] + car_ref[r0:, :]
        acc_ref[r0:, :] += _dot(jnp.exp2(arg).astype(BF16), block_at(v_ref, j))
        car_ref[r0:, :] += tot_ref[buf, r0:, :]

    acc_ref[...] = jnp.zeros_like(acc_ref)
    car_ref[...] = jnp.zeros_like(car_ref)

    for kj in reversed(range(sub)):
        stage1(n * sub + kj, kj * blk, True, kj)
    for kj in reversed(range(sub)):
        stage2(n * sub + kj, kj * blk, kj)

    top = n * sub - 1

    @pl.when(n > 0)
    def _():
        stage1(top, 0, False, 0)

    unroll = 4

    def body(i, c):
        j0 = top - unroll * i
        for u in range(unroll):
            nxt = j0 - u - 1
            if u == unroll - 1:
                nxt = jnp.maximum(nxt, 0)
            stage1(nxt, 0, False, (u + 1) % 2)
            stage2(j0 - u, 0, u % 2)
        return c

    lax.fori_loop(0, n * (sub // unroll), body, 0)
    o_ref[...] = acc_ref[...].astype(o_ref.dtype)


def _sb_attn(za, batch, seq):
    za3 = za.reshape(batch, seq, 3 * SB_WIDTH)
    pairs = SB_WIDTH // LANES
    tq = min(SB_QTILE, seq)
    return pl.pallas_call(
        _sb_kernel,
        grid=(batch, pairs, seq // tq),
        in_specs=[pl.BlockSpec((None, tq, LANES), lambda b, p, n: (b, n, p)),
                  pl.BlockSpec((None, seq, LANES), lambda b, p, n: (b, 0, pairs + p)),
                  pl.BlockSpec((None, seq, LANES), lambda b, p, n: (b, 0, 2 * pairs + p))],
        out_specs=pl.BlockSpec((None, tq, LANES), lambda b, p, n: (b, n, p)),
        out_shape=jax.ShapeDtypeStruct((batch, seq, SB_WIDTH), BF16),
        scratch_shapes=[pltpu.VMEM((tq, LANES), F32),
                        pltpu.VMEM((tq, 2 * LANES), F32),
                        pltpu.VMEM((tq // SB_BLOCK, tq, 2 * LANES), F32),
                        pltpu.VMEM((tq // SB_BLOCK, tq, 2 * LANES), F32)],
        compiler_params=_cparams(("parallel", "parallel", "arbitrary")),
        name="sb_attn",
    )(za3, za3, za3)


def _mlstm_kernel(qk_ref, vo_ref, g_ref, cw_ref, cb_ref, gb_ref, mg_ref, o_ref,
                  q_s, k_s, vt_s, cum_s, gmb_s, pm_s, gmbt_s, ct_s, m_s):
    seq = qk_ref.shape[0]
    L = ML_CHUNK
    cw = cw_ref[...]
    cb = cb_ref[...]
    rows = lax.broadcasted_iota(I32, (L, 2 * ML_PAD), 0)

    def conv_silu(r0, taps):
        y = taps[0] * cw[CONV_K - 1:CONV_K, :] + cb
        for d in range(1, CONV_K):
            y = y + taps[d] * cw[CONV_K - 1 - d:CONV_K - d, :]
        y = y * _sigmoid(y)
        q_s[pl.ds(r0, L), :] = y[:, :ML_PAD].astype(BF16)
        k_s[pl.ds(r0, L), :] = (y[:, ML_PAD:] * (ML_HEAD_DIM ** -0.5)).astype(BF16)

    ct_s[...] = jnp.zeros_like(ct_s)
    m_s[...] = jnp.zeros_like(m_s)

    ti = lax.broadcasted_iota(I32, (L, L), 0)
    si = lax.broadcasted_iota(I32, (L, L), 1)
    causal = si <= ti
    tril = jnp.where(causal, 1.0, 0.0).astype(BF16)
    lane = lax.broadcasted_iota(I32, (L, LANES), 1)
    row = lax.broadcasted_iota(I32, (L, LANES), 0)
    ones2 = jnp.full((2 * LANES, LANES), 1.0, BF16)
    gb = gb_ref[...]
    mg = mg_ref[...]
    norm_lane = LANES - 1

    def gates(c):
        r0 = pl.multiple_of(c * L, L)
        g = g_ref[pl.ds(r0, L), :] + gb
        logf = jnp.minimum(g, 0.0) - jnp.log1p(jnp.exp(-jnp.abs(g)))
        cum = pltpu.roll(_dot_split_rhs(tril, logf), LANES - ML_HEADS, axis=1)
        gmb = g - cum
        pm = gmb
        step = 1
        while step < L:
            pm = jnp.maximum(pm, jnp.where(row >= step, pltpu.roll(pm, step, axis=0), -jnp.inf))
            step *= 2
        cum_s[pl.ds(r0, L), :] = cum
        gmb_s[pl.ds(r0, L), :] = gmb
        pm_s[pl.ds(r0, L), :] = pm
        gmbt_s[c] = jnp.transpose(gmb)
        for h in range(ML_HEADS):
            v = vo_ref[pl.ds(r0, L), h * LANES:(h + 1) * LANES].astype(F32)
            vt_s[h, c] = jnp.transpose(jnp.where(lane == norm_lane, 1.0, v)).astype(BF16)

    x0 = qk_ref[0:L, :]
    conv_silu(0, [x0] + [jnp.where(rows >= d, pltpu.roll(x0, d, axis=0), 0.0) for d in range(1, CONV_K)])

    def conv_chunk(c, carry):
        r0 = pl.multiple_of(c * L, L)
        xe = qk_ref[pl.ds(r0 - 8, L + 8), :]
        conv_silu(r0, [xe[8 - d:8 - d + L, :] for d in range(CONV_K)])
        return carry

    lax.fori_loop(1, seq // L, conv_chunk, 0)

    gate_unroll = 4 if (seq // L) % 4 == 0 else 1

    def gate_chunk(c, carry):
        for u in range(gate_unroll):
            gates(gate_unroll * c + u)
        return carry

    lax.fori_loop(0, seq // (gate_unroll * L), gate_chunk, 0)

    heads = range(ML_HEADS)

    def chunk(c, carry):
        r0 = pl.multiple_of(c * L, L)
        cum = cum_s[pl.ds(r0, L), :]
        gmb = gmb_s[pl.ds(r0, L), :]
        pm = pm_s[pl.ds(r0, L), :]
        gmb_t = gmbt_s[c]
        qc = [q_s[pl.ds(r0, L), h * LANES:(h + 1) * LANES] for h in heads]
        kc = [k_s[pl.ds(r0, L), h * LANES:(h + 1) * LANES] for h in heads]
        vt = [vt_s[h, c] for h in heads]
        m_in = [m_s[h][0:1, 0:1] for h in heads]
        ct = [ct_s[h] for h in heads]
        s_raw = [_dot_nt(qc[h], kc[h]) for h in heads]
        mm = [jnp.maximum(m_in[h], pm[:, h:h + 1]) for h in heads]
        inter = [jnp.exp(m_in[h] - mm[h]) for h in heads]
        num = []
        for h in heads:
            wd = jnp.where(causal, jnp.exp(gmb_t[h:h + 1, :] - mm[h]), 0.0)
            lhs = jnp.concatenate([(s_raw[h] * wd).astype(BF16),
                                   (qc[h].astype(F32) * inter[h]).astype(BF16)], axis=1)
            rhs_t = jnp.concatenate([vt[h], ct[h].astype(BF16)], axis=1)
            num.append(_dot_nt(lhs, rhs_t))
        hh = []
        ms = []
        for h in heads:
            den = jnp.maximum(jnp.abs(num[h][:, norm_lane:]), jnp.exp(-(cum[:, h:h + 1] + mm[h])))
            x = jnp.where(lane == norm_lane, 0.0, num[h] * (1.0 / den))
            hi, lo = _split_bf16(x * x)
            hh.append(x)
            ms.append(_dot(jnp.concatenate([hi, lo], axis=1), ones2) * (1.0 / ML_HEAD_DIM))
        for h in heads:
            cs = slice(h * LANES, (h + 1) * LANES)
            oc = vo_ref[pl.ds(r0, L), ML_PAD + h * LANES:ML_PAD + (h + 1) * LANES].astype(F32)
            o_ref[pl.ds(r0, L), cs] = (hh[h] * lax.rsqrt(ms[h] + EPS) * mg[:, cs]
                                       * _sigmoid(oc)).astype(o_ref.dtype)
        for h in heads:
            mm_last = mm[h][L - 1:L, :]
            kw = (kc[h].astype(F32) * jnp.exp(gmb[:, h:h + 1] - mm_last)).astype(BF16)
            ct_s[h] = inter[h][L - 1:L, :] * ct[h] + _dot(vt[h], kw)
            m_s[h] = jnp.broadcast_to(cum[L - 1:L, h:h + 1] + mm_last, (8, LANES))
        return carry

    lax.fori_loop(0, seq // L, chunk, 0)


def _mlstm(zqk, zvo, zg, cw, cb, gb, mg, batch, seq):
    full = lambda a: pl.BlockSpec(a.shape, lambda b: (0,) * a.ndim)
    seq_blk = lambda n: pl.BlockSpec((None, seq, n), lambda b: (b, 0, 0))
    return pl.pallas_call(
        _mlstm_kernel,
        grid=(batch,),
        in_specs=[seq_blk(2 * ML_PAD), seq_blk(2 * ML_PAD), seq_blk(LANES),
                  full(cw), full(cb), full(gb), full(mg)],
        out_specs=seq_blk(ML_PAD),
        out_shape=jax.ShapeDtypeStruct((batch, seq, ML_PAD), BF16),
        scratch_shapes=[pltpu.VMEM((seq, ML_PAD), BF16), pltpu.VMEM((seq, ML_PAD), BF16),
                        pltpu.VMEM((ML_HEADS, seq // ML_CHUNK, LANES, ML_CHUNK), BF16),
                        pltpu.VMEM((seq, LANES), F32), pltpu.VMEM((seq, LANES), F32),
                        pltpu.VMEM((seq, LANES), F32),
                        pltpu.VMEM((seq // ML_CHUNK, LANES, ML_CHUNK), F32),
                        pltpu.VMEM((ML_HEADS, LANES, LANES), F32),
                        pltpu.VMEM((ML_HEADS, 8, LANES), F32)],
        compiler_params=_cparams(("parallel",)),
        name="mlstm",
    )(zqk.reshape(batch, seq, 2 * ML_PAD), zvo.reshape(batch, seq, 2 * ML_PAD),
      zg.reshape(batch, seq, LANES), cw, cb, gb, mg)


def _dispatch_rows(tm):
    return 2 * tm + N_EXPERTS * (CHUNK_ROWS - 1) + (N_EXPERTS * (CHUNK_ROWS - 1)) % CHUNK_ROWS


def _mix_out_kernel(h_ref, ya_ref, yb_ref, yc_ref, ag_ref, wa_ref, wb_ref, wc_ref, n2_ref,
                    rwh_ref, rwl_ref, rb_ref, h1_ref, xs_ref, meta_ref, cnt_ref):
    tm = h_ref.shape[0]
    rr = xs_ref.shape[0]
    ya = _rms(ya_ref[...].astype(F32), ag_ref[...]).astype(BF16)
    h1 = h_ref[...] + _dot(ya, wa_ref[...]) + _dot(yb_ref[...], wb_ref[...]) + _dot(yc_ref[...], wc_ref[...])
    h1_ref[...] = h1
    xn = _rms(h1, n2_ref[...])
    x_hi, x_lo = _split_bf16(xn)
    by_hi = _dot_nt(jnp.concatenate([rwh_ref[...], rwl_ref[...]], axis=0), x_hi)
    logits = by_hi[:ROUTER_ROWS] + by_hi[ROUTER_ROWS:] + _dot_nt(rwh_ref[...], x_lo) + rb_ref[...]
    e_log = logits[:N_EXPERTS]
    g_log = logits[N_EXPERTS:N_EXPERTS + N_GROUPS]
    g_iota = lax.broadcasted_iota(I32, (N_GROUPS, tm), 0)
    g_max = jnp.max(g_log, axis=0, keepdims=True)
    g_top = jnp.min(jnp.where(g_log == g_max, g_iota, N_GROUPS), axis=0, keepdims=True)
    g_w = 1.0 / jnp.sum(jnp.exp(g_log - g_max), axis=0, keepdims=True)
    e_iota = lax.broadcasted_iota(I32, (N_EXPERTS, tm), 0)
    val = jnp.where(e_iota // EXPERTS_PER_GROUP == g_top, e_log, -jnp.inf)
    m1 = jnp.max(val, axis=0, keepdims=True)
    i1 = jnp.min(jnp.where(val == m1, e_iota, N_EXPERTS), axis=0, keepdims=True)
    val2 = jnp.where(e_iota == i1, -jnp.inf, val)
    m2 = jnp.max(val2, axis=0, keepdims=True)
    i2 = jnp.min(jnp.where(val2 == m2, e_iota, N_EXPERTS), axis=0, keepdims=True)
    e2 = jnp.exp(m2 - m1)
    w0 = g_w / (1.0 + e2)
    w1 = g_w * e2 / (1.0 + e2)
    oh0 = e_iota == i1
    oh1 = e_iota == i2
    oh = jnp.concatenate([jnp.where(oh0, 1.0, 0.0), jnp.where(oh1, 1.0, 0.0)], axis=0).astype(BF16)
    ta = lax.broadcasted_iota(I32, (tm, tm), 0)
    tb = lax.broadcasted_iota(I32, (tm, tm), 1)
    before = jnp.where(ta < tb, 1.0, 0.0).astype(BF16)
    rank = _dot(oh, before)
    c0 = jnp.sum(jnp.where(oh0, 1.0, 0.0), axis=1, keepdims=True)
    c1 = jnp.sum(jnp.where(oh1, 1.0, 0.0), axis=1, keepdims=True)
    nchunk = jnp.floor((c0 + c1 + (CHUNK_ROWS - 1)) * (1.0 / CHUNK_ROWS))
    nchunk_b = jnp.broadcast_to(nchunk, (N_EXPERTS, LANES))
    ea = lax.broadcasted_iota(I32, (N_EXPERTS, N_EXPERTS), 0)
    eb = lax.broadcasted_iota(I32, (N_EXPERTS, N_EXPERTS), 1)
    lower = jnp.where(eb < ea, 1.0, 0.0).astype(BF16)
    loc = CHUNK_ROWS * _dot(lower, nchunk_b.astype(BF16))[:, 0:1]
    dest0 = jnp.sum(jnp.where(oh0, loc + rank[:N_EXPERTS], 0.0), axis=0, keepdims=True)
    dest1 = jnp.sum(jnp.where(oh1, loc + c0 + rank[N_EXPERTS:], 0.0), axis=0, keepdims=True)
    r_iota = lax.broadcasted_iota(I32, (rr, tm), 0)
    d0 = dest0.astype(I32)
    d1 = dest1.astype(I32)
    perm = jnp.where(r_iota == d0, 1.0, jnp.where(r_iota == d1, 1.0, 0.0)).astype(BF16)
    xs_ref[...] = _dot(perm, x_hi).astype(BF16)
    cnt_ref[...] = nchunk_b
    m_iota = lax.broadcasted_iota(I32, (8, tm), 0)
    meta = jnp.where(m_iota == 0, dest0, jnp.where(m_iota == 1, dest1, jnp.where(m_iota == 2, w0,
                     jnp.where(m_iota == 3, w1, 0.0))))
    meta = jnp.concatenate([meta, jnp.zeros((LANES - 8, tm), F32)], axis=0)
    meta_ref[...] = jnp.transpose(meta)


def _mix_out(h, ya, yb, yc, ag, wa, wb, wc, n2, rwh, rwl, rb, tm):
    t = h.shape[0]
    nt = t // tm
    rr = _dispatch_rows(tm)
    row = lambda n: pl.BlockSpec((tm, n), lambda i: (i, 0))
    full = lambda a: pl.BlockSpec(a.shape, lambda i: (0,) * a.ndim)
    return pl.pallas_call(
        _mix_out_kernel,
        grid=(nt,),
        in_specs=[row(D_MODEL), row(SB_WIDTH), row(ML_PAD), row(SGU_WIDTH), full(ag), full(wa), full(wb),
                  full(wc), full(n2), full(rwh), full(rwl), full(rb)],
        out_specs=[row(D_MODEL), pl.BlockSpec((rr, D_MODEL), lambda i: (i, 0)), row(LANES),
                   pl.BlockSpec((None, N_EXPERTS, LANES), lambda i: (i, 0, 0))],
        out_shape=[jax.ShapeDtypeStruct((t, D_MODEL), F32),
                   jax.ShapeDtypeStruct((nt * rr, D_MODEL), BF16),
                   jax.ShapeDtypeStruct((t, LANES), F32),
                   jax.ShapeDtypeStruct((nt, N_EXPERTS, LANES), F32)],
        compiler_params=_cparams(("parallel",)),
        name="mix_out",
    )(h, ya, yb, yc, ag, wa, wb, wc, n2, rwh, rwl, rb)


def _chunk_tables(cnt, rr, n_blocks_max):
    nt = cnt.shape[0]
    ne = N_EXPERTS + 1
    tail = rr // CHUNK_ROWS - jnp.sum(cnt, axis=1)
    cnt = jnp.concatenate([cnt, tail[:, None]], axis=1)
    loc = (jnp.cumsum(cnt, axis=1) - cnt) * CHUNK_ROWS
    cnt_e = cnt.T
    cum_e = jnp.cumsum(cnt_e, axis=1)
    total = cum_e[:, -1]
    nblk = (total + EXPERT_BLOCK_CHUNKS - 1) // EXPERT_BLOCK_CHUNKS
    bend = jnp.cumsum(nblk)
    bstart = bend - nblk
    n_blocks = bend[-1]
    i = jnp.arange(n_blocks_max, dtype=I32)
    be = jnp.minimum(jnp.sum(bend[None, :] <= i[:, None], axis=1), ne - 1).astype(I32)
    oh_b = be[:, None] == jnp.arange(ne, dtype=I32)[None, :]
    pick = lambda v: jnp.sum(jnp.where(oh_b, v[None, :], 0), axis=1)
    pick2 = lambda m: jnp.sum(jnp.where(oh_b[:, :, None], m[None, :, :], 0), axis=1)
    q = (i - pick(bstart))[:, None] * EXPERT_BLOCK_CHUNKS + jnp.arange(EXPERT_BLOCK_CHUNKS, dtype=I32)[None, :]
    valid = (q < pick(total)[:, None]) & (i < n_blocks)[:, None]
    cum_b = pick2(cum_e)
    base = jnp.arange(nt, dtype=I32)[None, :] * rr + loc.T - (cum_e - cnt_e) * CHUNK_ROWS
    base_b = pick2(base)
    j = jnp.minimum(jnp.sum(cum_b[:, None, :] <= q[:, :, None], axis=-1), nt - 1)
    oh_j = j[:, :, None] == jnp.arange(nt, dtype=I32)[None, None, :]
    row = jnp.sum(jnp.where(oh_j, base_b[:, None, :], 0), axis=-1) + q * CHUNK_ROWS
    spare = nt * rr + ((i % 2)[:, None] * EXPERT_BLOCK_CHUNKS
                       + jnp.arange(EXPERT_BLOCK_CHUNKS, dtype=I32)[None, :]) * CHUNK_ROWS
    src = jnp.where(valid, row, 0).astype(I32)
    dst = jnp.where(valid, row, spare).astype(I32)
    return be, src.reshape(-1), dst.reshape(-1), n_blocks.reshape(1).astype(I32)


def _experts_kernel(be_ref, src_ref, dst_ref, nb_ref, xs_ref, wg_ref, wu_ref, wd_ref, ys_ref,
                    xbuf, ybuf, wgb, wub, wdb, sem_in, sem_out):
    i = pl.program_id(0)
    nb = nb_ref[0]
    slot = lax.rem(i, 2)

    def is_compute(b):
        return be_ref[b] < N_EXPERTS

    def start_loads(b, sl):
        @pl.when(is_compute(b))
        def _():
            for s in range(EXPERT_BLOCK_CHUNKS):
                src = pl.multiple_of(src_ref[b * EXPERT_BLOCK_CHUNKS + s], CHUNK_ROWS)
                pltpu.make_async_copy(xs_ref.at[pl.ds(src, CHUNK_ROWS), :],
                                      xbuf.at[sl, s * CHUNK_ROWS:(s + 1) * CHUNK_ROWS, :], sem_in.at[sl]).start()

    def wait_loads(b, sl):
        @pl.when(is_compute(b))
        def _():
            pltpu.make_async_copy(xs_ref.at[0:EXPERT_BLOCK, :], xbuf.at[sl], sem_in.at[sl]).wait()

    def start_stores(b, sl):
        for s in range(EXPERT_BLOCK_CHUNKS):
            dst = pl.multiple_of(dst_ref[b * EXPERT_BLOCK_CHUNKS + s], CHUNK_ROWS)
            pltpu.make_async_copy(ybuf.at[sl, s * CHUNK_ROWS:(s + 1) * CHUNK_ROWS, :],
                                  ys_ref.at[pl.ds(dst, CHUNK_ROWS), :], sem_out.at[sl]).start()

    def wait_stores(sl):
        pltpu.make_async_copy(ybuf.at[sl], ys_ref.at[0:EXPERT_BLOCK, :], sem_out.at[sl]).wait()

    @pl.when(i == 0)
    def _():
        ybuf[...] = jnp.zeros(ybuf.shape, BF16)
        real_rows = ys_ref.shape[0] - 2 * EXPERT_BLOCK
        for par in range(2):
            fill = pltpu.make_async_copy(
                ybuf.at[par], ys_ref.at[real_rows + par * EXPERT_BLOCK:real_rows + (par + 1) * EXPERT_BLOCK, :],
                sem_out.at[par])
            fill.start()
            fill.wait()

        start_loads(0, 0)

    @pl.when(i < nb)
    def _():
        @pl.when(i + 1 < nb)
        def _():
            start_loads(i + 1, 1 - slot)

        wait_loads(i, slot)

        @pl.when(i >= 2)
        def _():
            wait_stores(slot)

        @pl.when(is_compute(i))
        def _():
            @pl.when((i == 0) | (be_ref[i] != be_ref[jnp.maximum(i - 1, 0)]))
            def _():
                wgb[...] = wg_ref[...].astype(BF16)
                wub[...] = wu_ref[...].astype(BF16)
                wdb[...] = wd_ref[...].astype(BF16)

            x = xbuf[slot]
            gate = _dot(x, wgb[...])
            up = _dot(x, wub[...])
            mid = (gate * _sigmoid(gate) * up).astype(BF16)
            ybuf[slot] = _dot(mid, wdb[...]).astype(BF16)

        @pl.when(jnp.logical_not(is_compute(i)))
        def _():
            ybuf[slot] = jnp.zeros((EXPERT_BLOCK, D_MODEL), BF16)

        start_stores(i, slot)

        @pl.when(i == nb - 1)
        def _():
            @pl.when(i >= 1)
            def _():
                wait_stores(1 - slot)

            wait_stores(slot)


def _experts(be, src, dst, nb, xs, wg, wu, wd, layer):
    n_blocks_max = be.shape[0]
    wspec = lambda a: pl.BlockSpec(
        (None,) + a.shape[1:],
        lambda i, be, src, dst, nb: (layer * N_EXPERTS + jnp.minimum(be[i], N_EXPERTS - 1), 0, 0))
    grid_spec = pltpu.PrefetchScalarGridSpec(
        num_scalar_prefetch=4,
        grid=(n_blocks_max,),
        in_specs=[pl.BlockSpec(memory_space=pl.ANY), wspec(wg), wspec(wu), wspec(wd)],
        out_specs=pl.BlockSpec(memory_space=pl.ANY),
        scratch_shapes=[pltpu.VMEM((2, EXPERT_BLOCK, D_MODEL), BF16), pltpu.VMEM((2, EXPERT_BLOCK, D_MODEL), BF16),
                        pltpu.VMEM(wg.shape[1:], BF16), pltpu.VMEM(wu.shape[1:], BF16),
                        pltpu.VMEM(wd.shape[1:], BF16),
                        pltpu.SemaphoreType.DMA((2,)), pltpu.SemaphoreType.DMA((2,))],
    )
    return pl.pallas_call(
        _experts_kernel,
        grid_spec=grid_spec,
        out_shape=jax.ShapeDtypeStruct((xs.shape[0] + 2 * EXPERT_BLOCK, D_MODEL), BF16),
        compiler_params=_cparams(("arbitrary",)),
        name="experts",
    )(be, src, dst, nb, xs, wg, wu, wd)


def _combine_kernel(h_ref, ys_ref, meta_ref, p_ref, pg_ref, gw_ref, pw_ref, fg_ref, o_ref, *, final):
    tm = h_ref.shape[0]
    rr = ys_ref.shape[0]
    meta = meta_ref[...]
    d0 = meta[:, 0:1].astype(I32)
    d1 = meta[:, 1:2].astype(I32)
    r_iota = lax.broadcasted_iota(I32, (tm, rr), 1)
    unperm = (jnp.where(r_iota == d0, meta[:, 2:3], 0.0) + jnp.where(r_iota == d1, meta[:, 3:4], 0.0)).astype(BF16)
    h2 = h_ref[...] + _dot(unperm, ys_ref[...])
    gate = _sigmoid(_dot(_rms(h2, pg_ref[...]).astype(BF16), gw_ref[...]))
    h3 = h2 + gate * _dot(p_ref[...].astype(BF16), pw_ref[...])
    if final:
        h3 = _rms(h3, fg_ref[...])
    o_ref[...] = h3


def _combine(h1, ys, meta, p, layer, pg, gw, pw, fg, tm, final):
    t = h1.shape[0]
    nt = t // tm
    rr = _dispatch_rows(tm)
    row = lambda n: pl.BlockSpec((tm, n), lambda i: (i, 0))
    full = lambda a: pl.BlockSpec(a.shape, lambda i: (0,) * a.ndim)
    return pl.pallas_call(
        functools.partial(_combine_kernel, final=final),
        grid=(nt,),
        in_specs=[row(D_MODEL), pl.BlockSpec((rr, D_MODEL), lambda i: (i, 0)), row(LANES),
                  pl.BlockSpec((tm, P_DIM), lambda i: (layer * nt + i, 0)),
                  full(pg), full(gw), full(pw), full(fg)],
        out_specs=row(D_MODEL),
        out_shape=jax.ShapeDtypeStruct((t, D_MODEL), F32),
        compiler_params=_cparams(("parallel",)),
        name="combine",
    )(h1, ys, meta, p, pg, gw, pw, fg)


def _pad_heads(a, axis):
    shape = a.shape
    a = a.reshape(shape[:axis] + (ML_HEADS, ML_HEAD_DIM) + shape[axis + 1:])
    pad = [(0, 0)] * a.ndim
    pad[axis + 1] = (0, LANES - ML_HEAD_DIM)
    a = jnp.pad(a, pad)
    return a.reshape(shape[:axis] + (ML_PAD,) + shape[axis + 1:])


def _layer_params(i, w_in, conv_w, conv_b, igate_b, fgate_b, mnorm_g, sgu_b, w_out, router_gw, router_gb,
                  router_ew, router_eb):
    w = w_in[i]
    s, m = SB_WIDTH, ML_WIDTH
    a_q, a_k, a_v = w[:, 0:s] * (SB_HEAD_DIM ** -0.5 * LOG2_E), w[:, s:2 * s], w[:, 2 * s:3 * s]
    o = 3 * s
    b_q, b_k, b_v, b_o = (w[:, o + k * m:o + (k + 1) * m] for k in range(4))
    o = o + 4 * m
    gates = w[:, o:o + 2 * ML_HEADS]
    c_uv = w[:, o + 2 * ML_HEADS:]
    w_r = jnp.concatenate([a_q, a_k, a_v, _pad_heads(b_q, 1), _pad_heads(b_k, 1), _pad_heads(b_v, 1),
                           _pad_heads(b_o, 1), c_uv,
                           jnp.pad(gates, ((0, 0), (0, LANES - 2 * ML_HEADS)))], axis=1).astype(BF16)
    cw = jnp.concatenate([_pad_heads(conv_w[i][:, :m], 1), _pad_heads(conv_w[i][:, m:], 1)], axis=1)
    cb = jnp.concatenate([_pad_heads(conv_b[i][:m], 0), _pad_heads(conv_b[i][m:], 0)])[None, :]
    gb = jnp.pad(jnp.concatenate([igate_b[i], fgate_b[i]]), (0, LANES - 2 * ML_HEADS))[None, :]
    mg = _pad_heads(mnorm_g[i], 0)[None, :]
    sgu_bias = jnp.repeat(sgu_b[i].T, SGU_GROUP_DIM, axis=1)
    wo = w_out[i]
    wa = wo[:s].astype(BF16)
    wb = _pad_heads(wo[s:s + m], 0).astype(BF16)
    wc = wo[s + m:].astype(BF16)
    rw = jnp.concatenate([router_ew[i].T, router_gw[i].T,
                          jnp.zeros((ROUTER_ROWS - N_EXPERTS - N_GROUPS, D_MODEL), F32)], axis=0)
    rwh = rw.astype(BF16)
    rwl = (rw - rwh.astype(F32)).astype(BF16)
    rb = jnp.concatenate([router_eb[i], router_gb[i],
                          jnp.zeros((ROUTER_ROWS - N_EXPERTS - N_GROUPS,), F32)])[:, None]
    return w_r, cw, cb, gb, mg, sgu_bias, wa, wb, wc, rwh, rwl, rb


def kernel(x, p, norm1_g, w_in, conv_w, conv_b, igate_b, fgate_b, mnorm_g, sb_out_g, sgu_ln_g, sgu_ln_b, sgu_w,
           sgu_b, sgu_out_g, w_out, norm2_g, router_gw, router_gb, router_ew, router_eb, w_gate, w_up, w_down,
           ple_norm_g, ple_gate_w, ple_proj_w, final_g, *, tile=512):
    batch, seq, d = x.shape
    depth = w_in.shape[0]
    t = batch * seq
    tm = min(tile, t)
    nt = t // tm
    rr = _dispatch_rows(tm)
    n_blocks_max = (nt * rr // CHUNK_ROWS) // EXPERT_BLOCK_CHUNKS + N_EXPERTS + 1
    h = x.astype(F32).reshape(t, d)
    p2 = p.reshape(depth * t, P_DIM)
    wg_all = w_gate.reshape((depth * N_EXPERTS,) + w_gate.shape[2:])
    wu_all = w_up.reshape((depth * N_EXPERTS,) + w_up.shape[2:])
    wd_all = w_down.reshape((depth * N_EXPERTS,) + w_down.shape[2:])
    for i in range(depth):
        (w_r, cw, cb, gb, mg, sgu_bias, wa, wb, wc, rwh, rwl, rb) = _layer_params(
            i, w_in, conv_w, conv_b, igate_b, fgate_b, mnorm_g, sgu_b, w_out, router_gw, router_gb,
            router_ew, router_eb)
        za, zqk, zvo, yc, zg = _in_proj(h, norm1_g[i][None, :], w_r, sgu_ln_g[i][None, :], sgu_ln_b[i][None, :],
                                        sgu_w[i], sgu_bias, sgu_out_g[i][None, :], min(2 * tm, seq))
        ya = _sb_attn(za, batch, seq).reshape(t, SB_WIDTH)
        yb = _mlstm(zqk, zvo, zg, cw, cb, gb, mg, batch, seq).reshape(t, ML_PAD)
        h1, xs, meta, cnt = _mix_out(h, ya, yb, yc, sb_out_g[i][None, :], wa, wb, wc, norm2_g[i][None, :],
                                     rwh, rwl, rb, tm)
        be, src, dst, nb = _chunk_tables(cnt[:, :, 0].astype(I32), rr, n_blocks_max)
        ys = _experts(be, src, dst, nb, xs, wg_all, wu_all, wd_all, i)
        h = _combine(h1, ys, meta, p2, i, ple_norm_g[i][None, :], ple_gate_w[i].astype(BF16),
                     ple_proj_w[i].astype(BF16), final_g[None, :], tm, i == depth - 1)
    return h.reshape(batch, seq, d).astype(x.dtype)
```

```python
import functools

import jax
import jax.numpy as jnp
import numpy as np
from jax import lax
from jax.experimental import pallas as pl
from jax.experimental.pallas import tpu as pltpu

F32 = jnp.float32
BF16 = jnp.bfloat16
I32 = jnp.int32

D_MODEL = 1024
P_DIM = 256
EPS = 1e-6
LANES = 128
SB_HEAD_DIM = 64
SB_WIDTH = 384
SB_BLOCK = 128
SB_QTILE = 512
SB_GROUP = 1
LOG2_E = 1.4426950408889634
ML_HEADS = 4
ML_HEAD_DIM = 96
ML_WIDTH = 384
ML_PAD = ML_HEADS * LANES
ML_CHUNK = 128
CONV_K = 4
SGU_WIDTH = 256
SGU_GROUPS = 4
SGU_GROUP_DIM = 64
SGU_BLOCK = 128
STREAM_CHUNK = 64
N_GROUPS = 4
EXPERTS_PER_GROUP = 8
N_EXPERTS = 32
D_EXPERT = 256
ROUTER_ROWS = 40
CHUNK_ROWS = 16
EXPERT_BLOCK_CHUNKS = 32
EXPERT_BLOCK = CHUNK_ROWS * EXPERT_BLOCK_CHUNKS

COL_A = 0
COL_BQK = 3 * SB_WIDTH
COL_BVO = COL_BQK + 2 * ML_PAD
COL_C = COL_BVO + 2 * ML_PAD
COL_G = COL_C + 2 * SGU_WIDTH
N_Z = COL_G + LANES

VMEM_LIMIT = 56 * 1024 * 1024


def _cparams(sem, vmem=VMEM_LIMIT):
    return pltpu.CompilerParams(dimension_semantics=sem, vmem_limit_bytes=vmem)


def _rms(x, g):
    return x * lax.rsqrt(jnp.mean(x * x, axis=-1, keepdims=True) + EPS) * g


def _sigmoid(x):
    return 0.5 * jnp.tanh(0.5 * x) + 0.5


def _split_bf16(x):
    hi = x.astype(BF16)
    lo = (x - hi.astype(F32)).astype(BF16)
    return hi, lo


def _dot(a, b):
    return jnp.dot(a, b, preferred_element_type=F32)


def _dot_nt(a, b):
    return lax.dot_general(a, b, (((1,), (1,)), ((), ())), preferred_element_type=F32)


def _dot_tn(a, b):
    return lax.dot_general(a, b, (((0,), (0,)), ((), ())), preferred_element_type=F32)


def _dot_split_lhs(x, m):
    hi, lo = _split_bf16(x)
    return _dot(hi, m) + _dot(lo, m)


def _dot_split_rhs(m, x):
    hi, lo = _split_bf16(x)
    return _dot(m, hi) + _dot(m, lo)


def _spatial_gating(zc, lg, lb, w_ref, bias, og):
    W = SGU_WIDTH
    gi = lax.broadcasted_iota(I32, (W, W), 0) // SGU_GROUP_DIM
    gj = lax.broadcasted_iota(I32, (W, W), 1) // SGU_GROUP_DIM
    avg = jnp.where(gi == gj, 1.0 / SGU_GROUP_DIM, 0.0).astype(BF16)
    ti = lax.broadcasted_iota(I32, (SGU_BLOCK, SGU_BLOCK), 0) // STREAM_CHUNK
    si = lax.broadcasted_iota(I32, (SGU_BLOCK, SGU_BLOCK), 1) // STREAM_CHUNK
    chunk_causal = si <= ti
    lane_group = lax.broadcasted_iota(I32, (SGU_BLOCK, W), 1) // SGU_GROUP_DIM
    wg = [jnp.where(chunk_causal, w_ref[g], 0.0).astype(BF16) for g in range(SGU_GROUPS)]
    wpair = [jnp.concatenate(wg[g:g + 2], axis=1) for g in range(0, SGU_GROUPS, 2)]
    out = []
    for r in range(zc.shape[0] // SGU_BLOCK):
        rs = slice(r * SGU_BLOCK, (r + 1) * SGU_BLOCK)
        u = jax.nn.gelu(zc[rs, :W])
        v = jax.nn.gelu(zc[rs, W:])
        mu = _dot_split_lhs(v, avg)
        vc = v - mu
        var = _dot_split_lhs(vc * vc, avg)
        vn = (vc * lax.rsqrt(var + EPS) * lg + lb).astype(BF16)
        mixed = bias
        zero = jnp.zeros_like(vn)
        for p, g in enumerate(range(0, SGU_GROUPS, 2)):
            vpair = jnp.concatenate([jnp.where(lane_group == g, vn, zero),
                                     jnp.where(lane_group == g + 1, vn, zero)], axis=0)
            mixed = mixed + _dot(wpair[p], vpair)
        out.append(_rms(u * mixed, og))
    return jnp.concatenate(out, axis=0)


def _in_proj_kernel(x_ref, g_ref, w_ref, lg_ref, lb_ref, sw_ref, sb_ref, og_ref,
                    za_ref, zqk_ref, zvo_ref, yc_ref, zg_ref):
    hn = _rms(x_ref[...], g_ref[...]).astype(BF16)
    za_ref[...] = _dot(hn, w_ref[:, COL_A:COL_BQK]).astype(BF16)
    zqk_ref[...] = _dot(hn, w_ref[:, COL_BQK:COL_BVO])
    zvo_ref[...] = _dot(hn, w_ref[:, COL_BVO:COL_C]).astype(BF16)
    zg_ref[...] = _dot(hn, w_ref[:, COL_G:N_Z])
    zc = _dot(hn, w_ref[:, COL_C:COL_G])
    yc_ref[...] = _spatial_gating(zc, lg_ref[...], lb_ref[...], sw_ref, sb_ref[...], og_ref[...]).astype(BF16)


def _in_proj(h, g, w, lg, lb, sw, sbias, og, tm):
    t = h.shape[0]
    row = lambda n: pl.BlockSpec((tm, n), lambda i: (i, 0))
    full = lambda a: pl.BlockSpec(a.shape, lambda i: (0,) * a.ndim)
    resident = pl.BlockSpec(w.shape, lambda i: (0, 0), pipeline_mode=pl.Buffered(1))
    return pl.pallas_call(
        _in_proj_kernel,
        grid=(t // tm,),
        in_specs=[row(D_MODEL), full(g), resident, full(lg), full(lb), full(sw), full(sbias), full(og)],
        out_specs=[row(3 * SB_WIDTH), row(2 * ML_PAD), row(2 * ML_PAD), row(SGU_WIDTH), row(LANES)],
        out_shape=[jax.ShapeDtypeStruct((t, 3 * SB_WIDTH), BF16),
                   jax.ShapeDtypeStruct((t, 2 * ML_PAD), F32),
                   jax.ShapeDtypeStruct((t, 2 * ML_PAD), BF16),
                   jax.ShapeDtypeStruct((t, SGU_WIDTH), BF16),
                   jax.ShapeDtypeStruct((t, LANES), F32)],
        compiler_params=_cparams(("parallel",)),
        name="in_proj",
    )(h, g, w, lg, lb, sw, sbias, og)


def _sb_kernel(q_ref, k_ref, v_ref, o_ref, acc_ref, car_ref, pre_ref, tot_ref, kh_ref, vh_ref):
    n = pl.program_id(2)
    blk = SB_BLOCK
    group, tq = q_ref.shape[0], q_ref.shape[1]
    sub = tq // blk
    lane_k = lax.broadcasted_iota(I32, (blk, LANES), 1)
    head0 = lane_k < SB_HEAD_DIM
    mj = lax.broadcasted_iota(I32, (2 * blk, 2 * blk), 0) % blk
    ms = lax.broadcasted_iota(I32, (2 * blk, 2 * blk), 1)
    suffix = jnp.where((mj > ms) | (ms >= blk), 1.0, 0.0).astype(BF16)

    def per_head(x):
        zero = jnp.zeros_like(x)
        return jnp.concatenate([jnp.where(head0, x, zero), jnp.where(head0, zero, x)], axis=0)

    @pl.when(n == 0)
    def _():
        for g in range(group):
            for j in range(k_ref.shape[1] // blk):
                kh_ref[g, j] = per_head(k_ref[g, j * blk:(j + 1) * blk, :])
                vh_ref[g, j] = per_head(v_ref[g, j * blk:(j + 1) * blk, :])

    def block_at(ref, g, j):
        return ref[g, j]

    def stage1(g, j, r0, masked, buf):
        z = _dot_nt(q_ref[g, r0:, :], block_at(kh_ref, g, j))
        sign = jnp.uint32(0x80000000)
        neg_abs = lax.bitcast_convert_type(lax.bitcast_convert_type(z, jnp.uint32) | sign, F32)
        e = jnp.exp2(neg_abs)
        log_sig = jnp.minimum(z, 0.0) - jnp.log2(1.0 + e)
        log1m = log_sig - z
        if masked:
            mask = (lax.broadcasted_iota(I32, (blk, 2 * blk), 1) % blk
                    < lax.broadcasted_iota(I32, (blk, 2 * blk), 0))
            top_l = jnp.where(mask, log1m[:blk], 0.0)
            top_s = jnp.where(mask, log_sig[:blk], -jnp.inf)
            if tq - r0 == blk:
                log1m, log_sig = top_l, top_s
            else:
                log1m = jnp.concatenate([top_l, log1m[blk:]], axis=0)
                log_sig = jnp.concatenate([top_s, log_sig[blk:]], axis=0)
        hi, lo = _split_bf16(log1m)
        for hh in range(2):
            cs = slice(hh * blk, (hh + 1) * blk)
            ca = _dot(jnp.concatenate([hi[:, cs], lo[:, cs]], axis=1), suffix)
            pre_ref[g, buf, r0:, cs] = log_sig[:, cs] + ca[:, :blk]
            tot_ref[g, buf, r0:, cs] = ca[:, blk:]

    def stage2(g, j, r0, buf):
        arg = pre_ref[g, buf, r0:, :] + car_ref[g, r0:, :]
        acc_ref[g, r0:, :] += _dot(jnp.exp2(arg).astype(BF16), block_at(vh_ref, g, j))
        car_ref[g, r0:, :] += tot_ref[g, buf, r0:, :]

    acc_ref[...] = jnp.zeros_like(acc_ref)
    car_ref[...] = jnp.zeros_like(car_ref)

    for kj in reversed(range(sub)):
        for g in range(group):
            stage1(g, n * sub + kj, kj * blk, True, kj)
    for kj in reversed(range(sub)):
        for g in range(group):
            stage2(g, n * sub + kj, kj * blk, kj)

    top = n * sub - 1

    @pl.when(n > 0)
    def _():
        for g in range(group):
            stage1(g, top, 0, False, 0)

    unroll = 4

    def body(i, c):
        j0 = top - unroll * i
        for u in range(unroll):
            nxt = j0 - u - 1
            if u == unroll - 1:
                nxt = jnp.maximum(nxt, 0)
            for g in range(group):
                stage1(g, nxt, 0, False, (u + 1) % 2)
            for g in range(group):
                stage2(g, j0 - u, 0, u % 2)
        return c

    lax.fori_loop(0, n * (sub // unroll), body, 0)
    o_ref[...] = acc_ref[...].astype(o_ref.dtype)


def _sb_attn(za, batch, seq):
    za3 = za.reshape(batch, seq, 3 * SB_WIDTH)
    pairs = SB_WIDTH // LANES
    tq = min(SB_QTILE, seq)
    group = SB_GROUP if batch % SB_GROUP == 0 else 1
    return pl.pallas_call(
        _sb_kernel,
        grid=(batch // group, pairs, seq // tq),
        in_specs=[pl.BlockSpec((group, tq, LANES), lambda b, p, n: (b, n, p)),
                  pl.BlockSpec((group, seq, LANES), lambda b, p, n: (b, 0, pairs + p)),
                  pl.BlockSpec((group, seq, LANES), lambda b, p, n: (b, 0, 2 * pairs + p))],
        out_specs=pl.BlockSpec((group, tq, LANES), lambda b, p, n: (b, n, p)),
        out_shape=jax.ShapeDtypeStruct((batch, seq, SB_WIDTH), BF16),
        scratch_shapes=[pltpu.VMEM((group, tq, LANES), F32),
                        pltpu.VMEM((group, tq, 2 * LANES), F32),
                        pltpu.VMEM((group, tq // SB_BLOCK, tq, 2 * LANES), F32),
                        pltpu.VMEM((group, tq // SB_BLOCK, tq, 2 * LANES), F32),
                        pltpu.VMEM((group, seq // SB_BLOCK, 2 * SB_BLOCK, LANES), BF16),
                        pltpu.VMEM((group, seq // SB_BLOCK, 2 * SB_BLOCK, LANES), BF16)],
        compiler_params=_cparams(("parallel", "parallel", "arbitrary")),
        name="sb_attn",
    )(za3, za3, za3)


def _mlstm_kernel(qk_ref, vo_ref, g_ref, cw_ref, cb_ref, gb_ref, mg_ref, o_ref,
                  q_s, k_s, vt_s, cum_s, gmb_s, pm_s, gmbt_s, ct_s, m_s):
    seq = qk_ref.shape[0]
    L = ML_CHUNK
    cw = cw_ref[...]
    cb = cb_ref[...]
    rows = lax.broadcasted_iota(I32, (L, 2 * ML_PAD), 0)

    def conv_silu(r0, taps):
        y = taps[0] * cw[CONV_K - 1:CONV_K, :] + cb
        for d in range(1, CONV_K):
            y = y + taps[d] * cw[CONV_K - 1 - d:CONV_K - d, :]
        y = y * _sigmoid(y)
        q_s[pl.ds(r0, L), :] = y[:, :ML_PAD].astype(BF16)
        k_s[pl.ds(r0, L), :] = (y[:, ML_PAD:] * (ML_HEAD_DIM ** -0.5)).astype(BF16)

    ct_s[...] = jnp.zeros_like(ct_s)
    m_s[...] = jnp.zeros_like(m_s)

    ti = lax.broadcasted_iota(I32, (L, L), 0)
    si = lax.broadcasted_iota(I32, (L, L), 1)
    causal = si <= ti
    tril = jnp.where(causal, 1.0, 0.0).astype(BF16)
    lane = lax.broadcasted_iota(I32, (L, LANES), 1)
    row = lax.broadcasted_iota(I32, (L, LANES), 0)
    ones2 = jnp.full((2 * LANES, LANES), 1.0, BF16)
    gb = gb_ref[...]
    mg = mg_ref[...]
    norm_lane = LANES - 1

    def gates(c):
        r0 = pl.multiple_of(c * L, L)
        g = g_ref[pl.ds(r0, L), :] + gb
        logf = jnp.minimum(g, 0.0) - jnp.log1p(jnp.exp(-jnp.abs(g)))
        cum = pltpu.roll(_dot_split_rhs(tril, logf), LANES - ML_HEADS, axis=1)
        gmb = g - cum
        pm = gmb
        step = 1
        while step < L:
            pm = jnp.maximum(pm, jnp.where(row >= step, pltpu.roll(pm, step, axis=0), -jnp.inf))
            step *= 2
        cum_s[pl.ds(r0, L), :] = cum
        gmb_s[pl.ds(r0, L), :] = gmb
        pm_s[pl.ds(r0, L), :] = pm
        gmbt_s[c] = jnp.transpose(gmb)
        for h in range(ML_HEADS):
            v = vo_ref[pl.ds(r0, L), h * LANES:(h + 1) * LANES].astype(F32)
            vt_s[h, c] = jnp.transpose(jnp.where(lane == norm_lane, 1.0, v)).astype(BF16)

    x0 = qk_ref[0:L, :]
    conv_silu(0, [x0] + [jnp.where(rows >= d, pltpu.roll(x0, d, axis=0), 0.0) for d in range(1, CONV_K)])

    def conv_chunk(c, carry):
        r0 = pl.multiple_of(c * L, L)
        xe = qk_ref[pl.ds(r0 - 8, L + 8), :]
        conv_silu(r0, [xe[8 - d:8 - d + L, :] for d in range(CONV_K)])
        return carry

    lax.fori_loop(1, seq // L, conv_chunk, 0)

    gate_unroll = 4 if (seq // L) % 4 == 0 else 1

    def gate_chunk(c, carry):
        for u in range(gate_unroll):
            gates(gate_unroll * c + u)
        return carry

    lax.fori_loop(0, seq // (gate_unroll * L), gate_chunk, 0)

    heads = range(ML_HEADS)

    def chunk(c, carry):
        r0 = pl.multiple_of(c * L, L)
        cum = cum_s[pl.ds(r0, L), :]
        gmb = gmb_s[pl.ds(r0, L), :]
        pm = pm_s[pl.ds(r0, L), :]
        gmb_t = gmbt_s[c]
        qc = [q_s[pl.ds(r0, L), h * LANES:(h + 1) * LANES] for h in heads]
        kc = [k_s[pl.ds(r0, L), h * LANES:(h + 1) * LANES] for h in heads]
        vt = [vt_s[h, c] for h in heads]
        m_in = [m_s[h][0:1, 0:1] for h in heads]
        ct = [ct_s[h] for h in heads]
        s_raw = [_dot_nt(qc[h], kc[h]) for h in heads]
        mm = [jnp.maximum(m_in[h], pm[:, h:h + 1]) for h in heads]
        inter = [jnp.exp(m_in[h] - mm[h]) for h in heads]
        num = []
        for h in heads:
            wd = jnp.where(causal, jnp.exp(gmb_t[h:h + 1, :] - mm[h]), 0.0)
            lhs = jnp.concatenate([(s_raw[h] * wd).astype(BF16),
                                   (qc[h].astype(F32) * inter[h]).astype(BF16)], axis=1)
            rhs_t = jnp.concatenate([vt[h], ct[h].astype(BF16)], axis=1)
            num.append(_dot_nt(lhs, rhs_t))
        hh = []
        ms = []
        for h in heads:
            den = jnp.maximum(jnp.abs(num[h][:, norm_lane:]), jnp.exp(-(cum[:, h:h + 1] + mm[h])))
            x = jnp.where(lane == norm_lane, 0.0, num[h] * (1.0 / den))
            hi, lo = _split_bf16(x * x)
            hh.append(x)
            ms.append(_dot(jnp.concatenate([hi, lo], axis=1), ones2) * (1.0 / ML_HEAD_DIM))
        for h in heads:
            cs = slice(h * LANES, (h + 1) * LANES)
            oc = vo_ref[pl.ds(r0, L), ML_PAD + h * LANES:ML_PAD + (h + 1) * LANES].astype(F32)
            o_ref[pl.ds(r0, L), cs] = (hh[h] * lax.rsqrt(ms[h] + EPS) * mg[:, cs]
                                       * _sigmoid(oc)).astype(o_ref.dtype)
        for h in heads:
            mm_last = mm[h][L - 1:L, :]
            kw = (kc[h].astype(F32) * jnp.exp(gmb[:, h:h + 1] - mm_last)).astype(BF16)
            ct_s[h] = inter[h][L - 1:L, :] * ct[h] + _dot(vt[h], kw)
            m_s[h] = jnp.broadcast_to(cum[L - 1:L, h:h + 1] + mm_last, (8, LANES))
        return carry

    lax.fori_loop(0, seq // L, chunk, 0)


def _mlstm(zqk, zvo, zg, cw, cb, gb, mg, batch, seq):
    full = lambda a: pl.BlockSpec(a.shape, lambda b: (0,) * a.ndim)
    seq_blk = lambda n: pl.BlockSpec((None, seq, n), lambda b: (b, 0, 0))
    return pl.pallas_call(
        _mlstm_kernel,
        grid=(batch,),
        in_specs=[seq_blk(2 * ML_PAD), seq_blk(2 * ML_PAD), seq_blk(LANES),
                  full(cw), full(cb), full(gb), full(mg)],
        out_specs=seq_blk(ML_PAD),
        out_shape=jax.ShapeDtypeStruct((batch, seq, ML_PAD), BF16),
        scratch_shapes=[pltpu.VMEM((seq, ML_PAD), BF16), pltpu.VMEM((seq, ML_PAD), BF16),
                        pltpu.VMEM((ML_HEADS, seq // ML_CHUNK, LANES, ML_CHUNK), BF16),
                        pltpu.VMEM((seq, LANES), F32), pltpu.VMEM((seq, LANES), F32),
                        pltpu.VMEM((seq, LANES), F32),
                        pltpu.VMEM((seq // ML_CHUNK, LANES, ML_CHUNK), F32),
                        pltpu.VMEM((ML_HEADS, LANES, LANES), F32),
                        pltpu.VMEM((ML_HEADS, 8, LANES), F32)],
        compiler_params=_cparams(("parallel",)),
        name="mlstm",
    )(zqk.reshape(batch, seq, 2 * ML_PAD), zvo.reshape(batch, seq, 2 * ML_PAD),
      zg.reshape(batch, seq, LANES), cw, cb, gb, mg)


def _dispatch_rows(tm):
    return 2 * tm + N_EXPERTS * (CHUNK_ROWS - 1) + (N_EXPERTS * (CHUNK_ROWS - 1)) % CHUNK_ROWS


def _mix_out_kernel(h_ref, ya_ref, yb_ref, yc_ref, ag_ref, wa_ref, wb_ref, wc_ref, n2_ref,
                    rwh_ref, rwl_ref, rb_ref, h1_ref, xs_ref, meta_ref, cnt_ref):
    tm = h_ref.shape[0]
    rr = xs_ref.shape[0]
    ya = _rms(ya_ref[...].astype(F32), ag_ref[...]).astype(BF16)
    h1 = h_ref[...] + _dot(ya, wa_ref[...]) + _dot(yb_ref[...], wb_ref[...]) + _dot(yc_ref[...], wc_ref[...])
    h1_ref[...] = h1
    xn = _rms(h1, n2_ref[...])
    x_hi, x_lo = _split_bf16(xn)
    by_hi = _dot_nt(jnp.concatenate([rwh_ref[...], rwl_ref[...]], axis=0), x_hi)
    logits = by_hi[:ROUTER_ROWS] + by_hi[ROUTER_ROWS:] + _dot_nt(rwh_ref[...], x_lo) + rb_ref[...]
    e_log = logits[:N_EXPERTS]
    g_log = logits[N_EXPERTS:N_EXPERTS + N_GROUPS]
    g_iota = lax.broadcasted_iota(I32, (N_GROUPS, tm), 0)
    g_max = jnp.max(g_log, axis=0, keepdims=True)
    g_top = jnp.min(jnp.where(g_log == g_max, g_iota, N_GROUPS), axis=0, keepdims=True)
    g_w = 1.0 / jnp.sum(jnp.exp(g_log - g_max), axis=0, keepdims=True)
    e_iota = lax.broadcasted_iota(I32, (N_EXPERTS, tm), 0)
    val = jnp.where(e_iota // EXPERTS_PER_GROUP == g_top, e_log, -jnp.inf)
    m1 = jnp.max(val, axis=0, keepdims=True)
    i1 = jnp.min(jnp.where(val == m1, e_iota, N_EXPERTS), axis=0, keepdims=True)
    val2 = jnp.where(e_iota == i1, -jnp.inf, val)
    m2 = jnp.max(val2, axis=0, keepdims=True)
    i2 = jnp.min(jnp.where(val2 == m2, e_iota, N_EXPERTS), axis=0, keepdims=True)
    e2 = jnp.exp(m2 - m1)
    w0 = g_w / (1.0 + e2)
    w1 = g_w * e2 / (1.0 + e2)
    oh0 = e_iota == i1
    oh1 = e_iota == i2
    oh = jnp.concatenate([jnp.where(oh0, 1.0, 0.0), jnp.where(oh1, 1.0, 0.0)], axis=0).astype(BF16)
    ta = lax.broadcasted_iota(I32, (tm, tm), 0)
    tb = lax.broadcasted_iota(I32, (tm, tm), 1)
    before = jnp.where(ta < tb, 1.0, 0.0).astype(BF16)
    rank = _dot(oh, before)
    c0 = jnp.sum(jnp.where(oh0, 1.0, 0.0), axis=1, keepdims=True)
    c1 = jnp.sum(jnp.where(oh1, 1.0, 0.0), axis=1, keepdims=True)
    nchunk = jnp.floor((c0 + c1 + (CHUNK_ROWS - 1)) * (1.0 / CHUNK_ROWS))
    nchunk_b = jnp.broadcast_to(nchunk, (N_EXPERTS, LANES))
    ea = lax.broadcasted_iota(I32, (N_EXPERTS, N_EXPERTS), 0)
    eb = lax.broadcasted_iota(I32, (N_EXPERTS, N_EXPERTS), 1)
    lower = jnp.where(eb < ea, 1.0, 0.0).astype(BF16)
    loc = CHUNK_ROWS * _dot(lower, nchunk_b.astype(BF16))[:, 0:1]
    dest0 = jnp.sum(jnp.where(oh0, loc + rank[:N_EXPERTS], 0.0), axis=0, keepdims=True)
    dest1 = jnp.sum(jnp.where(oh1, loc + c0 + rank[N_EXPERTS:], 0.0), axis=0, keepdims=True)
    r_iota = lax.broadcasted_iota(I32, (rr, tm), 0)
    d0 = dest0.astype(I32)
    d1 = dest1.astype(I32)
    perm = jnp.where(r_iota == d0, 1.0, jnp.where(r_iota == d1, 1.0, 0.0)).astype(BF16)
    xs_ref[...] = _dot(perm, x_hi).astype(BF16)
    cnt_ref[...] = nchunk_b
    m_iota = lax.broadcasted_iota(I32, (8, tm), 0)
    meta = jnp.where(m_iota == 0, dest0, jnp.where(m_iota == 1, dest1, jnp.where(m_iota == 2, w0,
                     jnp.where(m_iota == 3, w1, 0.0))))
    meta = jnp.concatenate([meta, jnp.zeros((LANES - 8, tm), F32)], axis=0)
    meta_ref[...] = jnp.transpose(meta)


def _mix_out(h, ya, yb, yc, ag, wa, wb, wc, n2, rwh, rwl, rb, tm):
    t = h.shape[0]
    nt = t // tm
    rr = _dispatch_rows(tm)
    row = lambda n: pl.BlockSpec((tm, n), lambda i: (i, 0))
    full = lambda a: pl.BlockSpec(a.shape, lambda i: (0,) * a.ndim)
    return pl.pallas_call(
        _mix_out_kernel,
        grid=(nt,),
        in_specs=[row(D_MODEL), row(SB_WIDTH), row(ML_PAD), row(SGU_WIDTH), full(ag), full(wa), full(wb),
                  full(wc), full(n2), full(rwh), full(rwl), full(rb)],
        out_specs=[row(D_MODEL), pl.BlockSpec((rr, D_MODEL), lambda i: (i, 0)), row(LANES),
                   pl.BlockSpec((None, N_EXPERTS, LANES), lambda i: (i, 0, 0))],
        out_shape=[jax.ShapeDtypeStruct((t, D_MODEL), F32),
                   jax.ShapeDtypeStruct((nt * rr, D_MODEL), BF16),
                   jax.ShapeDtypeStruct((t, LANES), F32),
                   jax.ShapeDtypeStruct((nt, N_EXPERTS, LANES), F32)],
        compiler_params=_cparams(("parallel",)),
        name="mix_out",
    )(h, ya, yb, yc, ag, wa, wb, wc, n2, rwh, rwl, rb)


def _chunk_tables(cnt, rr, n_blocks_max):
    nt = cnt.shape[0]
    ne = N_EXPERTS + 1
    tail = rr // CHUNK_ROWS - jnp.sum(cnt, axis=1)
    cnt = jnp.concatenate([cnt, tail[:, None]], axis=1)
    loc = (jnp.cumsum(cnt, axis=1) - cnt) * CHUNK_ROWS
    cnt_e = cnt.T
    cum_e = jnp.cumsum(cnt_e, axis=1)
    total = cum_e[:, -1]
    nblk = (total + EXPERT_BLOCK_CHUNKS - 1) // EXPERT_BLOCK_CHUNKS
    bend = jnp.cumsum(nblk)
    bstart = bend - nblk
    n_blocks = bend[-1]
    i = jnp.arange(n_blocks_max, dtype=I32)
    be = jnp.minimum(jnp.sum(bend[None, :] <= i[:, None], axis=1), ne - 1).astype(I32)
    oh_b = be[:, None] == jnp.arange(ne, dtype=I32)[None, :]
    pick = lambda v: jnp.sum(jnp.where(oh_b, v[None, :], 0), axis=1)
    pick2 = lambda m: jnp.sum(jnp.where(oh_b[:, :, None], m[None, :, :], 0), axis=1)
    q = (i - pick(bstart))[:, None] * EXPERT_BLOCK_CHUNKS + jnp.arange(EXPERT_BLOCK_CHUNKS, dtype=I32)[None, :]
    valid = (q < pick(total)[:, None]) & (i < n_blocks)[:, None]
    cum_b = pick2(cum_e)
    base = jnp.arange(nt, dtype=I32)[None, :] * rr + loc.T - (cum_e - cnt_e) * CHUNK_ROWS
    base_b = pick2(base)
    j = jnp.minimum(jnp.sum(cum_b[:, None, :] <= q[:, :, None], axis=-1), nt - 1)
    oh_j = j[:, :, None] == jnp.arange(nt, dtype=I32)[None, None, :]
    row = jnp.sum(jnp.where(oh_j, base_b[:, None, :], 0), axis=-1) + q * CHUNK_ROWS
    spare = nt * rr + ((i % 2)[:, None] * EXPERT_BLOCK_CHUNKS
                       + jnp.arange(EXPERT_BLOCK_CHUNKS, dtype=I32)[None, :]) * CHUNK_ROWS
    src = jnp.where(valid, row, 0).astype(I32)
    dst = jnp.where(valid, row, spare).astype(I32)
    return be, src.reshape(-1), dst.reshape(-1), n_blocks.reshape(1).astype(I32)


def _experts_kernel(be_ref, src_ref, dst_ref, nb_ref, xs_ref, wg_ref, wu_ref, wd_ref, ys_ref,
                    xbuf, ybuf, wgb, wub, wdb, sem_in, sem_out):
    i = pl.program_id(0)
    nb = nb_ref[0]
    slot = lax.rem(i, 2)

    def is_compute(b):
        return be_ref[b] < N_EXPERTS

    def start_loads(b, sl):
        @pl.when(is_compute(b))
        def _():
            for s in range(EXPERT_BLOCK_CHUNKS):
                src = pl.multiple_of(src_ref[b * EXPERT_BLOCK_CHUNKS + s], CHUNK_ROWS)
                pltpu.make_async_copy(xs_ref.at[pl.ds(src, CHUNK_ROWS), :],
                                      xbuf.at[sl, s * CHUNK_ROWS:(s + 1) * CHUNK_ROWS, :], sem_in.at[sl]).start()

    def wait_loads(b, sl):
        @pl.when(is_compute(b))
        def _():
            pltpu.make_async_copy(xs_ref.at[0:EXPERT_BLOCK, :], xbuf.at[sl], sem_in.at[sl]).wait()

    def start_stores(b, sl):
        for s in range(EXPERT_BLOCK_CHUNKS):
            dst = pl.multiple_of(dst_ref[b * EXPERT_BLOCK_CHUNKS + s], CHUNK_ROWS)
            pltpu.make_async_copy(ybuf.at[sl, s * CHUNK_ROWS:(s + 1) * CHUNK_ROWS, :],
                                  ys_ref.at[pl.ds(dst, CHUNK_ROWS), :], sem_out.at[sl]).start()

    def wait_stores(sl):
        pltpu.make_async_copy(ybuf.at[sl], ys_ref.at[0:EXPERT_BLOCK, :], sem_out.at[sl]).wait()

    @pl.when(i == 0)
    def _():
        ybuf[...] = jnp.zeros(ybuf.shape, BF16)
        real_rows = ys_ref.shape[0] - 2 * EXPERT_BLOCK
        for par in range(2):
            fill = pltpu.make_async_copy(
                ybuf.at[par], ys_ref.at[real_rows + par * EXPERT_BLOCK:real_rows + (par + 1) * EXPERT_BLOCK, :],
                sem_out.at[par])
            fill.start()
            fill.wait()

        start_loads(0, 0)

    @pl.when(i < nb)
    def _():
        @pl.when(i + 1 < nb)
        def _():
            start_loads(i + 1, 1 - slot)

        wait_loads(i, slot)

        @pl.when(i >= 2)
        def _():
            wait_stores(slot)

        @pl.when(is_compute(i))
        def _():
            @pl.when((i == 0) | (be_ref[i] != be_ref[jnp.maximum(i - 1, 0)]))
            def _():
                wgb[...] = wg_ref[...].astype(BF16)
                wub[...] = wu_ref[...].astype(BF16)
                wdb[...] = wd_ref[...].astype(BF16)

            x = xbuf[slot]
            gate = _dot(x, wgb[...])
            up = _dot(x, wub[...])
            mid = (gate * _sigmoid(gate) * up).astype(BF16)
            ybuf[slot] = _dot(mid, wdb[...]).astype(BF16)

        @pl.when(jnp.logical_not(is_compute(i)))
        def _():
            ybuf[slot] = jnp.zeros((EXPERT_BLOCK, D_MODEL), BF16)

        start_stores(i, slot)

        @pl.when(i == nb - 1)
        def _():
            @pl.when(i >= 1)
            def _():
                wait_stores(1 - slot)

            wait_stores(slot)


def _experts(be, src, dst, nb, xs, wg, wu, wd, layer):
    n_blocks_max = be.shape[0]
    wspec = lambda a: pl.BlockSpec(
        (None,) + a.shape[1:],
        lambda i, be, src, dst, nb: (layer * N_EXPERTS + jnp.minimum(be[i], N_EXPERTS - 1), 0, 0))
    grid_spec = pltpu.PrefetchScalarGridSpec(
        num_scalar_prefetch=4,
        grid=(n_blocks_max,),
        in_specs=[pl.BlockSpec(memory_space=pl.ANY), wspec(wg), wspec(wu), wspec(wd)],
        out_specs=pl.BlockSpec(memory_space=pl.ANY),
        scratch_shapes=[pltpu.VMEM((2, EXPERT_BLOCK, D_MODEL), BF16), pltpu.VMEM((2, EXPERT_BLOCK, D_MODEL), BF16),
                        pltpu.VMEM(wg.shape[1:], BF16), pltpu.VMEM(wu.shape[1:], BF16),
                        pltpu.VMEM(wd.shape[1:], BF16),
                        pltpu.SemaphoreType.DMA((2,)), pltpu.SemaphoreType.DMA((2,))],
    )
    return pl.pallas_call(
        _experts_kernel,
        grid_spec=grid_spec,
        out_shape=jax.ShapeDtypeStruct((xs.shape[0] + 2 * EXPERT_BLOCK, D_MODEL), BF16),
        compiler_params=_cparams(("arbitrary",)),
        name="experts",
    )(be, src, dst, nb, xs, wg, wu, wd)


def _combine_kernel(h_ref, ys_ref, meta_ref, p_ref, pg_ref, gw_ref, pw_ref, fg_ref, o_ref, *, final):
    tm = h_ref.shape[0]
    rr = ys_ref.shape[0]
    meta = meta_ref[...]
    d0 = meta[:, 0:1].astype(I32)
    d1 = meta[:, 1:2].astype(I32)
    r_iota = lax.broadcasted_iota(I32, (tm, rr), 1)
    unperm = (jnp.where(r_iota == d0, meta[:, 2:3], 0.0) + jnp.where(r_iota == d1, meta[:, 3:4], 0.0)).astype(BF16)
    h2 = h_ref[...] + _dot(unperm, ys_ref[...])
    gate = _sigmoid(_dot(_rms(h2, pg_ref[...]).astype(BF16), gw_ref[...]))
    h3 = h2 + gate * _dot(p_ref[...].astype(BF16), pw_ref[...])
    if final:
        h3 = _rms(h3, fg_ref[...])
    o_ref[...] = h3


def _combine(h1, ys, meta, p, layer, pg, gw, pw, fg, tm, final):
    t = h1.shape[0]
    nt = t // tm
    rr = _dispatch_rows(tm)
    row = lambda n: pl.BlockSpec((tm, n), lambda i: (i, 0))
    full = lambda a: pl.BlockSpec(a.shape, lambda i: (0,) * a.ndim)
    return pl.pallas_call(
        functools.partial(_combine_kernel, final=final),
        grid=(nt,),
        in_specs=[row(D_MODEL), pl.BlockSpec((rr, D_MODEL), lambda i: (i, 0)), row(LANES),
                  pl.BlockSpec((tm, P_DIM), lambda i: (layer * nt + i, 0)),
                  full(pg), full(gw), full(pw), full(fg)],
        out_specs=row(D_MODEL),
        out_shape=jax.ShapeDtypeStruct((t, D_MODEL), F32),
        compiler_params=_cparams(("parallel",)),
        name="combine",
    )(h1, ys, meta, p, pg, gw, pw, fg)


def _pad_heads(a, axis):
    shape = a.shape
    a = a.reshape(shape[:axis] + (ML_HEADS, ML_HEAD_DIM) + shape[axis + 1:])
    pad = [(0, 0)] * a.ndim
    pad[axis + 1] = (0, LANES - ML_HEAD_DIM)
    a = jnp.pad(a, pad)
    return a.reshape(shape[:axis] + (ML_PAD,) + shape[axis + 1:])


def _layer_params(i, w_in, conv_w, conv_b, igate_b, fgate_b, mnorm_g, sgu_b, w_out, router_gw, router_gb,
                  router_ew, router_eb):
    w = w_in[i]
    s, m = SB_WIDTH, ML_WIDTH
    a_q, a_k, a_v = w[:, 0:s] * (SB_HEAD_DIM ** -0.5 * LOG2_E), w[:, s:2 * s], w[:, 2 * s:3 * s]
    o = 3 * s
    b_q, b_k, b_v, b_o = (w[:, o + k * m:o + (k + 1) * m] for k in range(4))
    o = o + 4 * m
    gates = w[:, o:o + 2 * ML_HEADS]
    c_uv = w[:, o + 2 * ML_HEADS:]
    w_r = jnp.concatenate([a_q, a_k, a_v, _pad_heads(b_q, 1), _pad_heads(b_k, 1), _pad_heads(b_v, 1),
                           _pad_heads(b_o, 1), c_uv,
                           jnp.pad(gates, ((0, 0), (0, LANES - 2 * ML_HEADS)))], axis=1).astype(BF16)
    cw = jnp.concatenate([_pad_heads(conv_w[i][:, :m], 1), _pad_heads(conv_w[i][:, m:], 1)], axis=1)
    cb = jnp.concatenate([_pad_heads(conv_b[i][:m], 0), _pad_heads(conv_b[i][m:], 0)])[None, :]
    gb = jnp.pad(jnp.concatenate([igate_b[i], fgate_b[i]]), (0, LANES - 2 * ML_HEADS))[None, :]
    mg = _pad_heads(mnorm_g[i], 0)[None, :]
    sgu_bias = jnp.repeat(sgu_b[i].T, SGU_GROUP_DIM, axis=1)
    wo = w_out[i]
    wa = wo[:s].astype(BF16)
    wb = _pad_heads(wo[s:s + m], 0).astype(BF16)
    wc = wo[s + m:].astype(BF16)
    rw = jnp.concatenate([router_ew[i].T, router_gw[i].T,
                          jnp.zeros((ROUTER_ROWS - N_EXPERTS - N_GROUPS, D_MODEL), F32)], axis=0)
    rwh = rw.astype(BF16)
    rwl = (rw - rwh.astype(F32)).astype(BF16)
    rb = jnp.concatenate([router_eb[i], router_gb[i],
                          jnp.zeros((ROUTER_ROWS - N_EXPERTS - N_GROUPS,), F32)])[:, None]
    return w_r, cw, cb, gb, mg, sgu_bias, wa, wb, wc, rwh, rwl, rb


def kernel(x, p, norm1_g, w_in, conv_w, conv_b, igate_b, fgate_b, mnorm_g, sb_out_g, sgu_ln_g, sgu_ln_b, sgu_w,
           sgu_b, sgu_out_g, w_out, norm2_g, router_gw, router_gb, router_ew, router_eb, w_gate, w_up, w_down,
           ple_norm_g, ple_gate_w, ple_proj_w, final_g, *, tile=512):
    batch, seq, d = x.shape
    depth = w_in.shape[0]
    t = batch * seq
    tm = min(tile, t)
    nt = t // tm
    rr = _dispatch_rows(tm)
    n_blocks_max = (nt * rr // CHUNK_ROWS) // EXPERT_BLOCK_CHUNKS + N_EXPERTS + 1
    h = x.astype(F32).reshape(t, d)
    p2 = p.reshape(depth * t, P_DIM)
    wg_all = w_gate.reshape((depth * N_EXPERTS,) + w_gate.shape[2:])
    wu_all = w_up.reshape((depth * N_EXPERTS,) + w_up.shape[2:])
    wd_all = w_down.reshape((depth * N_EXPERTS,) + w_down.shape[2:])
    for i in range(depth):
        (w_r, cw, cb, gb, mg, sgu_bias, wa, wb, wc, rwh, rwl, rb) = _layer_params(
            i, w_in, conv_w, conv_b, igate_b, fgate_b, mnorm_g, sgu_b, w_out, router_gw, router_gb,
            router_ew, router_eb)
        za, zqk, zvo, yc, zg = _in_proj(h, norm1_g[i][None, :], w_r, sgu_ln_g[i][None, :], sgu_ln_b[i][None, :],
                                        sgu_w[i], sgu_bias, sgu_out_g[i][None, :], min(2 * tm, seq))
        ya = _sb_attn(za, batch, seq).reshape(t, SB_WIDTH)
        yb = _mlstm(zqk, zvo, zg, cw, cb, gb, mg, batch, seq).reshape(t, ML_PAD)
        h1, xs, meta, cnt = _mix_out(h, ya, yb, yc, sb_out_g[i][None, :], wa, wb, wc, norm2_g[i][None, :],
                                     rwh, rwl, rb, tm)
        be, src, dst, nb = _chunk_tables(cnt[:, :, 0].astype(I32), rr, n_blocks_max)
        ys = _experts(be, src, dst, nb, xs, wg_all, wu_all, wd_all, i)
        h = _combine(h1, ys, meta, p2, i, ple_norm_g[i][None, :], ple_gate_w[i].astype(BF16),
                     ple_proj_w[i].astype(BF16), final_g[None, :], tm, i == depth - 1)
    return h.reshape(batch, seq, d).astype(x.dtype)
```

```python
import functools

import jax
import jax.numpy as jnp
import numpy as np
from jax import lax
from jax.experimental import pallas as pl
from jax.experimental.pallas import tpu as pltpu

F32 = jnp.float32
BF16 = jnp.bfloat16
I32 = jnp.int32

D_MODEL = 1024
P_DIM = 256
EPS = 1e-6
LANES = 128
SB_HEAD_DIM = 64
SB_WIDTH = 384
SB_BLOCK = 128
SB_QTILE = 512
LOG2_E = 1.4426950408889634
ML_HEADS = 4
ML_HEAD_DIM = 96
ML_WIDTH = 384
ML_PAD = ML_HEADS * LANES
ML_CHUNK = 128
CONV_K = 4
SGU_WIDTH = 256
SGU_GROUPS = 4
SGU_GROUP_DIM = 64
SGU_BLOCK = 128
STREAM_CHUNK = 64
N_GROUPS = 4
EXPERTS_PER_GROUP = 8
N_EXPERTS = 32
D_EXPERT = 256
ROUTER_ROWS = 40
CHUNK_ROWS = 16
EXPERT_BLOCK_CHUNKS = 32
EXPERT_BLOCK = CHUNK_ROWS * EXPERT_BLOCK_CHUNKS

COL_A = 0
COL_BQK = 3 * SB_WIDTH
COL_BVO = COL_BQK + 2 * ML_PAD
COL_C = COL_BVO + 2 * ML_PAD
COL_G = COL_C + 2 * SGU_WIDTH
N_Z = COL_G + LANES

VMEM_LIMIT = 56 * 1024 * 1024


def _cparams(sem, vmem=VMEM_LIMIT):
    return pltpu.CompilerParams(dimension_semantics=sem, vmem_limit_bytes=vmem)


def _rms(x, g):
    return x * lax.rsqrt(jnp.mean(x * x, axis=-1, keepdims=True) + EPS) * g


def _sigmoid(x):
    return 0.5 * jnp.tanh(0.5 * x) + 0.5


def _split_bf16(x):
    hi = x.astype(BF16)
    lo = (x - hi.astype(F32)).astype(BF16)
    return hi, lo


def _dot(a, b):
    return jnp.dot(a, b, preferred_element_type=F32)


def _dot_nt(a, b):
    return lax.dot_general(a, b, (((1,), (1,)), ((), ())), preferred_element_type=F32)


def _dot_tn(a, b):
    return lax.dot_general(a, b, (((0,), (0,)), ((), ())), preferred_element_type=F32)


def _dot_split_lhs(x, m):
    hi, lo = _split_bf16(x)
    return _dot(hi, m) + _dot(lo, m)


def _dot_split_rhs(m, x):
    hi, lo = _split_bf16(x)
    return _dot(m, hi) + _dot(m, lo)


def _spatial_gating(zc, lg, lb, w_ref, bias, og):
    W = SGU_WIDTH
    gi = lax.broadcasted_iota(I32, (W, W), 0) // SGU_GROUP_DIM
    gj = lax.broadcasted_iota(I32, (W, W), 1) // SGU_GROUP_DIM
    avg = jnp.where(gi == gj, 1.0 / SGU_GROUP_DIM, 0.0).astype(BF16)
    ti = lax.broadcasted_iota(I32, (SGU_BLOCK, SGU_BLOCK), 0) // STREAM_CHUNK
    si = lax.broadcasted_iota(I32, (SGU_BLOCK, SGU_BLOCK), 1) // STREAM_CHUNK
    chunk_causal = si <= ti
    lane_group = lax.broadcasted_iota(I32, (SGU_BLOCK, W), 1) // SGU_GROUP_DIM
    wg = [jnp.where(chunk_causal, w_ref[g], 0.0).astype(BF16) for g in range(SGU_GROUPS)]
    wpair = [jnp.concatenate(wg[g:g + 2], axis=1) for g in range(0, SGU_GROUPS, 2)]
    out = []
    for r in range(zc.shape[0] // SGU_BLOCK):
        rs = slice(r * SGU_BLOCK, (r + 1) * SGU_BLOCK)
        u = jax.nn.gelu(zc[rs, :W])
        v = jax.nn.gelu(zc[rs, W:])
        mu = _dot_split_lhs(v, avg)
        vc = v - mu
        var = _dot_split_lhs(vc * vc, avg)
        vn = (vc * lax.rsqrt(var + EPS) * lg + lb).astype(BF16)
        mixed = bias
        zero = jnp.zeros_like(vn)
        for p, g in enumerate(range(0, SGU_GROUPS, 2)):
            vpair = jnp.concatenate([jnp.where(lane_group == g, vn, zero),
                                     jnp.where(lane_group == g + 1, vn, zero)], axis=0)
            mixed = mixed + _dot(wpair[p], vpair)
        out.append(_rms(u * mixed, og))
    return jnp.concatenate(out, axis=0)


def _in_proj_kernel(x_ref, g_ref, w_ref, lg_ref, lb_ref, sw_ref, sb_ref, og_ref,
                    za_ref, zqk_ref, zvo_ref, yc_ref, zg_ref):
    hn = _rms(x_ref[...], g_ref[...]).astype(BF16)
    za_ref[...] = _dot(hn, w_ref[:, COL_A:COL_BQK]).astype(BF16)
    zqk_ref[...] = _dot(hn, w_ref[:, COL_BQK:COL_BVO])
    zvo_ref[...] = _dot(hn, w_ref[:, COL_BVO:COL_C]).astype(BF16)
    zg_ref[...] = _dot(hn, w_ref[:, COL_G:N_Z])
    zc = _dot(hn, w_ref[:, COL_C:COL_G])
    yc_ref[...] = _spatial_gating(zc, lg_ref[...], lb_ref[...], sw_ref, sb_ref[...], og_ref[...]).astype(BF16)


def _in_proj(h, g, w, lg, lb, sw, sbias, og, tm):
    t = h.shape[0]
    row = lambda n: pl.BlockSpec((tm, n), lambda i: (i, 0))
    full = lambda a: pl.BlockSpec(a.shape, lambda i: (0,) * a.ndim)
    resident = pl.BlockSpec(w.shape, lambda i: (0, 0), pipeline_mode=pl.Buffered(1))
    return pl.pallas_call(
        _in_proj_kernel,
        grid=(t // tm,),
        in_specs=[row(D_MODEL), full(g), resident, full(lg), full(lb), full(sw), full(sbias), full(og)],
        out_specs=[row(3 * SB_WIDTH), row(2 * ML_PAD), row(2 * ML_PAD), row(SGU_WIDTH), row(LANES)],
        out_shape=[jax.ShapeDtypeStruct((t, 3 * SB_WIDTH), BF16),
                   jax.ShapeDtypeStruct((t, 2 * ML_PAD), F32),
                   jax.ShapeDtypeStruct((t, 2 * ML_PAD), BF16),
                   jax.ShapeDtypeStruct((t, SGU_WIDTH), BF16),
                   jax.ShapeDtypeStruct((t, LANES), F32)],
        compiler_params=_cparams(("parallel",)),
        name="in_proj",
    )(h, g, w, lg, lb, sw, sbias, og)


def _sb_kernel(q_ref, k_ref, v_ref, o_ref, acc_ref, car_ref, pre_ref, tot_ref):
    n = pl.program_id(2)
    blk = SB_BLOCK
    tq = q_ref.shape[0]
    sub = tq // blk
    lane_k = lax.broadcasted_iota(I32, (blk, LANES), 1)
    head0 = lane_k < SB_HEAD_DIM
    mj = lax.broadcasted_iota(I32, (2 * blk, 2 * blk), 0)
    ms = lax.broadcasted_iota(I32, (2 * blk, 2 * blk), 1)
    same_head = (mj // blk) == (ms // blk)
    suffix = jnp.where(same_head & (mj > ms), 1.0, 0.0).astype(BF16)
    totals = jnp.where(same_head, 1.0, 0.0).astype(BF16)

    def per_head(x):
        zero = jnp.zeros_like(x)
        return jnp.concatenate([jnp.where(head0, x, zero), jnp.where(head0, zero, x)], axis=0)

    def block_at(ref, j):
        return per_head(ref[pl.ds(pl.multiple_of(j * blk, blk), blk), :])

    def stage1(j, r0, masked, buf):
        z = _dot_nt(q_ref[r0:, :], block_at(k_ref, j))
        sign = jnp.uint32(0x80000000)
        neg_abs = lax.bitcast_convert_type(lax.bitcast_convert_type(z, jnp.uint32) | sign, F32)
        e = jnp.exp2(neg_abs)
        log_sig = jnp.minimum(z, 0.0) - jnp.log2(1.0 + e)
        log1m = log_sig - z
        if masked:
            mask = (lax.broadcasted_iota(I32, (blk, 2 * blk), 1) % blk
                    < lax.broadcasted_iota(I32, (blk, 2 * blk), 0))
            top_l = jnp.where(mask, log1m[:blk], 0.0)
            top_s = jnp.where(mask, log_sig[:blk], -jnp.inf)
            if tq - r0 == blk:
                log1m, log_sig = top_l, top_s
            else:
                log1m = jnp.concatenate([top_l, log1m[blk:]], axis=0)
                log_sig = jnp.concatenate([top_s, log_sig[blk:]], axis=0)
        terms = log1m.astype(BF16)
        pre_ref[buf, r0:, :] = log_sig + _dot(terms, suffix)
        tot_ref[buf, r0:, :] = _dot(terms, totals)

    def stage2(j, r0, buf):
        arg = pre_ref[buf, r0:, :] + car_ref[r0:, :]
        acc_ref[r0:, :] += _dot(jnp.exp2(arg).astype(BF16), block_at(v_ref, j))
        car_ref[r0:, :] += tot_ref[buf, r0:, :]

    acc_ref[...] = jnp.zeros_like(acc_ref)
    car_ref[...] = jnp.zeros_like(car_ref)

    for kj in reversed(range(sub)):
        stage1(n * sub + kj, kj * blk, True, kj)
    for kj in reversed(range(sub)):
        stage2(n * sub + kj, kj * blk, kj)

    top = n * sub - 1

    @pl.when(n > 0)
    def _():
        stage1(top, 0, False, 0)

    unroll = 4

    def body(i, c):
        j0 = top - unroll * i
        for u in range(unroll):
            nxt = j0 - u - 1
            if u == unroll - 1:
                nxt = jnp.maximum(nxt, 0)
            stage1(nxt, 0, False, (u + 1) % 2)
            stage2(j0 - u, 0, u % 2)
        return c

    lax.fori_loop(0, n * (sub // unroll), body, 0)
    o_ref[...] = acc_ref[...].astype(o_ref.dtype)


def _sb_attn(za, batch, seq):
    za3 = za.reshape(batch, seq, 3 * SB_WIDTH)
    pairs = SB_WIDTH // LANES
    tq = min(SB_QTILE, seq)
    return pl.pallas_call(
        _sb_kernel,
        grid=(batch, pairs, seq // tq),
        in_specs=[pl.BlockSpec((None, tq, LANES), lambda b, p, n: (b, n, p)),
                  pl.BlockSpec((None, seq, LANES), lambda b, p, n: (b, 0, pairs + p)),
                  pl.BlockSpec((None, seq, LANES), lambda b, p, n: (b, 0, 2 * pairs + p))],
        out_specs=pl.BlockSpec((None, tq, LANES), lambda b, p, n: (b, n, p)),
        out_shape=jax.ShapeDtypeStruct((batch, seq, SB_WIDTH), BF16),
        scratch_shapes=[pltpu.VMEM((tq, LANES), F32),
                        pltpu.VMEM((tq, 2 * LANES), F32),
                        pltpu.VMEM((tq // SB_BLOCK, tq, 2 * LANES), F32),
                        pltpu.VMEM((tq // SB_BLOCK, tq, 2 * LANES), F32)],
        compiler_params=_cparams(("parallel", "parallel", "arbitrary")),
        name="sb_attn",
    )(za3, za3, za3)


def _mlstm_kernel(qk_ref, vo_ref, g_ref, cw_ref, cb_ref, gb_ref, mg_ref, o_ref,
                  q_s, k_s, vt_s, cum_s, gmb_s, pm_s, gmbt_s, ct_s, m_s):
    seq = qk_ref.shape[0]
    L = ML_CHUNK
    cw = cw_ref[...]
    cb = cb_ref[...]
    rows = lax.broadcasted_iota(I32, (L, 2 * ML_PAD), 0)

    def conv_silu(r0, taps):
        y = taps[0] * cw[CONV_K - 1:CONV_K, :] + cb
        for d in range(1, CONV_K):
            y = y + taps[d] * cw[CONV_K - 1 - d:CONV_K - d, :]
        y = y * _sigmoid(y)
        q_s[pl.ds(r0, L), :] = y[:, :ML_PAD].astype(BF16)
        k_s[pl.ds(r0, L), :] = (y[:, ML_PAD:] * (ML_HEAD_DIM ** -0.5)).astype(BF16)

    ct_s[...] = jnp.zeros_like(ct_s)
    m_s[...] = jnp.zeros_like(m_s)

    ti = lax.broadcasted_iota(I32, (L, L), 0)
    si = lax.broadcasted_iota(I32, (L, L), 1)
    causal = si <= ti
    tril = jnp.where(causal, 1.0, 0.0).astype(BF16)
    lane = lax.broadcasted_iota(I32, (L, LANES), 1)
    row = lax.broadcasted_iota(I32, (L, LANES), 0)
    ones2 = jnp.full((2 * LANES, LANES), 1.0, BF16)
    gb = gb_ref[...]
    mg = mg_ref[...]
    norm_lane = LANES - 1

    def gates(c):
        r0 = pl.multiple_of(c * L, L)
        g = g_ref[pl.ds(r0, L), :] + gb
        logf = jnp.minimum(g, 0.0) - jnp.log1p(jnp.exp(-jnp.abs(g)))
        cum = pltpu.roll(_dot_split_rhs(tril, logf), LANES - ML_HEADS, axis=1)
        gmb = g - cum
        pm = gmb
        step = 1
        while step < L:
            pm = jnp.maximum(pm, jnp.where(row >= step, pltpu.roll(pm, step, axis=0), -jnp.inf))
            step *= 2
        cum_s[pl.ds(r0, L), :] = cum
        gmb_s[pl.ds(r0, L), :] = gmb
        pm_s[pl.ds(r0, L), :] = pm
        gmbt_s[c] = jnp.transpose(gmb)
        for h in range(ML_HEADS):
            v = vo_ref[pl.ds(r0, L), h * LANES:(h + 1) * LANES].astype(F32)
            vt_s[h, c] = jnp.transpose(jnp.where(lane == norm_lane, 1.0, v)).astype(BF16)

    x0 = qk_ref[0:L, :]
    conv_silu(0, [x0] + [jnp.where(rows >= d, pltpu.roll(x0, d, axis=0), 0.0) for d in range(1, CONV_K)])

    def conv_chunk(c, carry):
        r0 = pl.multiple_of(c * L, L)
        xe = qk_ref[pl.ds(r0 - 8, L + 8), :]
        conv_silu(r0, [xe[8 - d:8 - d + L, :] for d in range(CONV_K)])
        return carry

    lax.fori_loop(1, seq // L, conv_chunk, 0)

    gate_unroll = 4 if (seq // L) % 4 == 0 else 1

    def gate_chunk(c, carry):
        for u in range(gate_unroll):
            gates(gate_unroll * c + u)
        return carry

    lax.fori_loop(0, seq // (gate_unroll * L), gate_chunk, 0)

    heads = range(ML_HEADS)

    def chunk(c, carry):
        r0 = pl.multiple_of(c * L, L)
        cum = cum_s[pl.ds(r0, L), :]
        gmb = gmb_s[pl.ds(r0, L), :]
        pm = pm_s[pl.ds(r0, L), :]
        gmb_t = gmbt_s[c]
        qc = [q_s[pl.ds(r0, L), h * LANES:(h + 1) * LANES] for h in heads]
        kc = [k_s[pl.ds(r0, L), h * LANES:(h + 1) * LANES] for h in heads]
        vt = [vt_s[h, c] for h in heads]
        m_in = [m_s[h][0:1, 0:1] for h in heads]
        ct = [ct_s[h] for h in heads]
        s_raw = [_dot_nt(qc[h], kc[h]) for h in heads]
        mm = [jnp.maximum(m_in[h], pm[:, h:h + 1]) for h in heads]
        inter = [jnp.exp(m_in[h] - mm[h]) for h in heads]
        num = []
        for h in heads:
            wd = jnp.where(causal, jnp.exp(gmb_t[h:h + 1, :] - mm[h]), 0.0)
            lhs = jnp.concatenate([(s_raw[h] * wd).astype(BF16),
                                   (qc[h].astype(F32) * inter[h]).astype(BF16)], axis=1)
            rhs_t = jnp.concatenate([vt[h], ct[h].astype(BF16)], axis=1)
            num.append(_dot_nt(lhs, rhs_t))
        hh = []
        ms = []
        for h in heads:
            den = jnp.maximum(jnp.abs(num[h][:, norm_lane:]), jnp.exp(-(cum[:, h:h + 1] + mm[h])))
            x = jnp.where(lane == norm_lane, 0.0, num[h] * (1.0 / den))
            hi, lo = _split_bf16(x * x)
            hh.append(x)
            ms.append(_dot(jnp.concatenate([hi, lo], axis=1), ones2) * (1.0 / ML_HEAD_DIM))
        for h in heads:
            cs = slice(h * LANES, (h + 1) * LANES)
            oc = vo_ref[pl.ds(r0, L), ML_PAD + h * LANES:ML_PAD + (h + 1) * LANES].astype(F32)
            o_ref[pl.ds(r0, L), cs] = (hh[h] * lax.rsqrt(ms[h] + EPS) * mg[:, cs]
                                       * _sigmoid(oc)).astype(o_ref.dtype)
        for h in heads:
            mm_last = mm[h][L - 1:L, :]
            kw = (kc[h].astype(F32) * jnp.exp(gmb[:, h:h + 1] - mm_last)).astype(BF16)
            ct_s[h] = inter[h][L - 1:L, :] * ct[h] + _dot(vt[h], kw)
            m_s[h] = jnp.broadcast_to(cum[L - 1:L, h:h + 1] + mm_last, (8, LANES))
        return carry

    lax.fori_loop(0, seq // L, chunk, 0)


def _mlstm(zqk, zvo, zg, cw, cb, gb, mg, batch, seq):
    full = lambda a: pl.BlockSpec(a.shape, lambda b: (0,) * a.ndim)
    seq_blk = lambda n: pl.BlockSpec((None, seq, n), lambda b: (b, 0, 0))
    return pl.pallas_call(
        _mlstm_kernel,
        grid=(batch,),
        in_specs=[seq_blk(2 * ML_PAD), seq_blk(2 * ML_PAD), seq_blk(LANES),
                  full(cw), full(cb), full(gb), full(mg)],
        out_specs=seq_blk(ML_PAD),
        out_shape=jax.ShapeDtypeStruct((batch, seq, ML_PAD), BF16),
        scratch_shapes=[pltpu.VMEM((seq, ML_PAD), BF16), pltpu.VMEM((seq, ML_PAD), BF16),
                        pltpu.VMEM((ML_HEADS, seq // ML_CHUNK, LANES, ML_CHUNK), BF16),
                        pltpu.VMEM((seq, LANES), F32), pltpu.VMEM((seq, LANES), F32),
                        pltpu.VMEM((seq, LANES), F32),
                        pltpu.VMEM((seq // ML_CHUNK, LANES, ML_CHUNK), F32),
                        pltpu.VMEM((ML_HEADS, LANES, LANES), F32),
                        pltpu.VMEM((ML_HEADS, 8, LANES), F32)],
        compiler_params=_cparams(("parallel",)),
        name="mlstm",
    )(zqk.reshape(batch, seq, 2 * ML_PAD), zvo.reshape(batch, seq, 2 * ML_PAD),
      zg.reshape(batch, seq, LANES), cw, cb, gb, mg)


def _dispatch_rows(tm):
    return 2 * tm + N_EXPERTS * (CHUNK_ROWS - 1) + (N_EXPERTS * (CHUNK_ROWS - 1)) % CHUNK_ROWS


def _mix_out_kernel(h_ref, ya_ref, yb_ref, yc_ref, ag_ref, wa_ref, wb_ref, wc_ref, n2_ref,
                    rwh_ref, rwl_ref, rb_ref, h1_ref, xs_ref, meta_ref, cnt_ref):
    tm = h_ref.shape[0]
    rr = xs_ref.shape[0]
    ya = _rms(ya_ref[...].astype(F32), ag_ref[...]).astype(BF16)
    h1 = h_ref[...] + _dot(ya, wa_ref[...]) + _dot(yb_ref[...], wb_ref[...]) + _dot(yc_ref[...], wc_ref[...])
    h1_ref[...] = h1
    xn = _rms(h1, n2_ref[...])
    x_hi, x_lo = _split_bf16(xn)
    by_hi = _dot_nt(jnp.concatenate([rwh_ref[...], rwl_ref[...]], axis=0), x_hi)
    logits = by_hi[:ROUTER_ROWS] + by_hi[ROUTER_ROWS:] + _dot_nt(rwh_ref[...], x_lo) + rb_ref[...]
    e_log = logits[:N_EXPERTS]
    g_log = logits[N_EXPERTS:N_EXPERTS + N_GROUPS]
    g_iota = lax.broadcasted_iota(I32, (N_GROUPS, tm), 0)
    g_max = jnp.max(g_log, axis=0, keepdims=True)
    g_top = jnp.min(jnp.where(g_log == g_max, g_iota, N_GROUPS), axis=0, keepdims=True)
    g_w = 1.0 / jnp.sum(jnp.exp(g_log - g_max), axis=0, keepdims=True)
    e_iota = lax.broadcasted_iota(I32, (N_EXPERTS, tm), 0)
    val = jnp.where(e_iota // EXPERTS_PER_GROUP == g_top, e_log, -jnp.inf)
    m1 = jnp.max(val, axis=0, keepdims=True)
    i1 = jnp.min(jnp.where(val == m1, e_iota, N_EXPERTS), axis=0, keepdims=True)
    val2 = jnp.where(e_iota == i1, -jnp.inf, val)
    m2 = jnp.max(val2, axis=0, keepdims=True)
    i2 = jnp.min(jnp.where(val2 == m2, e_iota, N_EXPERTS), axis=0, keepdims=True)
    e2 = jnp.exp(m2 - m1)
    w0 = g_w / (1.0 + e2)
    w1 = g_w * e2 / (1.0 + e2)
    oh0 = e_iota == i1
    oh1 = e_iota == i2
    oh = jnp.concatenate([jnp.where(oh0, 1.0, 0.0), jnp.where(oh1, 1.0, 0.0)], axis=0).astype(BF16)
    ta = lax.broadcasted_iota(I32, (tm, tm), 0)
    tb = lax.broadcasted_iota(I32, (tm, tm), 1)
    before = jnp.where(ta < tb, 1.0, 0.0).astype(BF16)
    rank = _dot(oh, before)
    c0 = jnp.sum(jnp.where(oh0, 1.0, 0.0), axis=1, keepdims=True)
    c1 = jnp.sum(jnp.where(oh1, 1.0, 0.0), axis=1, keepdims=True)
    nchunk = jnp.floor((c0 + c1 + (CHUNK_ROWS - 1)) * (1.0 / CHUNK_ROWS))
    nchunk_b = jnp.broadcast_to(nchunk, (N_EXPERTS, LANES))
    ea = lax.broadcasted_iota(I32, (N_EXPERTS, N_EXPERTS), 0)
    eb = lax.broadcasted_iota(I32, (N_EXPERTS, N_EXPERTS), 1)
    lower = jnp.where(eb < ea, 1.0, 0.0).astype(BF16)
    loc = CHUNK_ROWS * _dot(lower, nchunk_b.astype(BF16))[:, 0:1]
    dest0 = jnp.sum(jnp.where(oh0, loc + rank[:N_EXPERTS], 0.0), axis=0, keepdims=True)
    dest1 = jnp.sum(jnp.where(oh1, loc + c0 + rank[N_EXPERTS:], 0.0), axis=0, keepdims=True)
    r_iota = lax.broadcasted_iota(I32, (rr, tm), 0)
    d0 = dest0.astype(I32)
    d1 = dest1.astype(I32)
    perm = jnp.where(r_iota == d0, 1.0, jnp.where(r_iota == d1, 1.0, 0.0)).astype(BF16)
    xs_ref[...] = _dot(perm, x_hi).astype(BF16)
    cnt_ref[...] = nchunk_b
    m_iota = lax.broadcasted_iota(I32, (8, tm), 0)
    meta = jnp.where(m_iota == 0, dest0, jnp.where(m_iota == 1, dest1, jnp.where(m_iota == 2, w0,
                     jnp.where(m_iota == 3, w1, 0.0))))
    meta = jnp.concatenate([meta, jnp.zeros((LANES - 8, tm), F32)], axis=0)
    meta_ref[...] = jnp.transpose(meta)


def _mix_out(h, ya, yb, yc, ag, wa, wb, wc, n2, rwh, rwl, rb, tm):
    t = h.shape[0]
    nt = t // tm
    rr = _dispatch_rows(tm)
    row = lambda n: pl.BlockSpec((tm, n), lambda i: (i, 0))
    full = lambda a: pl.BlockSpec(a.shape, lambda i: (0,) * a.ndim)
    return pl.pallas_call(
        _mix_out_kernel,
        grid=(nt,),
        in_specs=[row(D_MODEL), row(SB_WIDTH), row(ML_PAD), row(SGU_WIDTH), full(ag), full(wa), full(wb),
                  full(wc), full(n2), full(rwh), full(rwl), full(rb)],
        out_specs=[row(D_MODEL), pl.BlockSpec((rr, D_MODEL), lambda i: (i, 0)), row(LANES),
                   pl.BlockSpec((None, N_EXPERTS, LANES), lambda i: (i, 0, 0))],
        out_shape=[jax.ShapeDtypeStruct((t, D_MODEL), F32),
                   jax.ShapeDtypeStruct((nt * rr, D_MODEL), BF16),
                   jax.ShapeDtypeStruct((t, LANES), F32),
                   jax.ShapeDtypeStruct((nt, N_EXPERTS, LANES), F32)],
        compiler_params=_cparams(("parallel",)),
        name="mix_out",
    )(h, ya, yb, yc, ag, wa, wb, wc, n2, rwh, rwl, rb)


def _chunk_tables(cnt, rr, n_blocks_max):
    nt = cnt.shape[0]
    ne = N_EXPERTS + 1
    tail = rr // CHUNK_ROWS - jnp.sum(cnt, axis=1)
    cnt = jnp.concatenate([cnt, tail[:, None]], axis=1)
    loc = (jnp.cumsum(cnt, axis=1) - cnt) * CHUNK_ROWS
    cnt_e = cnt.T
    cum_e = jnp.cumsum(cnt_e, axis=1)
    total = cum_e[:, -1]
    nblk = (total + EXPERT_BLOCK_CHUNKS - 1) // EXPERT_BLOCK_CHUNKS
    bend = jnp.cumsum(nblk)
    bstart = bend - nblk
    n_blocks = bend[-1]
    i = jnp.arange(n_blocks_max, dtype=I32)
    be = jnp.minimum(jnp.sum(bend[None, :] <= i[:, None], axis=1), ne - 1).astype(I32)
    oh_b = be[:, None] == jnp.arange(ne, dtype=I32)[None, :]
    pick = lambda v: jnp.sum(jnp.where(oh_b, v[None, :], 0), axis=1)
    pick2 = lambda m: jnp.sum(jnp.where(oh_b[:, :, None], m[None, :, :], 0), axis=1)
    q = (i - pick(bstart))[:, None] * EXPERT_BLOCK_CHUNKS + jnp.arange(EXPERT_BLOCK_CHUNKS, dtype=I32)[None, :]
    valid = (q < pick(total)[:, None]) & (i < n_blocks)[:, None]
    cum_b = pick2(cum_e)
    base = jnp.arange(nt, dtype=I32)[None, :] * rr + loc.T - (cum_e - cnt_e) * CHUNK_ROWS
    base_b = pick2(base)
    j = jnp.minimum(jnp.sum(cum_b[:, None, :] <= q[:, :, None], axis=-1), nt - 1)
    oh_j = j[:, :, None] == jnp.arange(nt, dtype=I32)[None, None, :]
    row = jnp.sum(jnp.where(oh_j, base_b[:, None, :], 0), axis=-1) + q * CHUNK_ROWS
    spare = nt * rr + ((i % 2)[:, None] * EXPERT_BLOCK_CHUNKS
                       + jnp.arange(EXPERT_BLOCK_CHUNKS, dtype=I32)[None, :]) * CHUNK_ROWS
    src = jnp.where(valid, row, 0).astype(I32)
    dst = jnp.where(valid, row, spare).astype(I32)
    return be, src.reshape(-1), dst.reshape(-1), n_blocks.reshape(1).astype(I32)


def _experts_kernel(be_ref, src_ref, dst_ref, nb_ref, xs_ref, wg_ref, wu_ref, wd_ref, ys_ref,
                    xbuf, ybuf, wgb, wub, wdb, sem_in, sem_out):
    i = pl.program_id(0)
    nb = nb_ref[0]
    slot = lax.rem(i, 2)

    def is_compute(b):
        return be_ref[b] < N_EXPERTS

    def start_loads(b, sl):
        @pl.when(is_compute(b))
        def _():
            for s in range(EXPERT_BLOCK_CHUNKS):
                src = pl.multiple_of(src_ref[b * EXPERT_BLOCK_CHUNKS + s], CHUNK_ROWS)
                pltpu.make_async_copy(xs_ref.at[pl.ds(src, CHUNK_ROWS), :],
                                      xbuf.at[sl, s * CHUNK_ROWS:(s + 1) * CHUNK_ROWS, :], sem_in.at[sl]).start()

    def wait_loads(b, sl):
        @pl.when(is_compute(b))
        def _():
            pltpu.make_async_copy(xs_ref.at[0:EXPERT_BLOCK, :], xbuf.at[sl], sem_in.at[sl]).wait()

    def start_stores(b, sl):
        for s in range(EXPERT_BLOCK_CHUNKS):
            dst = pl.multiple_of(dst_ref[b * EXPERT_BLOCK_CHUNKS + s], CHUNK_ROWS)
            pltpu.make_async_copy(ybuf.at[sl, s * CHUNK_ROWS:(s + 1) * CHUNK_ROWS, :],
                                  ys_ref.at[pl.ds(dst, CHUNK_ROWS), :], sem_out.at[sl]).start()

    def wait_stores(sl):
        pltpu.make_async_copy(ybuf.at[sl], ys_ref.at[0:EXPERT_BLOCK, :], sem_out.at[sl]).wait()

    @pl.when(i == 0)
    def _():
        ybuf[...] = jnp.zeros(ybuf.shape, BF16)
        real_rows = ys_ref.shape[0] - 2 * EXPERT_BLOCK
        for par in range(2):
            fill = pltpu.make_async_copy(
                ybuf.at[par], ys_ref.at[real_rows + par * EXPERT_BLOCK:real_rows + (par + 1) * EXPERT_BLOCK, :],
                sem_out.at[par])
            fill.start()
            fill.wait()

        start_loads(0, 0)

    @pl.when(i < nb)
    def _():
        @pl.when(i + 1 < nb)
        def _():
            start_loads(i + 1, 1 - slot)

        wait_loads(i, slot)

        @pl.when(i >= 2)
        def _():
            wait_stores(slot)

        @pl.when(is_compute(i))
        def _():
            @pl.when((i == 0) | (be_ref[i] != be_ref[jnp.maximum(i - 1, 0)]))
            def _():
                wgb[...] = wg_ref[...].astype(BF16)
                wub[...] = wu_ref[...].astype(BF16)
                wdb[...] = wd_ref[...].astype(BF16)

            x = xbuf[slot]
            gate = _dot(x, wgb[...])
            up = _dot(x, wub[...])
            mid = (gate * _sigmoid(gate) * up).astype(BF16)
            ybuf[slot] = _dot(mid, wdb[...]).astype(BF16)

        @pl.when(jnp.logical_not(is_compute(i)))
        def _():
            ybuf[slot] = jnp.zeros((EXPERT_BLOCK, D_MODEL), BF16)

        start_stores(i, slot)

        @pl.when(i == nb - 1)
        def _():
            @pl.when(i >= 1)
            def _():
                wait_stores(1 - slot)

            wait_stores(slot)


def _experts(be, src, dst, nb, xs, wg, wu, wd, layer):
    n_blocks_max = be.shape[0]
    wspec = lambda a: pl.BlockSpec(
        (None,) + a.shape[1:],
        lambda i, be, src, dst, nb: (layer * N_EXPERTS + jnp.minimum(be[i], N_EXPERTS - 1), 0, 0))
    grid_spec = pltpu.PrefetchScalarGridSpec(
        num_scalar_prefetch=4,
        grid=(n_blocks_max,),
        in_specs=[pl.BlockSpec(memory_space=pl.ANY), wspec(wg), wspec(wu), wspec(wd)],
        out_specs=pl.BlockSpec(memory_space=pl.ANY),
        scratch_shapes=[pltpu.VMEM((2, EXPERT_BLOCK, D_MODEL), BF16), pltpu.VMEM((2, EXPERT_BLOCK, D_MODEL), BF16),
                        pltpu.VMEM(wg.shape[1:], BF16), pltpu.VMEM(wu.shape[1:], BF16),
                        pltpu.VMEM(wd.shape[1:], BF16),
                        pltpu.SemaphoreType.DMA((2,)), pltpu.SemaphoreType.DMA((2,))],
    )
    return pl.pallas_call(
        _experts_kernel,
        grid_spec=grid_spec,
        out_shape=jax.ShapeDtypeStruct((xs.shape[0] + 2 * EXPERT_BLOCK, D_MODEL), BF16),
        compiler_params=_cparams(("arbitrary",)),
        name="experts",
    )(be, src, dst, nb, xs, wg, wu, wd)


def _combine_kernel(h_ref, ys_ref, meta_ref, p_ref, pg_ref, gw_ref, pw_ref, fg_ref, o_ref, *, final):
    tm = h_ref.shape[0]
    rr = ys_ref.shape[0]
    meta = meta_ref[...]
    d0 = meta[:, 0:1].astype(I32)
    d1 = meta[:, 1:2].astype(I32)
    r_iota = lax.broadcasted_iota(I32, (tm, rr), 1)
    unperm = (jnp.where(r_iota == d0, meta[:, 2:3], 0.0) + jnp.where(r_iota == d1, meta[:, 3:4], 0.0)).astype(BF16)
    h2 = h_ref[...] + _dot(unperm, ys_ref[...])
    gate = _sigmoid(_dot(_rms(h2, pg_ref[...]).astype(BF16), gw_ref[...]))
    h3 = h2 + gate * _dot(p_ref[...].astype(BF16), pw_ref[...])
    if final:
        h3 = _rms(h3, fg_ref[...])
    o_ref[...] = h3


def _combine(h1, ys, meta, p, layer, pg, gw, pw, fg, tm, final):
    t = h1.shape[0]
    nt = t // tm
    rr = _dispatch_rows(tm)
    row = lambda n: pl.BlockSpec((tm, n), lambda i: (i, 0))
    full = lambda a: pl.BlockSpec(a.shape, lambda i: (0,) * a.ndim)
    return pl.pallas_call(
        functools.partial(_combine_kernel, final=final),
        grid=(nt,),
        in_specs=[row(D_MODEL), pl.BlockSpec((rr, D_MODEL), lambda i: (i, 0)), row(LANES),
                  pl.BlockSpec((tm, P_DIM), lambda i: (layer * nt + i, 0)),
                  full(pg), full(gw), full(pw), full(fg)],
        out_specs=row(D_MODEL),
        out_shape=jax.ShapeDtypeStruct((t, D_MODEL), F32),
        compiler_params=_cparams(("parallel",)),
        name="combine",
    )(h1, ys, meta, p, pg, gw, pw, fg)


def _pad_heads(a, axis):
    shape = a.shape
    a = a.reshape(shape[:axis] + (ML_HEADS, ML_HEAD_DIM) + shape[axis + 1:])
    pad = [(0, 0)] * a.ndim
    pad[axis + 1] = (0, LANES - ML_HEAD_DIM)
    a = jnp.pad(a, pad)
    return a.reshape(shape[:axis] + (ML_PAD,) + shape[axis + 1:])


def _layer_params(i, w_in, conv_w, conv_b, igate_b, fgate_b, mnorm_g, sgu_b, w_out, router_gw, router_gb,
                  router_ew, router_eb):
    w = w_in[i]
    s, m = SB_WIDTH, ML_WIDTH
    a_q, a_k, a_v = w[:, 0:s] * (SB_HEAD_DIM ** -0.5 * LOG2_E), w[:, s:2 * s], w[:, 2 * s:3 * s]
    o = 3 * s
    b_q, b_k, b_v, b_o = (w[:, o + k * m:o + (k + 1) * m] for k in range(4))
    o = o + 4 * m
    gates = w[:, o:o + 2 * ML_HEADS]
    c_uv = w[:, o + 2 * ML_HEADS:]
    w_r = jnp.concatenate([a_q, a_k, a_v, _pad_heads(b_q, 1), _pad_heads(b_k, 1), _pad_heads(b_v, 1),
                           _pad_heads(b_o, 1), c_uv,
                           jnp.pad(gates, ((0, 0), (0, LANES - 2 * ML_HEADS)))], axis=1).astype(BF16)
    cw = jnp.concatenate([_pad_heads(conv_w[i][:, :m], 1), _pad_heads(conv_w[i][:, m:], 1)], axis=1)
    cb = jnp.concatenate([_pad_heads(conv_b[i][:m], 0), _pad_heads(conv_b[i][m:], 0)])[None, :]
    gb = jnp.pad(jnp.concatenate([igate_b[i], fgate_b[i]]), (0, LANES - 2 * ML_HEADS))[None, :]
    mg = _pad_heads(mnorm_g[i], 0)[None, :]
    sgu_bias = jnp.repeat(sgu_b[i].T, SGU_GROUP_DIM, axis=1)
    wo = w_out[i]
    wa = wo[:s].astype(BF16)
    wb = _pad_heads(wo[s:s + m], 0).astype(BF16)
    wc = wo[s + m:].astype(BF16)
    rw = jnp.concatenate([router_ew[i].T, router_gw[i].T,
                          jnp.zeros((ROUTER_ROWS - N_EXPERTS - N_GROUPS, D_MODEL), F32)], axis=0)
    rwh = rw.astype(BF16)
    rwl = (rw - rwh.astype(F32)).astype(BF16)
    rb = jnp.concatenate([router_eb[i], router_gb[i],
                          jnp.zeros((ROUTER_ROWS - N_EXPERTS - N_GROUPS,), F32)])[:, None]
    return w_r, cw, cb, gb, mg, sgu_bias, wa, wb, wc, rwh, rwl, rb


def kernel(x, p, norm1_g, w_in, conv_w, conv_b, igate_b, fgate_b, mnorm_g, sb_out_g, sgu_ln_g, sgu_ln_b, sgu_w,
           sgu_b, sgu_out_g, w_out, norm2_g, router_gw, router_gb, router_ew, router_eb, w_gate, w_up, w_down,
           ple_norm_g, ple_gate_w, ple_proj_w, final_g, *, tile=512):
    batch, seq, d = x.shape
    depth = w_in.shape[0]
    t = batch * seq
    tm = min(tile, t)
    nt = t // tm
    rr = _dispatch_rows(tm)
    n_blocks_max = (nt * rr // CHUNK_ROWS) // EXPERT_BLOCK_CHUNKS + N_EXPERTS + 1
    h = x.astype(F32).reshape(t, d)
    p2 = p.reshape(depth * t, P_DIM)
    wg_all = w_gate.reshape((depth * N_EXPERTS,) + w_gate.shape[2:])
    wu_all = w_up.reshape((depth * N_EXPERTS,) + w_up.shape[2:])
    wd_all = w_down.reshape((depth * N_EXPERTS,) + w_down.shape[2:])
    for i in range(depth):
        (w_r, cw, cb, gb, mg, sgu_bias, wa, wb, wc, rwh, rwl, rb) = _layer_params(
            i, w_in, conv_w, conv_b, igate_b, fgate_b, mnorm_g, sgu_b, w_out, router_gw, router_gb,
            router_ew, router_eb)
        za, zqk, zvo, yc, zg = _in_proj(h, norm1_g[i][None, :], w_r, sgu_ln_g[i][None, :], sgu_ln_b[i][None, :],
                                        sgu_w[i], sgu_bias, sgu_out_g[i][None, :], min(2 * tm, seq))
        ya = _sb_attn(za, batch, seq).reshape(t, SB_WIDTH)
        yb = _mlstm(zqk, zvo, zg, cw, cb, gb, mg, batch, seq).reshape(t, ML_PAD)
        h1, xs, meta, cnt = _mix_out(h, ya, yb, yc, sb_out_g[i][None, :], wa, wb, wc, norm2_g[i][None, :],
                                     rwh, rwl, rb, tm)
        be, src, dst, nb = _chunk_tables(cnt[:, :, 0].astype(I32), rr, n_blocks_max)
        ys = _experts(be, src, dst, nb, xs, wg_all, wu_all, wd_all, i)
        h = _combine(h1, ys, meta, p2, i, ple_norm_g[i][None, :], ple_gate_w[i].astype(BF16),
                     ple_proj_w[i].astype(BF16), final_g[None, :], tm, i == depth - 1)
    return h.reshape(batch, seq, d).astype(x.dtype)
```

```python
import functools

import jax
import jax.numpy as jnp
import numpy as np
from jax import lax
from jax.experimental import pallas as pl
from jax.experimental.pallas import tpu as pltpu

F32 = jnp.float32
BF16 = jnp.bfloat16
I32 = jnp.int32

D_MODEL = 1024
P_DIM = 256
EPS = 1e-6
LANES = 128
SB_HEAD_DIM = 64
SB_WIDTH = 384
SB_BLOCK = 128
SB_QTILE = 512
LOG2_E = 1.4426950408889634
ML_HEADS = 4
ML_HEAD_DIM = 96
ML_WIDTH = 384
ML_PAD = ML_HEADS * LANES
ML_CHUNK = 256
CONV_K = 4
SGU_WIDTH = 256
SGU_GROUPS = 4
SGU_GROUP_DIM = 64
SGU_BLOCK = 128
STREAM_CHUNK = 64
N_GROUPS = 4
EXPERTS_PER_GROUP = 8
N_EXPERTS = 32
D_EXPERT = 256
ROUTER_ROWS = 40
CHUNK_ROWS = 16
EXPERT_BLOCK_CHUNKS = 32
EXPERT_BLOCK = CHUNK_ROWS * EXPERT_BLOCK_CHUNKS

COL_A = 0
COL_BQK = 3 * SB_WIDTH
COL_BVO = COL_BQK + 2 * ML_PAD
COL_C = COL_BVO + 2 * ML_PAD
COL_G = COL_C + 2 * SGU_WIDTH
N_Z = COL_G + LANES

VMEM_LIMIT = 56 * 1024 * 1024


def _cparams(sem, vmem=VMEM_LIMIT):
    return pltpu.CompilerParams(dimension_semantics=sem, vmem_limit_bytes=vmem)


def _rms(x, g):
    return x * lax.rsqrt(jnp.mean(x * x, axis=-1, keepdims=True) + EPS) * g


def _sigmoid(x):
    return 0.5 * jnp.tanh(0.5 * x) + 0.5


def _split_bf16(x):
    hi = x.astype(BF16)
    lo = (x - hi.astype(F32)).astype(BF16)
    return hi, lo


def _dot(a, b):
    return jnp.dot(a, b, preferred_element_type=F32)


def _dot_nt(a, b):
    return lax.dot_general(a, b, (((1,), (1,)), ((), ())), preferred_element_type=F32)


def _dot_tn(a, b):
    return lax.dot_general(a, b, (((0,), (0,)), ((), ())), preferred_element_type=F32)


def _dot_split_lhs(x, m):
    hi, lo = _split_bf16(x)
    return _dot(hi, m) + _dot(lo, m)


def _dot_split_rhs(m, x):
    hi, lo = _split_bf16(x)
    return _dot(m, hi) + _dot(m, lo)


def _spatial_gating(zc, lg, lb, w_ref, bias, og):
    W = SGU_WIDTH
    gi = lax.broadcasted_iota(I32, (W, W), 0) // SGU_GROUP_DIM
    gj = lax.broadcasted_iota(I32, (W, W), 1) // SGU_GROUP_DIM
    avg = jnp.where(gi == gj, 1.0 / SGU_GROUP_DIM, 0.0).astype(BF16)
    ti = lax.broadcasted_iota(I32, (SGU_BLOCK, SGU_BLOCK), 0) // STREAM_CHUNK
    si = lax.broadcasted_iota(I32, (SGU_BLOCK, SGU_BLOCK), 1) // STREAM_CHUNK
    chunk_causal = si <= ti
    lane_group = lax.broadcasted_iota(I32, (SGU_BLOCK, W), 1) // SGU_GROUP_DIM
    wg = [jnp.where(chunk_causal, w_ref[g], 0.0).astype(BF16) for g in range(SGU_GROUPS)]
    wpair = [jnp.concatenate(wg[g:g + 2], axis=1) for g in range(0, SGU_GROUPS, 2)]
    out = []
    for r in range(zc.shape[0] // SGU_BLOCK):
        rs = slice(r * SGU_BLOCK, (r + 1) * SGU_BLOCK)
        u = jax.nn.gelu(zc[rs, :W])
        v = jax.nn.gelu(zc[rs, W:])
        mu = _dot_split_lhs(v, avg)
        vc = v - mu
        var = _dot_split_lhs(vc * vc, avg)
        vn = (vc * lax.rsqrt(var + EPS) * lg + lb).astype(BF16)
        mixed = bias
        zero = jnp.zeros_like(vn)
        for p, g in enumerate(range(0, SGU_GROUPS, 2)):
            vpair = jnp.concatenate([jnp.where(lane_group == g, vn, zero),
                                     jnp.where(lane_group == g + 1, vn, zero)], axis=0)
            mixed = mixed + _dot(wpair[p], vpair)
        out.append(_rms(u * mixed, og))
    return jnp.concatenate(out, axis=0)


def _in_proj_kernel(x_ref, g_ref, w_ref, lg_ref, lb_ref, sw_ref, sb_ref, og_ref,
                    za_ref, zqk_ref, zvo_ref, yc_ref, zg_ref):
    hn = _rms(x_ref[...], g_ref[...]).astype(BF16)
    za_ref[...] = _dot(hn, w_ref[:, COL_A:COL_BQK]).astype(BF16)
    zqk_ref[...] = _dot(hn, w_ref[:, COL_BQK:COL_BVO])
    zvo_ref[...] = _dot(hn, w_ref[:, COL_BVO:COL_C]).astype(BF16)
    zg_ref[...] = _dot(hn, w_ref[:, COL_G:N_Z])
    zc = _dot(hn, w_ref[:, COL_C:COL_G])
    yc_ref[...] = _spatial_gating(zc, lg_ref[...], lb_ref[...], sw_ref, sb_ref[...], og_ref[...]).astype(BF16)


def _in_proj(h, g, w, lg, lb, sw, sbias, og, tm):
    t = h.shape[0]
    row = lambda n: pl.BlockSpec((tm, n), lambda i: (i, 0))
    full = lambda a: pl.BlockSpec(a.shape, lambda i: (0,) * a.ndim)
    resident = pl.BlockSpec(w.shape, lambda i: (0, 0), pipeline_mode=pl.Buffered(1))
    return pl.pallas_call(
        _in_proj_kernel,
        grid=(t // tm,),
        in_specs=[row(D_MODEL), full(g), resident, full(lg), full(lb), full(sw), full(sbias), full(og)],
        out_specs=[row(3 * SB_WIDTH), row(2 * ML_PAD), row(2 * ML_PAD), row(SGU_WIDTH), row(LANES)],
        out_shape=[jax.ShapeDtypeStruct((t, 3 * SB_WIDTH), BF16),
                   jax.ShapeDtypeStruct((t, 2 * ML_PAD), F32),
                   jax.ShapeDtypeStruct((t, 2 * ML_PAD), BF16),
                   jax.ShapeDtypeStruct((t, SGU_WIDTH), BF16),
                   jax.ShapeDtypeStruct((t, LANES), F32)],
        compiler_params=_cparams(("parallel",)),
        name="in_proj",
    )(h, g, w, lg, lb, sw, sbias, og)


def _sb_kernel(q_ref, k_ref, v_ref, o_ref, acc_ref, car_ref, pre_ref, tot_ref):
    n = pl.program_id(2)
    blk = SB_BLOCK
    tq = q_ref.shape[0]
    sub = tq // blk
    lane_k = lax.broadcasted_iota(I32, (blk, LANES), 1)
    head0 = lane_k < SB_HEAD_DIM
    mj = lax.broadcasted_iota(I32, (2 * blk, 2 * blk), 0)
    ms = lax.broadcasted_iota(I32, (2 * blk, 2 * blk), 1)
    same_head = (mj // blk) == (ms // blk)
    suffix = jnp.where(same_head & (mj > ms), 1.0, 0.0).astype(BF16)
    totals = jnp.where(same_head, 1.0, 0.0).astype(BF16)

    def per_head(x):
        zero = jnp.zeros_like(x)
        return jnp.concatenate([jnp.where(head0, x, zero), jnp.where(head0, zero, x)], axis=0)

    def block_at(ref, j):
        return per_head(ref[pl.ds(pl.multiple_of(j * blk, blk), blk), :])

    def stage1(j, r0, masked, buf):
        z = _dot_nt(q_ref[r0:, :], block_at(k_ref, j))
        sign = jnp.uint32(0x80000000)
        neg_abs = lax.bitcast_convert_type(lax.bitcast_convert_type(z, jnp.uint32) | sign, F32)
        e = jnp.exp2(neg_abs)
        log_sig = jnp.minimum(z, 0.0) - jnp.log2(1.0 + e)
        log1m = log_sig - z
        if masked:
            mask = (lax.broadcasted_iota(I32, (blk, 2 * blk), 1) % blk
                    < lax.broadcasted_iota(I32, (blk, 2 * blk), 0))
            top_l = jnp.where(mask, log1m[:blk], 0.0)
            top_s = jnp.where(mask, log_sig[:blk], -jnp.inf)
            if tq - r0 == blk:
                log1m, log_sig = top_l, top_s
            else:
                log1m = jnp.concatenate([top_l, log1m[blk:]], axis=0)
                log_sig = jnp.concatenate([top_s, log_sig[blk:]], axis=0)
        terms = log1m.astype(BF16)
        pre_ref[buf, r0:, :] = log_sig + _dot(terms, suffix)
        tot_ref[buf, r0:, :] = _dot(terms, totals)

    def stage2(j, r0, buf):
        arg = pre_ref[buf, r0:, :] + car_ref[r0:, :]
        acc_ref[r0:, :] += _dot(jnp.exp2(arg).astype(BF16), block_at(v_ref, j))
        car_ref[r0:, :] += tot_ref[buf, r0:, :]

    acc_ref[...] = jnp.zeros_like(acc_ref)
    car_ref[...] = jnp.zeros_like(car_ref)

    for kj in reversed(range(sub)):
        stage1(n * sub + kj, kj * blk, True, kj)
    for kj in reversed(range(sub)):
        stage2(n * sub + kj, kj * blk, kj)

    top = n * sub - 1

    @pl.when(n > 0)
    def _():
        stage1(top, 0, False, 0)

    unroll = 4

    def body(i, c):
        j0 = top - unroll * i
        for u in range(unroll):
            nxt = j0 - u - 1
            if u == unroll - 1:
                nxt = jnp.maximum(nxt, 0)
            stage1(nxt, 0, False, (u + 1) % 2)
            stage2(j0 - u, 0, u % 2)
        return c

    lax.fori_loop(0, n * (sub // unroll), body, 0)
    o_ref[...] = acc_ref[...].astype(o_ref.dtype)


def _sb_attn(za, batch, seq):
    za3 = za.reshape(batch, seq, 3 * SB_WIDTH)
    pairs = SB_WIDTH // LANES
    tq = min(SB_QTILE, seq)
    return pl.pallas_call(
        _sb_kernel,
        grid=(batch, pairs, seq // tq),
        in_specs=[pl.BlockSpec((None, tq, LANES), lambda b, p, n: (b, n, p)),
                  pl.BlockSpec((None, seq, LANES), lambda b, p, n: (b, 0, pairs + p)),
                  pl.BlockSpec((None, seq, LANES), lambda b, p, n: (b, 0, 2 * pairs + p))],
        out_specs=pl.BlockSpec((None, tq, LANES), lambda b, p, n: (b, n, p)),
        out_shape=jax.ShapeDtypeStruct((batch, seq, SB_WIDTH), BF16),
        scratch_shapes=[pltpu.VMEM((tq, LANES), F32),
                        pltpu.VMEM((tq, 2 * LANES), F32),
                        pltpu.VMEM((tq // SB_BLOCK, tq, 2 * LANES), F32),
                        pltpu.VMEM((tq // SB_BLOCK, tq, 2 * LANES), F32)],
        compiler_params=_cparams(("parallel", "parallel", "arbitrary")),
        name="sb_attn",
    )(za3, za3, za3)


def _mlstm_kernel(qk_ref, vo_ref, g_ref, cw_ref, cb_ref, gb_ref, mg_ref, o_ref,
                  q_s, k_s, vt_s, cum_s, gmb_s, pm_s, gmbt_s, ct_s, m_s):
    seq = qk_ref.shape[0]
    L = ML_CHUNK
    cw = cw_ref[...]
    cb = cb_ref[...]
    rows = lax.broadcasted_iota(I32, (L, 2 * ML_PAD), 0)

    def conv_silu(r0, taps):
        y = taps[0] * cw[CONV_K - 1:CONV_K, :] + cb
        for d in range(1, CONV_K):
            y = y + taps[d] * cw[CONV_K - 1 - d:CONV_K - d, :]
        y = y * _sigmoid(y)
        q_s[pl.ds(r0, L), :] = y[:, :ML_PAD].astype(BF16)
        k_s[pl.ds(r0, L), :] = (y[:, ML_PAD:] * (ML_HEAD_DIM ** -0.5)).astype(BF16)

    ct_s[...] = jnp.zeros_like(ct_s)
    m_s[...] = jnp.zeros_like(m_s)

    ti = lax.broadcasted_iota(I32, (L, L), 0)
    si = lax.broadcasted_iota(I32, (L, L), 1)
    causal = si <= ti
    tril = jnp.where(causal, 1.0, 0.0).astype(BF16)
    lane = lax.broadcasted_iota(I32, (L, LANES), 1)
    row = lax.broadcasted_iota(I32, (L, LANES), 0)
    ones2 = jnp.full((2 * LANES, LANES), 1.0, BF16)
    gb = gb_ref[...]
    mg = mg_ref[...]
    norm_lane = LANES - 1

    def gates(c):
        r0 = pl.multiple_of(c * L, L)
        g = g_ref[pl.ds(r0, L), :] + gb
        logf = jnp.minimum(g, 0.0) - jnp.log1p(jnp.exp(-jnp.abs(g)))
        cum = pltpu.roll(_dot_split_rhs(tril, logf), LANES - ML_HEADS, axis=1)
        gmb = g - cum
        pm = gmb
        step = 1
        while step < L:
            pm = jnp.maximum(pm, jnp.where(row >= step, pltpu.roll(pm, step, axis=0), -jnp.inf))
            step *= 2
        cum_s[pl.ds(r0, L), :] = cum
        gmb_s[pl.ds(r0, L), :] = gmb
        pm_s[pl.ds(r0, L), :] = pm
        gmbt_s[c] = jnp.transpose(gmb)
        for h in range(ML_HEADS):
            v = vo_ref[pl.ds(r0, L), h * LANES:(h + 1) * LANES].astype(F32)
            vt_s[h, c] = jnp.transpose(jnp.where(lane == norm_lane, 1.0, v)).astype(BF16)

    x0 = qk_ref[0:L, :]
    conv_silu(0, [x0] + [jnp.where(rows >= d, pltpu.roll(x0, d, axis=0), 0.0) for d in range(1, CONV_K)])

    def conv_chunk(c, carry):
        r0 = pl.multiple_of(c * L, L)
        xe = qk_ref[pl.ds(r0 - 8, L + 8), :]
        conv_silu(r0, [xe[8 - d:8 - d + L, :] for d in range(CONV_K)])
        return carry

    lax.fori_loop(1, seq // L, conv_chunk, 0)

    gate_unroll = 4 if (seq // L) % 4 == 0 else 1

    def gate_chunk(c, carry):
        for u in range(gate_unroll):
            gates(gate_unroll * c + u)
        return carry

    lax.fori_loop(0, seq // (gate_unroll * L), gate_chunk, 0)

    heads = range(ML_HEADS)

    def chunk(c, carry):
        r0 = pl.multiple_of(c * L, L)
        cum = cum_s[pl.ds(r0, L), :]
        gmb = gmb_s[pl.ds(r0, L), :]
        pm = pm_s[pl.ds(r0, L), :]
        gmb_t = gmbt_s[c]
        qc = [q_s[pl.ds(r0, L), h * LANES:(h + 1) * LANES] for h in heads]
        kc = [k_s[pl.ds(r0, L), h * LANES:(h + 1) * LANES] for h in heads]
        vt = [vt_s[h, c] for h in heads]
        m_in = [m_s[h][0:1, 0:1] for h in heads]
        ct = [ct_s[h] for h in heads]
        s_raw = [_dot_nt(qc[h], kc[h]) for h in heads]
        mm = [jnp.maximum(m_in[h], pm[:, h:h + 1]) for h in heads]
        inter = [jnp.exp(m_in[h] - mm[h]) for h in heads]
        num = []
        for h in heads:
            wd = jnp.where(causal, jnp.exp(gmb_t[h:h + 1, :] - mm[h]), 0.0)
            lhs = jnp.concatenate([(s_raw[h] * wd).astype(BF16),
                                   (qc[h].astype(F32) * inter[h]).astype(BF16)], axis=1)
            rhs_t = jnp.concatenate([vt[h], ct[h].astype(BF16)], axis=1)
            num.append(_dot_nt(lhs, rhs_t))
        hh = []
        ms = []
        for h in heads:
            den = jnp.maximum(jnp.abs(num[h][:, norm_lane:]), jnp.exp(-(cum[:, h:h + 1] + mm[h])))
            x = jnp.where(lane == norm_lane, 0.0, num[h] * (1.0 / den))
            hi, lo = _split_bf16(x * x)
            hh.append(x)
            ms.append(_dot(jnp.concatenate([hi, lo], axis=1), ones2) * (1.0 / ML_HEAD_DIM))
        for h in heads:
            cs = slice(h * LANES, (h + 1) * LANES)
            oc = vo_ref[pl.ds(r0, L), ML_PAD + h * LANES:ML_PAD + (h + 1) * LANES].astype(F32)
            o_ref[pl.ds(r0, L), cs] = (hh[h] * lax.rsqrt(ms[h] + EPS) * mg[:, cs]
                                       * _sigmoid(oc)).astype(o_ref.dtype)
        for h in heads:
            mm_last = mm[h][L - 1:L, :]
            kw = (kc[h].astype(F32) * jnp.exp(gmb[:, h:h + 1] - mm_last)).astype(BF16)
            ct_s[h] = inter[h][L - 1:L, :] * ct[h] + _dot(vt[h], kw)
            m_s[h] = jnp.broadcast_to(cum[L - 1:L, h:h + 1] + mm_last, (8, LANES))
        return carry

    lax.fori_loop(0, seq // L, chunk, 0)


def _mlstm(zqk, zvo, zg, cw, cb, gb, mg, batch, seq):
    full = lambda a: pl.BlockSpec(a.shape, lambda b: (0,) * a.ndim)
    seq_blk = lambda n: pl.BlockSpec((None, seq, n), lambda b: (b, 0, 0))
    return pl.pallas_call(
        _mlstm_kernel,
        grid=(batch,),
        in_specs=[seq_blk(2 * ML_PAD), seq_blk(2 * ML_PAD), seq_blk(LANES),
                  full(cw), full(cb), full(gb), full(mg)],
        out_specs=seq_blk(ML_PAD),
        out_shape=jax.ShapeDtypeStruct((batch, seq, ML_PAD), BF16),
        scratch_shapes=[pltpu.VMEM((seq, ML_PAD), BF16), pltpu.VMEM((seq, ML_PAD), BF16),
                        pltpu.VMEM((ML_HEADS, seq // ML_CHUNK, LANES, ML_CHUNK), BF16),
                        pltpu.VMEM((seq, LANES), F32), pltpu.VMEM((seq, LANES), F32),
                        pltpu.VMEM((seq, LANES), F32),
                        pltpu.VMEM((seq // ML_CHUNK, LANES, ML_CHUNK), F32),
                        pltpu.VMEM((ML_HEADS, LANES, LANES), F32),
                        pltpu.VMEM((ML_HEADS, 8, LANES), F32)],
        compiler_params=_cparams(("parallel",)),
        name="mlstm",
    )(zqk.reshape(batch, seq, 2 * ML_PAD), zvo.reshape(batch, seq, 2 * ML_PAD),
      zg.reshape(batch, seq, LANES), cw, cb, gb, mg)


def _dispatch_rows(tm):
    return 2 * tm + N_EXPERTS * (CHUNK_ROWS - 1) + (N_EXPERTS * (CHUNK_ROWS - 1)) % CHUNK_ROWS


def _mix_out_kernel(h_ref, ya_ref, yb_ref, yc_ref, ag_ref, wa_ref, wb_ref, wc_ref, n2_ref,
                    rwh_ref, rwl_ref, rb_ref, h1_ref, xs_ref, meta_ref, cnt_ref):
    tm = h_ref.shape[0]
    rr = xs_ref.shape[0]
    ya = _rms(ya_ref[...].astype(F32), ag_ref[...]).astype(BF16)
    h1 = h_ref[...] + _dot(ya, wa_ref[...]) + _dot(yb_ref[...], wb_ref[...]) + _dot(yc_ref[...], wc_ref[...])
    h1_ref[...] = h1
    xn = _rms(h1, n2_ref[...])
    x_hi, x_lo = _split_bf16(xn)
    by_hi = _dot_nt(jnp.concatenate([rwh_ref[...], rwl_ref[...]], axis=0), x_hi)
    logits = by_hi[:ROUTER_ROWS] + by_hi[ROUTER_ROWS:] + _dot_nt(rwh_ref[...], x_lo) + rb_ref[...]
    e_log = logits[:N_EXPERTS]
    g_log = logits[N_EXPERTS:N_EXPERTS + N_GROUPS]
    g_iota = lax.broadcasted_iota(I32, (N_GROUPS, tm), 0)
    g_max = jnp.max(g_log, axis=0, keepdims=True)
    g_top = jnp.min(jnp.where(g_log == g_max, g_iota, N_GROUPS), axis=0, keepdims=True)
    g_w = 1.0 / jnp.sum(jnp.exp(g_log - g_max), axis=0, keepdims=True)
    e_iota = lax.broadcasted_iota(I32, (N_EXPERTS, tm), 0)
    val = jnp.where(e_iota // EXPERTS_PER_GROUP == g_top, e_log, -jnp.inf)
    m1 = jnp.max(val, axis=0, keepdims=True)
    i1 = jnp.min(jnp.where(val == m1, e_iota, N_EXPERTS), axis=0, keepdims=True)
    val2 = jnp.where(e_iota == i1, -jnp.inf, val)
    m2 = jnp.max(val2, axis=0, keepdims=True)
    i2 = jnp.min(jnp.where(val2 == m2, e_iota, N_EXPERTS), axis=0, keepdims=True)
    e2 = jnp.exp(m2 - m1)
    w0 = g_w / (1.0 + e2)
    w1 = g_w * e2 / (1.0 + e2)
    oh0 = e_iota == i1
    oh1 = e_iota == i2
    oh = jnp.concatenate([jnp.where(oh0, 1.0, 0.0), jnp.where(oh1, 1.0, 0.0)], axis=0).astype(BF16)
    ta = lax.broadcasted_iota(I32, (tm, tm), 0)
    tb = lax.broadcasted_iota(I32, (tm, tm), 1)
    before = jnp.where(ta < tb, 1.0, 0.0).astype(BF16)
    rank = _dot(oh, before)
    c0 = jnp.sum(jnp.where(oh0, 1.0, 0.0), axis=1, keepdims=True)
    c1 = jnp.sum(jnp.where(oh1, 1.0, 0.0), axis=1, keepdims=True)
    nchunk = jnp.floor((c0 + c1 + (CHUNK_ROWS - 1)) * (1.0 / CHUNK_ROWS))
    nchunk_b = jnp.broadcast_to(nchunk, (N_EXPERTS, LANES))
    ea = lax.broadcasted_iota(I32, (N_EXPERTS, N_EXPERTS), 0)
    eb = lax.broadcasted_iota(I32, (N_EXPERTS, N_EXPERTS), 1)
    lower = jnp.where(eb < ea, 1.0, 0.0).astype(BF16)
    loc = CHUNK_ROWS * _dot(lower, nchunk_b.astype(BF16))[:, 0:1]
    dest0 = jnp.sum(jnp.where(oh0, loc + rank[:N_EXPERTS], 0.0), axis=0, keepdims=True)
    dest1 = jnp.sum(jnp.where(oh1, loc + c0 + rank[N_EXPERTS:], 0.0), axis=0, keepdims=True)
    r_iota = lax.broadcasted_iota(I32, (rr, tm), 0)
    d0 = dest0.astype(I32)
    d1 = dest1.astype(I32)
    perm = jnp.where(r_iota == d0, 1.0, jnp.where(r_iota == d1, 1.0, 0.0)).astype(BF16)
    xs_ref[...] = _dot(perm, x_hi).astype(BF16)
    cnt_ref[...] = nchunk_b
    m_iota = lax.broadcasted_iota(I32, (8, tm), 0)
    meta = jnp.where(m_iota == 0, dest0, jnp.where(m_iota == 1, dest1, jnp.where(m_iota == 2, w0,
                     jnp.where(m_iota == 3, w1, 0.0))))
    meta = jnp.concatenate([meta, jnp.zeros((LANES - 8, tm), F32)], axis=0)
    meta_ref[...] = jnp.transpose(meta)


def _mix_out(h, ya, yb, yc, ag, wa, wb, wc, n2, rwh, rwl, rb, tm):
    t = h.shape[0]
    nt = t // tm
    rr = _dispatch_rows(tm)
    row = lambda n: pl.BlockSpec((tm, n), lambda i: (i, 0))
    full = lambda a: pl.BlockSpec(a.shape, lambda i: (0,) * a.ndim)
    return pl.pallas_call(
        _mix_out_kernel,
        grid=(nt,),
        in_specs=[row(D_MODEL), row(SB_WIDTH), row(ML_PAD), row(SGU_WIDTH), full(ag), full(wa), full(wb),
                  full(wc), full(n2), full(rwh), full(rwl), full(rb)],
        out_specs=[row(D_MODEL), pl.BlockSpec((rr, D_MODEL), lambda i: (i, 0)), row(LANES),
                   pl.BlockSpec((None, N_EXPERTS, LANES), lambda i: (i, 0, 0))],
        out_shape=[jax.ShapeDtypeStruct((t, D_MODEL), F32),
                   jax.ShapeDtypeStruct((nt * rr, D_MODEL), BF16),
                   jax.ShapeDtypeStruct((t, LANES), F32),
                   jax.ShapeDtypeStruct((nt, N_EXPERTS, LANES), F32)],
        compiler_params=_cparams(("parallel",)),
        name="mix_out",
    )(h, ya, yb, yc, ag, wa, wb, wc, n2, rwh, rwl, rb)


def _chunk_tables(cnt, rr, n_blocks_max):
    nt = cnt.shape[0]
    ne = N_EXPERTS + 1
    tail = rr // CHUNK_ROWS - jnp.sum(cnt, axis=1)
    cnt = jnp.concatenate([cnt, tail[:, None]], axis=1)
    loc = (jnp.cumsum(cnt, axis=1) - cnt) * CHUNK_ROWS
    cnt_e = cnt.T
    cum_e = jnp.cumsum(cnt_e, axis=1)
    total = cum_e[:, -1]
    nblk = (total + EXPERT_BLOCK_CHUNKS - 1) // EXPERT_BLOCK_CHUNKS
    bend = jnp.cumsum(nblk)
    bstart = bend - nblk
    n_blocks = bend[-1]
    i = jnp.arange(n_blocks_max, dtype=I32)
    be = jnp.minimum(jnp.sum(bend[None, :] <= i[:, None], axis=1), ne - 1).astype(I32)
    oh_b = be[:, None] == jnp.arange(ne, dtype=I32)[None, :]
    pick = lambda v: jnp.sum(jnp.where(oh_b, v[None, :], 0), axis=1)
    pick2 = lambda m: jnp.sum(jnp.where(oh_b[:, :, None], m[None, :, :], 0), axis=1)
    q = (i - pick(bstart))[:, None] * EXPERT_BLOCK_CHUNKS + jnp.arange(EXPERT_BLOCK_CHUNKS, dtype=I32)[None, :]
    valid = (q < pick(total)[:, None]) & (i < n_blocks)[:, None]
    cum_b = pick2(cum_e)
    base = jnp.arange(nt, dtype=I32)[None, :] * rr + loc.T - (cum_e - cnt_e) * CHUNK_ROWS
    base_b = pick2(base)
    j = jnp.minimum(jnp.sum(cum_b[:, None, :] <= q[:, :, None], axis=-1), nt - 1)
    oh_j = j[:, :, None] == jnp.arange(nt, dtype=I32)[None, None, :]
    row = jnp.sum(jnp.where(oh_j, base_b[:, None, :], 0), axis=-1) + q * CHUNK_ROWS
    spare = nt * rr + ((i % 2)[:, None] * EXPERT_BLOCK_CHUNKS
                       + jnp.arange(EXPERT_BLOCK_CHUNKS, dtype=I32)[None, :]) * CHUNK_ROWS
    src = jnp.where(valid, row, 0).astype(I32)
    dst = jnp.where(valid, row, spare).astype(I32)
    return be, src.reshape(-1), dst.reshape(-1), n_blocks.reshape(1).astype(I32)


def _experts_kernel(be_ref, src_ref, dst_ref, nb_ref, xs_ref, wg_ref, wu_ref, wd_ref, ys_ref,
                    xbuf, ybuf, wgb, wub, wdb, sem_in, sem_out):
    i = pl.program_id(0)
    nb = nb_ref[0]
    slot = lax.rem(i, 2)

    def is_compute(b):
        return be_ref[b] < N_EXPERTS

    def start_loads(b, sl):
        @pl.when(is_compute(b))
        def _():
            for s in range(EXPERT_BLOCK_CHUNKS):
                src = pl.multiple_of(src_ref[b * EXPERT_BLOCK_CHUNKS + s], CHUNK_ROWS)
                pltpu.make_async_copy(xs_ref.at[pl.ds(src, CHUNK_ROWS), :],
                                      xbuf.at[sl, s * CHUNK_ROWS:(s + 1) * CHUNK_ROWS, :], sem_in.at[sl]).start()

    def wait_loads(b, sl):
        @pl.when(is_compute(b))
        def _():
            pltpu.make_async_copy(xs_ref.at[0:EXPERT_BLOCK, :], xbuf.at[sl], sem_in.at[sl]).wait()

    def start_stores(b, sl):
        for s in range(EXPERT_BLOCK_CHUNKS):
            dst = pl.multiple_of(dst_ref[b * EXPERT_BLOCK_CHUNKS + s], CHUNK_ROWS)
            pltpu.make_async_copy(ybuf.at[sl, s * CHUNK_ROWS:(s + 1) * CHUNK_ROWS, :],
                                  ys_ref.at[pl.ds(dst, CHUNK_ROWS), :], sem_out.at[sl]).start()

    def wait_stores(sl):
        pltpu.make_async_copy(ybuf.at[sl], ys_ref.at[0:EXPERT_BLOCK, :], sem_out.at[sl]).wait()

    @pl.when(i == 0)
    def _():
        ybuf[...] = jnp.zeros(ybuf.shape, BF16)
        real_rows = ys_ref.shape[0] - 2 * EXPERT_BLOCK
        for par in range(2):
            fill = pltpu.make_async_copy(
                ybuf.at[par], ys_ref.at[real_rows + par * EXPERT_BLOCK:real_rows + (par + 1) * EXPERT_BLOCK, :],
                sem_out.at[par])
            fill.start()
            fill.wait()

        start_loads(0, 0)

    @pl.when(i < nb)
    def _():
        @pl.when(i + 1 < nb)
        def _():
            start_loads(i + 1, 1 - slot)

        wait_loads(i, slot)

        @pl.when(i >= 2)
        def _():
            wait_stores(slot)

        @pl.when(is_compute(i))
        def _():
            @pl.when((i == 0) | (be_ref[i] != be_ref[jnp.maximum(i - 1, 0)]))
            def _():
                wgb[...] = wg_ref[...].astype(BF16)
                wub[...] = wu_ref[...].astype(BF16)
                wdb[...] = wd_ref[...].astype(BF16)

            x = xbuf[slot]
            gate = _dot(x, wgb[...])
            up = _dot(x, wub[...])
            mid = (gate * _sigmoid(gate) * up).astype(BF16)
            ybuf[slot] = _dot(mid, wdb[...]).astype(BF16)

        @pl.when(jnp.logical_not(is_compute(i)))
        def _():
            ybuf[slot] = jnp.zeros((EXPERT_BLOCK, D_MODEL), BF16)

        start_stores(i, slot)

        @pl.when(i == nb - 1)
        def _():
            @pl.when(i >= 1)
            def _():
                wait_stores(1 - slot)

            wait_stores(slot)


def _experts(be, src, dst, nb, xs, wg, wu, wd, layer):
    n_blocks_max = be.shape[0]
    wspec = lambda a: pl.BlockSpec(
        (None,) + a.shape[1:],
        lambda i, be, src, dst, nb: (layer * N_EXPERTS + jnp.minimum(be[i], N_EXPERTS - 1), 0, 0))
    grid_spec = pltpu.PrefetchScalarGridSpec(
        num_scalar_prefetch=4,
        grid=(n_blocks_max,),
        in_specs=[pl.BlockSpec(memory_space=pl.ANY), wspec(wg), wspec(wu), wspec(wd)],
        out_specs=pl.BlockSpec(memory_space=pl.ANY),
        scratch_shapes=[pltpu.VMEM((2, EXPERT_BLOCK, D_MODEL), BF16), pltpu.VMEM((2, EXPERT_BLOCK, D_MODEL), BF16),
                        pltpu.VMEM(wg.shape[1:], BF16), pltpu.VMEM(wu.shape[1:], BF16),
                        pltpu.VMEM(wd.shape[1:], BF16),
                        pltpu.SemaphoreType.DMA((2,)), pltpu.SemaphoreType.DMA((2,))],
    )
    return pl.pallas_call(
        _experts_kernel,
        grid_spec=grid_spec,
        out_shape=jax.ShapeDtypeStruct((xs.shape[0] + 2 * EXPERT_BLOCK, D_MODEL), BF16),
        compiler_params=_cparams(("arbitrary",)),
        name="experts",
    )(be, src, dst, nb, xs, wg, wu, wd)


def _combine_kernel(h_ref, ys_ref, meta_ref, p_ref, pg_ref, gw_ref, pw_ref, fg_ref, o_ref, *, final):
    tm = h_ref.shape[0]
    rr = ys_ref.shape[0]
    meta = meta_ref[...]
    d0 = meta[:, 0:1].astype(I32)
    d1 = meta[:, 1:2].astype(I32)
    r_iota = lax.broadcasted_iota(I32, (tm, rr), 1)
    unperm = (jnp.where(r_iota == d0, meta[:, 2:3], 0.0) + jnp.where(r_iota == d1, meta[:, 3:4], 0.0)).astype(BF16)
    h2 = h_ref[...] + _dot(unperm, ys_ref[...])
    gate = _sigmoid(_dot(_rms(h2, pg_ref[...]).astype(BF16), gw_ref[...]))
    h3 = h2 + gate * _dot(p_ref[...].astype(BF16), pw_ref[...])
    if final:
        h3 = _rms(h3, fg_ref[...])
    o_ref[...] = h3


def _combine(h1, ys, meta, p, layer, pg, gw, pw, fg, tm, final):
    t = h1.shape[0]
    nt = t // tm
    rr = _dispatch_rows(tm)
    row = lambda n: pl.BlockSpec((tm, n), lambda i: (i, 0))
    full = lambda a: pl.BlockSpec(a.shape, lambda i: (0,) * a.ndim)
    return pl.pallas_call(
        functools.partial(_combine_kernel, final=final),
        grid=(nt,),
        in_specs=[row(D_MODEL), pl.BlockSpec((rr, D_MODEL), lambda i: (i, 0)), row(LANES),
                  pl.BlockSpec((tm, P_DIM), lambda i: (layer * nt + i, 0)),
                  full(pg), full(gw), full(pw), full(fg)],
        out_specs=row(D_MODEL),
        out_shape=jax.ShapeDtypeStruct((t, D_MODEL), F32),
        compiler_params=_cparams(("parallel",)),
        name="combine",
    )(h1, ys, meta, p, pg, gw, pw, fg)


def _pad_heads(a, axis):
    shape = a.shape
    a = a.reshape(shape[:axis] + (ML_HEADS, ML_HEAD_DIM) + shape[axis + 1:])
    pad = [(0, 0)] * a.ndim
    pad[axis + 1] = (0, LANES - ML_HEAD_DIM)
    a = jnp.pad(a, pad)
    return a.reshape(shape[:axis] + (ML_PAD,) + shape[axis + 1:])


def _layer_params(i, w_in, conv_w, conv_b, igate_b, fgate_b, mnorm_g, sgu_b, w_out, router_gw, router_gb,
                  router_ew, router_eb):
    w = w_in[i]
    s, m = SB_WIDTH, ML_WIDTH
    a_q, a_k, a_v = w[:, 0:s] * (SB_HEAD_DIM ** -0.5 * LOG2_E), w[:, s:2 * s], w[:, 2 * s:3 * s]
    o = 3 * s
    b_q, b_k, b_v, b_o = (w[:, o + k * m:o + (k + 1) * m] for k in range(4))
    o = o + 4 * m
    gates = w[:, o:o + 2 * ML_HEADS]
    c_uv = w[:, o + 2 * ML_HEADS:]
    w_r = jnp.concatenate([a_q, a_k, a_v, _pad_heads(b_q, 1), _pad_heads(b_k, 1), _pad_heads(b_v, 1),
                           _pad_heads(b_o, 1), c_uv,
                           jnp.pad(gates, ((0, 0), (0, LANES - 2 * ML_HEADS)))], axis=1).astype(BF16)
    cw = jnp.concatenate([_pad_heads(conv_w[i][:, :m], 1), _pad_heads(conv_w[i][:, m:], 1)], axis=1)
    cb = jnp.concatenate([_pad_heads(conv_b[i][:m], 0), _pad_heads(conv_b[i][m:], 0)])[None, :]
    gb = jnp.pad(jnp.concatenate([igate_b[i], fgate_b[i]]), (0, LANES - 2 * ML_HEADS))[None, :]
    mg = _pad_heads(mnorm_g[i], 0)[None, :]
    sgu_bias = jnp.repeat(sgu_b[i].T, SGU_GROUP_DIM, axis=1)
    wo = w_out[i]
    wa = wo[:s].astype(BF16)
    wb = _pad_heads(wo[s:s + m], 0).astype(BF16)
    wc = wo[s + m:].astype(BF16)
    rw = jnp.concatenate([router_ew[i].T, router_gw[i].T,
                          jnp.zeros((ROUTER_ROWS - N_EXPERTS - N_GROUPS, D_MODEL), F32)], axis=0)
    rwh = rw.astype(BF16)
    rwl = (rw - rwh.astype(F32)).astype(BF16)
    rb = jnp.concatenate([router_eb[i], router_gb[i],
                          jnp.zeros((ROUTER_ROWS - N_EXPERTS - N_GROUPS,), F32)])[:, None]
    return w_r, cw, cb, gb, mg, sgu_bias, wa, wb, wc, rwh, rwl, rb


def kernel(x, p, norm1_g, w_in, conv_w, conv_b, igate_b, fgate_b, mnorm_g, sb_out_g, sgu_ln_g, sgu_ln_b, sgu_w,
           sgu_b, sgu_out_g, w_out, norm2_g, router_gw, router_gb, router_ew, router_eb, w_gate, w_up, w_down,
           ple_norm_g, ple_gate_w, ple_proj_w, final_g, *, tile=512):
    batch, seq, d = x.shape
    depth = w_in.shape[0]
    t = batch * seq
    tm = min(tile, t)
    nt = t // tm
    rr = _dispatch_rows(tm)
    n_blocks_max = (nt * rr // CHUNK_ROWS) // EXPERT_BLOCK_CHUNKS + N_EXPERTS + 1
    h = x.astype(F32).reshape(t, d)
    p2 = p.reshape(depth * t, P_DIM)
    wg_all = w_gate.reshape((depth * N_EXPERTS,) + w_gate.shape[2:])
    wu_all = w_up.reshape((depth * N_EXPERTS,) + w_up.shape[2:])
    wd_all = w_down.reshape((depth * N_EXPERTS,) + w_down.shape[2:])
    for i in range(depth):
        (w_r, cw, cb, gb, mg, sgu_bias, wa, wb, wc, rwh, rwl, rb) = _layer_params(
            i, w_in, conv_w, conv_b, igate_b, fgate_b, mnorm_g, sgu_b, w_out, router_gw, router_gb,
            router_ew, router_eb)
        za, zqk, zvo, yc, zg = _in_proj(h, norm1_g[i][None, :], w_r, sgu_ln_g[i][None, :], sgu_ln_b[i][None, :],
                                        sgu_w[i], sgu_bias, sgu_out_g[i][None, :], min(2 * tm, seq))
        ya = _sb_attn(za, batch, seq).reshape(t, SB_WIDTH)
        yb = _mlstm(zqk, zvo, zg, cw, cb, gb, mg, batch, seq).reshape(t, ML_PAD)
        h1, xs, meta, cnt = _mix_out(h, ya, yb, yc, sb_out_g[i][None, :], wa, wb, wc, norm2_g[i][None, :],
                                     rwh, rwl, rb, tm)
        be, src, dst, nb = _chunk_tables(cnt[:, :, 0].astype(I32), rr, n_blocks_max)
        ys = _experts(be, src, dst, nb, xs, wg_all, wu_all, wd_all, i)
        h = _combine(h1, ys, meta, p2, i, ple_norm_g[i][None, :], ple_gate_w[i].astype(BF16),
                     ple_proj_w[i].astype(BF16), final_g[None, :], tm, i == depth - 1)
    return h.reshape(batch, seq, d).astype(x.dtype)
```

```python
import functools

import jax
import jax.numpy as jnp
import numpy as np
from jax import lax
from jax.experimental import pallas as pl
from jax.experimental.pallas import tpu as pltpu

F32 = jnp.float32
BF16 = jnp.bfloat16
I32 = jnp.int32

D_MODEL = 1024
P_DIM = 256
EPS = 1e-6
LANES = 128
SB_HEAD_DIM = 64
SB_WIDTH = 384
SB_BLOCK = 128
SB_QTILE = 512
LOG2_E = 1.4426950408889634
ML_HEADS = 4
ML_HEAD_DIM = 96
ML_WIDTH = 384
ML_PAD = ML_HEADS * LANES
ML_CHUNK = 512
CONV_K = 4
SGU_WIDTH = 256
SGU_GROUPS = 4
SGU_GROUP_DIM = 64
SGU_BLOCK = 128
STREAM_CHUNK = 64
N_GROUPS = 4
EXPERTS_PER_GROUP = 8
N_EXPERTS = 32
D_EXPERT = 256
ROUTER_ROWS = 40
CHUNK_ROWS = 16
EXPERT_BLOCK_CHUNKS = 32
EXPERT_BLOCK = CHUNK_ROWS * EXPERT_BLOCK_CHUNKS

COL_A = 0
COL_BQK = 3 * SB_WIDTH
COL_BVO = COL_BQK + 2 * ML_PAD
COL_C = COL_BVO + 2 * ML_PAD
COL_G = COL_C + 2 * SGU_WIDTH
N_Z = COL_G + LANES

VMEM_LIMIT = 56 * 1024 * 1024


def _cparams(sem, vmem=VMEM_LIMIT):
    return pltpu.CompilerParams(dimension_semantics=sem, vmem_limit_bytes=vmem)


def _rms(x, g):
    return x * lax.rsqrt(jnp.mean(x * x, axis=-1, keepdims=True) + EPS) * g


def _sigmoid(x):
    return 0.5 * jnp.tanh(0.5 * x) + 0.5


def _split_bf16(x):
    hi = x.astype(BF16)
    lo = (x - hi.astype(F32)).astype(BF16)
    return hi, lo


def _dot(a, b):
    return jnp.dot(a, b, preferred_element_type=F32)


def _dot_nt(a, b):
    return lax.dot_general(a, b, (((1,), (1,)), ((), ())), preferred_element_type=F32)


def _dot_tn(a, b):
    return lax.dot_general(a, b, (((0,), (0,)), ((), ())), preferred_element_type=F32)


def _dot_split_lhs(x, m):
    hi, lo = _split_bf16(x)
    return _dot(hi, m) + _dot(lo, m)


def _dot_split_rhs(m, x):
    hi, lo = _split_bf16(x)
    return _dot(m, hi) + _dot(m, lo)


def _spatial_gating(zc, lg, lb, w_ref, bias, og):
    W = SGU_WIDTH
    gi = lax.broadcasted_iota(I32, (W, W), 0) // SGU_GROUP_DIM
    gj = lax.broadcasted_iota(I32, (W, W), 1) // SGU_GROUP_DIM
    avg = jnp.where(gi == gj, 1.0 / SGU_GROUP_DIM, 0.0).astype(BF16)
    ti = lax.broadcasted_iota(I32, (SGU_BLOCK, SGU_BLOCK), 0) // STREAM_CHUNK
    si = lax.broadcasted_iota(I32, (SGU_BLOCK, SGU_BLOCK), 1) // STREAM_CHUNK
    chunk_causal = si <= ti
    lane_group = lax.broadcasted_iota(I32, (SGU_BLOCK, W), 1) // SGU_GROUP_DIM
    wg = [jnp.where(chunk_causal, w_ref[g], 0.0).astype(BF16) for g in range(SGU_GROUPS)]
    wpair = [jnp.concatenate(wg[g:g + 2], axis=1) for g in range(0, SGU_GROUPS, 2)]
    out = []
    for r in range(zc.shape[0] // SGU_BLOCK):
        rs = slice(r * SGU_BLOCK, (r + 1) * SGU_BLOCK)
        u = jax.nn.gelu(zc[rs, :W])
        v = jax.nn.gelu(zc[rs, W:])
        mu = _dot_split_lhs(v, avg)
        vc = v - mu
        var = _dot_split_lhs(vc * vc, avg)
        vn = (vc * lax.rsqrt(var + EPS) * lg + lb).astype(BF16)
        mixed = bias
        zero = jnp.zeros_like(vn)
        for p, g in enumerate(range(0, SGU_GROUPS, 2)):
            vpair = jnp.concatenate([jnp.where(lane_group == g, vn, zero),
                                     jnp.where(lane_group == g + 1, vn, zero)], axis=0)
            mixed = mixed + _dot(wpair[p], vpair)
        out.append(_rms(u * mixed, og))
    return jnp.concatenate(out, axis=0)


def _in_proj_kernel(x_ref, g_ref, w_ref, lg_ref, lb_ref, sw_ref, sb_ref, og_ref,
                    za_ref, zqk_ref, zvo_ref, yc_ref, zg_ref):
    hn = _rms(x_ref[...], g_ref[...]).astype(BF16)
    za_ref[...] = _dot(hn, w_ref[:, COL_A:COL_BQK]).astype(BF16)
    zqk_ref[...] = _dot(hn, w_ref[:, COL_BQK:COL_BVO])
    zvo_ref[...] = _dot(hn, w_ref[:, COL_BVO:COL_C]).astype(BF16)
    zg_ref[...] = _dot(hn, w_ref[:, COL_G:N_Z])
    zc = _dot(hn, w_ref[:, COL_C:COL_G])
    yc_ref[...] = _spatial_gating(zc, lg_ref[...], lb_ref[...], sw_ref, sb_ref[...], og_ref[...]).astype(BF16)


def _in_proj(h, g, w, lg, lb, sw, sbias, og, tm):
    t = h.shape[0]
    row = lambda n: pl.BlockSpec((tm, n), lambda i: (i, 0))
    full = lambda a: pl.BlockSpec(a.shape, lambda i: (0,) * a.ndim)
    resident = pl.BlockSpec(w.shape, lambda i: (0, 0), pipeline_mode=pl.Buffered(1))
    return pl.pallas_call(
        _in_proj_kernel,
        grid=(t // tm,),
        in_specs=[row(D_MODEL), full(g), resident, full(lg), full(lb), full(sw), full(sbias), full(og)],
        out_specs=[row(3 * SB_WIDTH), row(2 * ML_PAD), row(2 * ML_PAD), row(SGU_WIDTH), row(LANES)],
        out_shape=[jax.ShapeDtypeStruct((t, 3 * SB_WIDTH), BF16),
                   jax.ShapeDtypeStruct((t, 2 * ML_PAD), F32),
                   jax.ShapeDtypeStruct((t, 2 * ML_PAD), BF16),
                   jax.ShapeDtypeStruct((t, SGU_WIDTH), BF16),
                   jax.ShapeDtypeStruct((t, LANES), F32)],
        compiler_params=_cparams(("parallel",)),
        name="in_proj",
    )(h, g, w, lg, lb, sw, sbias, og)


def _sb_kernel(q_ref, k_ref, v_ref, o_ref, acc_ref, car_ref, pre_ref, tot_ref):
    n = pl.program_id(2)
    blk = SB_BLOCK
    tq = q_ref.shape[0]
    sub = tq // blk
    lane_k = lax.broadcasted_iota(I32, (blk, LANES), 1)
    head0 = lane_k < SB_HEAD_DIM
    mj = lax.broadcasted_iota(I32, (2 * blk, 2 * blk), 0)
    ms = lax.broadcasted_iota(I32, (2 * blk, 2 * blk), 1)
    same_head = (mj // blk) == (ms // blk)
    suffix = jnp.where(same_head & (mj > ms), 1.0, 0.0).astype(BF16)
    totals = jnp.where(same_head, 1.0, 0.0).astype(BF16)

    def per_head(x):
        zero = jnp.zeros_like(x)
        return jnp.concatenate([jnp.where(head0, x, zero), jnp.where(head0, zero, x)], axis=0)

    def block_at(ref, j):
        return per_head(ref[pl.ds(pl.multiple_of(j * blk, blk), blk), :])

    def stage1(j, r0, masked, buf):
        z = _dot_nt(q_ref[r0:, :], block_at(k_ref, j))
        sign = jnp.uint32(0x80000000)
        neg_abs = lax.bitcast_convert_type(lax.bitcast_convert_type(z, jnp.uint32) | sign, F32)
        e = jnp.exp2(neg_abs)
        log_sig = jnp.minimum(z, 0.0) - jnp.log2(1.0 + e)
        log1m = log_sig - z
        if masked:
            mask = (lax.broadcasted_iota(I32, (blk, 2 * blk), 1) % blk
                    < lax.broadcasted_iota(I32, (blk, 2 * blk), 0))
            top_l = jnp.where(mask, log1m[:blk], 0.0)
            top_s = jnp.where(mask, log_sig[:blk], -jnp.inf)
            if tq - r0 == blk:
                log1m, log_sig = top_l, top_s
            else:
                log1m = jnp.concatenate([top_l, log1m[blk:]], axis=0)
                log_sig = jnp.concatenate([top_s, log_sig[blk:]], axis=0)
        terms = log1m.astype(BF16)
        pre_ref[buf, r0:, :] = log_sig + _dot(terms, suffix)
        tot_ref[buf, r0:, :] = _dot(terms, totals)

    def stage2(j, r0, buf):
        arg = pre_ref[buf, r0:, :] + car_ref[r0:, :]
        acc_ref[r0:, :] += _dot(jnp.exp2(arg).astype(BF16), block_at(v_ref, j))
        car_ref[r0:, :] += tot_ref[buf, r0:, :]

    acc_ref[...] = jnp.zeros_like(acc_ref)
    car_ref[...] = jnp.zeros_like(car_ref)

    for kj in reversed(range(sub)):
        stage1(n * sub + kj, kj * blk, True, kj)
    for kj in reversed(range(sub)):
        stage2(n * sub + kj, kj * blk, kj)

    top = n * sub - 1

    @pl.when(n > 0)
    def _():
        stage1(top, 0, False, 0)

    unroll = 4

    def body(i, c):
        j0 = top - unroll * i
        for u in range(unroll):
            nxt = j0 - u - 1
            if u == unroll - 1:
                nxt = jnp.maximum(nxt, 0)
            stage1(nxt, 0, False, (u + 1) % 2)
            stage2(j0 - u, 0, u % 2)
        return c

    lax.fori_loop(0, n * (sub // unroll), body, 0)
    o_ref[...] = acc_ref[...].astype(o_ref.dtype)


def _sb_attn(za, batch, seq):
    za3 = za.reshape(batch, seq, 3 * SB_WIDTH)
    pairs = SB_WIDTH // LANES
    tq = min(SB_QTILE, seq)
    return pl.pallas_call(
        _sb_kernel,
        grid=(batch, pairs, seq // tq),
        in_specs=[pl.BlockSpec((None, tq, LANES), lambda b, p, n: (b, n, p)),
                  pl.BlockSpec((None, seq, LANES), lambda b, p, n: (b, 0, pairs + p)),
                  pl.BlockSpec((None, seq, LANES), lambda b, p, n: (b, 0, 2 * pairs + p))],
        out_specs=pl.BlockSpec((None, tq, LANES), lambda b, p, n: (b, n, p)),
        out_shape=jax.ShapeDtypeStruct((batch, seq, SB_WIDTH), BF16),
        scratch_shapes=[pltpu.VMEM((tq, LANES), F32),
                        pltpu.VMEM((tq, 2 * LANES), F32),
                        pltpu.VMEM((tq // SB_BLOCK, tq, 2 * LANES), F32),
                        pltpu.VMEM((tq // SB_BLOCK, tq, 2 * LANES), F32)],
        compiler_params=_cparams(("parallel", "parallel", "arbitrary")),
        name="sb_attn",
    )(za3, za3, za3)


def _mlstm_kernel(qk_ref, vo_ref, g_ref, cw_ref, cb_ref, gb_ref, mg_ref, o_ref,
                  q_s, k_s, vt_s, cum_s, gmb_s, pm_s, gmbt_s, ct_s, m_s):
    seq = qk_ref.shape[0]
    L = ML_CHUNK
    cw = cw_ref[...]
    cb = cb_ref[...]
    rows = lax.broadcasted_iota(I32, (L, 2 * ML_PAD), 0)

    def conv_silu(r0, taps):
        y = taps[0] * cw[CONV_K - 1:CONV_K, :] + cb
        for d in range(1, CONV_K):
            y = y + taps[d] * cw[CONV_K - 1 - d:CONV_K - d, :]
        y = y * _sigmoid(y)
        q_s[pl.ds(r0, L), :] = y[:, :ML_PAD].astype(BF16)
        k_s[pl.ds(r0, L), :] = (y[:, ML_PAD:] * (ML_HEAD_DIM ** -0.5)).astype(BF16)

    ct_s[...] = jnp.zeros_like(ct_s)
    m_s[...] = jnp.zeros_like(m_s)

    ti = lax.broadcasted_iota(I32, (L, L), 0)
    si = lax.broadcasted_iota(I32, (L, L), 1)
    causal = si <= ti
    tril = jnp.where(causal, 1.0, 0.0).astype(BF16)
    lane = lax.broadcasted_iota(I32, (L, LANES), 1)
    row = lax.broadcasted_iota(I32, (L, LANES), 0)
    ones2 = jnp.full((2 * LANES, LANES), 1.0, BF16)
    gb = gb_ref[...]
    mg = mg_ref[...]
    norm_lane = LANES - 1

    def gates(c):
        r0 = pl.multiple_of(c * L, L)
        g = g_ref[pl.ds(r0, L), :] + gb
        logf = jnp.minimum(g, 0.0) - jnp.log1p(jnp.exp(-jnp.abs(g)))
        cum = pltpu.roll(_dot_split_rhs(tril, logf), LANES - ML_HEADS, axis=1)
        gmb = g - cum
        pm = gmb
        step = 1
        while step < L:
            pm = jnp.maximum(pm, jnp.where(row >= step, pltpu.roll(pm, step, axis=0), -jnp.inf))
            step *= 2
        cum_s[pl.ds(r0, L), :] = cum
        gmb_s[pl.ds(r0, L), :] = gmb
        pm_s[pl.ds(r0, L), :] = pm
        gmbt_s[c] = jnp.transpose(gmb)
        for h in range(ML_HEADS):
            v = vo_ref[pl.ds(r0, L), h * LANES:(h + 1) * LANES].astype(F32)
            vt_s[h, c] = jnp.transpose(jnp.where(lane == norm_lane, 1.0, v)).astype(BF16)

    x0 = qk_ref[0:L, :]
    conv_silu(0, [x0] + [jnp.where(rows >= d, pltpu.roll(x0, d, axis=0), 0.0) for d in range(1, CONV_K)])

    def conv_chunk(c, carry):
        r0 = pl.multiple_of(c * L, L)
        xe = qk_ref[pl.ds(r0 - 8, L + 8), :]
        conv_silu(r0, [xe[8 - d:8 - d + L, :] for d in range(CONV_K)])
        return carry

    lax.fori_loop(1, seq // L, conv_chunk, 0)

    gate_unroll = 4 if (seq // L) % 4 == 0 else 1

    def gate_chunk(c, carry):
        for u in range(gate_unroll):
            gates(gate_unroll * c + u)
        return carry

    lax.fori_loop(0, seq // (gate_unroll * L), gate_chunk, 0)

    heads = range(ML_HEADS)

    def chunk(c, carry):
        r0 = pl.multiple_of(c * L, L)
        cum = cum_s[pl.ds(r0, L), :]
        gmb = gmb_s[pl.ds(r0, L), :]
        pm = pm_s[pl.ds(r0, L), :]
        gmb_t = gmbt_s[c]
        qc = [q_s[pl.ds(r0, L), h * LANES:(h + 1) * LANES] for h in heads]
        kc = [k_s[pl.ds(r0, L), h * LANES:(h + 1) * LANES] for h in heads]
        vt = [vt_s[h, c] for h in heads]
        m_in = [m_s[h][0:1, 0:1] for h in heads]
        ct = [ct_s[h] for h in heads]
        s_raw = [_dot_nt(qc[h], kc[h]) for h in heads]
        mm = [jnp.maximum(m_in[h], pm[:, h:h + 1]) for h in heads]
        inter = [jnp.exp(m_in[h] - mm[h]) for h in heads]
        num = []
        for h in heads:
            wd = jnp.where(causal, jnp.exp(gmb_t[h:h + 1, :] - mm[h]), 0.0)
            lhs = jnp.concatenate([(s_raw[h] * wd).astype(BF16),
                                   (qc[h].astype(F32) * inter[h]).astype(BF16)], axis=1)
            rhs_t = jnp.concatenate([vt[h], ct[h].astype(BF16)], axis=1)
            num.append(_dot_nt(lhs, rhs_t))
        hh = []
        ms = []
        for h in heads:
            den = jnp.maximum(jnp.abs(num[h][:, norm_lane:]), jnp.exp(-(cum[:, h:h + 1] + mm[h])))
            x = jnp.where(lane == norm_lane, 0.0, num[h] * (1.0 / den))
            hi, lo = _split_bf16(x * x)
            hh.append(x)
            ms.append(_dot(jnp.concatenate([hi, lo], axis=1), ones2) * (1.0 / ML_HEAD_DIM))
        for h in heads:
            cs = slice(h * LANES, (h + 1) * LANES)
            oc = vo_ref[pl.ds(r0, L), ML_PAD + h * LANES:ML_PAD + (h + 1) * LANES].astype(F32)
            o_ref[pl.ds(r0, L), cs] = (hh[h] * lax.rsqrt(ms[h] + EPS) * mg[:, cs]
                                       * _sigmoid(oc)).astype(o_ref.dtype)
        for h in heads:
            mm_last = mm[h][L - 1:L, :]
            kw = (kc[h].astype(F32) * jnp.exp(gmb[:, h:h + 1] - mm_last)).astype(BF16)
            ct_s[h] = inter[h][L - 1:L, :] * ct[h] + _dot(vt[h], kw)
            m_s[h] = jnp.broadcast_to(cum[L - 1:L, h:h + 1] + mm_last, (8, LANES))
        return carry

    lax.fori_loop(0, seq // L, chunk, 0)


def _mlstm(zqk, zvo, zg, cw, cb, gb, mg, batch, seq):
    full = lambda a: pl.BlockSpec(a.shape, lambda b: (0,) * a.ndim)
    seq_blk = lambda n: pl.BlockSpec((None, seq, n), lambda b: (b, 0, 0))
    return pl.pallas_call(
        _mlstm_kernel,
        grid=(batch,),
        in_specs=[seq_blk(2 * ML_PAD), seq_blk(2 * ML_PAD), seq_blk(LANES),
                  full(cw), full(cb), full(gb), full(mg)],
        out_specs=seq_blk(ML_PAD),
        out_shape=jax.ShapeDtypeStruct((batch, seq, ML_PAD), BF16),
        scratch_shapes=[pltpu.VMEM((seq, ML_PAD), BF16), pltpu.VMEM((seq, ML_PAD), BF16),
                        pltpu.VMEM((ML_HEADS, seq // ML_CHUNK, LANES, ML_CHUNK), BF16),
                        pltpu.VMEM((seq, LANES), F32), pltpu.VMEM((seq, LANES), F32),
                        pltpu.VMEM((seq, LANES), F32),
                        pltpu.VMEM((seq // ML_CHUNK, LANES, ML_CHUNK), F32),
                        pltpu.VMEM((ML_HEADS, LANES, LANES), F32),
                        pltpu.VMEM((ML_HEADS, 8, LANES), F32)],
        compiler_params=_cparams(("parallel",)),
        name="mlstm",
    )(zqk.reshape(batch, seq, 2 * ML_PAD), zvo.reshape(batch, seq, 2 * ML_PAD),
      zg.reshape(batch, seq, LANES), cw, cb, gb, mg)


def _dispatch_rows(tm):
    return 2 * tm + N_EXPERTS * (CHUNK_ROWS - 1) + (N_EXPERTS * (CHUNK_ROWS - 1)) % CHUNK_ROWS


def _mix_out_kernel(h_ref, ya_ref, yb_ref, yc_ref, ag_ref, wa_ref, wb_ref, wc_ref, n2_ref,
                    rwh_ref, rwl_ref, rb_ref, h1_ref, xs_ref, meta_ref, cnt_ref):
    tm = h_ref.shape[0]
    rr = xs_ref.shape[0]
    ya = _rms(ya_ref[...].astype(F32), ag_ref[...]).astype(BF16)
    h1 = h_ref[...] + _dot(ya, wa_ref[...]) + _dot(yb_ref[...], wb_ref[...]) + _dot(yc_ref[...], wc_ref[...])
    h1_ref[...] = h1
    xn = _rms(h1, n2_ref[...])
    x_hi, x_lo = _split_bf16(xn)
    by_hi = _dot_nt(jnp.concatenate([rwh_ref[...], rwl_ref[...]], axis=0), x_hi)
    logits = by_hi[:ROUTER_ROWS] + by_hi[ROUTER_ROWS:] + _dot_nt(rwh_ref[...], x_lo) + rb_ref[...]
    e_log = logits[:N_EXPERTS]
    g_log = logits[N_EXPERTS:N_EXPERTS + N_GROUPS]
    g_iota = lax.broadcasted_iota(I32, (N_GROUPS, tm), 0)
    g_max = jnp.max(g_log, axis=0, keepdims=True)
    g_top = jnp.min(jnp.where(g_log == g_max, g_iota, N_GROUPS), axis=0, keepdims=True)
    g_w = 1.0 / jnp.sum(jnp.exp(g_log - g_max), axis=0, keepdims=True)
    e_iota = lax.broadcasted_iota(I32, (N_EXPERTS, tm), 0)
    val = jnp.where(e_iota // EXPERTS_PER_GROUP == g_top, e_log, -jnp.inf)
    m1 = jnp.max(val, axis=0, keepdims=True)
    i1 = jnp.min(jnp.where(val == m1, e_iota, N_EXPERTS), axis=0, keepdims=True)
    val2 = jnp.where(e_iota == i1, -jnp.inf, val)
    m2 = jnp.max(val2, axis=0, keepdims=True)
    i2 = jnp.min(jnp.where(val2 == m2, e_iota, N_EXPERTS), axis=0, keepdims=True)
    e2 = jnp.exp(m2 - m1)
    w0 = g_w / (1.0 + e2)
    w1 = g_w * e2 / (1.0 + e2)
    oh0 = e_iota == i1
    oh1 = e_iota == i2
    oh = jnp.concatenate([jnp.where(oh0, 1.0, 0.0), jnp.where(oh1, 1.0, 0.0)], axis=0).astype(BF16)
    ta = lax.broadcasted_iota(I32, (tm, tm), 0)
    tb = lax.broadcasted_iota(I32, (tm, tm), 1)
    before = jnp.where(ta < tb, 1.0, 0.0).astype(BF16)
    rank = _dot(oh, before)
    c0 = jnp.sum(jnp.where(oh0, 1.0, 0.0), axis=1, keepdims=True)
    c1 = jnp.sum(jnp.where(oh1, 1.0, 0.0), axis=1, keepdims=True)
    nchunk = jnp.floor((c0 + c1 + (CHUNK_ROWS - 1)) * (1.0 / CHUNK_ROWS))
    nchunk_b = jnp.broadcast_to(nchunk, (N_EXPERTS, LANES))
    ea = lax.broadcasted_iota(I32, (N_EXPERTS, N_EXPERTS), 0)
    eb = lax.broadcasted_iota(I32, (N_EXPERTS, N_EXPERTS), 1)
    lower = jnp.where(eb < ea, 1.0, 0.0).astype(BF16)
    loc = CHUNK_ROWS * _dot(lower, nchunk_b.astype(BF16))[:, 0:1]
    dest0 = jnp.sum(jnp.where(oh0, loc + rank[:N_EXPERTS], 0.0), axis=0, keepdims=True)
    dest1 = jnp.sum(jnp.where(oh1, loc + c0 + rank[N_EXPERTS:], 0.0), axis=0, keepdims=True)
    r_iota = lax.broadcasted_iota(I32, (rr, tm), 0)
    d0 = dest0.astype(I32)
    d1 = dest1.astype(I32)
    perm = jnp.where(r_iota == d0, 1.0, jnp.where(r_iota == d1, 1.0, 0.0)).astype(BF16)
    xs_ref[...] = _dot(perm, x_hi).astype(BF16)
    cnt_ref[...] = nchunk_b
    m_iota = lax.broadcasted_iota(I32, (8, tm), 0)
    meta = jnp.where(m_iota == 0, dest0, jnp.where(m_iota == 1, dest1, jnp.where(m_iota == 2, w0,
                     jnp.where(m_iota == 3, w1, 0.0))))
    meta = jnp.concatenate([meta, jnp.zeros((LANES - 8, tm), F32)], axis=0)
    meta_ref[...] = jnp.transpose(meta)


def _mix_out(h, ya, yb, yc, ag, wa, wb, wc, n2, rwh, rwl, rb, tm):
    t = h.shape[0]
    nt = t // tm
    rr = _dispatch_rows(tm)
    row = lambda n: pl.BlockSpec((tm, n), lambda i: (i, 0))
    full = lambda a: pl.BlockSpec(a.shape, lambda i: (0,) * a.ndim)
    return pl.pallas_call(
        _mix_out_kernel,
        grid=(nt,),
        in_specs=[row(D_MODEL), row(SB_WIDTH), row(ML_PAD), row(SGU_WIDTH), full(ag), full(wa), full(wb),
                  full(wc), full(n2), full(rwh), full(rwl), full(rb)],
        out_specs=[row(D_MODEL), pl.BlockSpec((rr, D_MODEL), lambda i: (i, 0)), row(LANES),
                   pl.BlockSpec((None, N_EXPERTS, LANES), lambda i: (i, 0, 0))],
        out_shape=[jax.ShapeDtypeStruct((t, D_MODEL), F32),
                   jax.ShapeDtypeStruct((nt * rr, D_MODEL), BF16),
                   jax.ShapeDtypeStruct((t, LANES), F32),
                   jax.ShapeDtypeStruct((nt, N_EXPERTS, LANES), F32)],
        compiler_params=_cparams(("parallel",)),
        name="mix_out",
    )(h, ya, yb, yc, ag, wa, wb, wc, n2, rwh, rwl, rb)


def _chunk_tables(cnt, rr, n_blocks_max):
    nt = cnt.shape[0]
    ne = N_EXPERTS + 1
    tail = rr // CHUNK_ROWS - jnp.sum(cnt, axis=1)
    cnt = jnp.concatenate([cnt, tail[:, None]], axis=1)
    loc = (jnp.cumsum(cnt, axis=1) - cnt) * CHUNK_ROWS
    cnt_e = cnt.T
    cum_e = jnp.cumsum(cnt_e, axis=1)
    total = cum_e[:, -1]
    nblk = (total + EXPERT_BLOCK_CHUNKS - 1) // EXPERT_BLOCK_CHUNKS
    bend = jnp.cumsum(nblk)
    bstart = bend - nblk
    n_blocks = bend[-1]
    i = jnp.arange(n_blocks_max, dtype=I32)
    be = jnp.minimum(jnp.sum(bend[None, :] <= i[:, None], axis=1), ne - 1).astype(I32)
    oh_b = be[:, None] == jnp.arange(ne, dtype=I32)[None, :]
    pick = lambda v: jnp.sum(jnp.where(oh_b, v[None, :], 0), axis=1)
    pick2 = lambda m: jnp.sum(jnp.where(oh_b[:, :, None], m[None, :, :], 0), axis=1)
    q = (i - pick(bstart))[:, None] * EXPERT_BLOCK_CHUNKS + jnp.arange(EXPERT_BLOCK_CHUNKS, dtype=I32)[None, :]
    valid = (q < pick(total)[:, None]) & (i < n_blocks)[:, None]
    cum_b = pick2(cum_e)
    base = jnp.arange(nt, dtype=I32)[None, :] * rr + loc.T - (cum_e - cnt_e) * CHUNK_ROWS
    base_b = pick2(base)
    j = jnp.minimum(jnp.sum(cum_b[:, None, :] <= q[:, :, None], axis=-1), nt - 1)
    oh_j = j[:, :, None] == jnp.arange(nt, dtype=I32)[None, None, :]
    row = jnp.sum(jnp.where(oh_j, base_b[:, None, :], 0), axis=-1) + q * CHUNK_ROWS
    spare = nt * rr + ((i % 2)[:, None] * EXPERT_BLOCK_CHUNKS
                       + jnp.arange(EXPERT_BLOCK_CHUNKS, dtype=I32)[None, :]) * CHUNK_ROWS
    src = jnp.where(valid, row, 0).astype(I32)
    dst = jnp.where(valid, row, spare).astype(I32)
    return be, src.reshape(-1), dst.reshape(-1), n_blocks.reshape(1).astype(I32)


def _experts_kernel(be_ref, src_ref, dst_ref, nb_ref, xs_ref, wg_ref, wu_ref, wd_ref, ys_ref,
                    xbuf, ybuf, wgb, wub, wdb, sem_in, sem_out):
    i = pl.program_id(0)
    nb = nb_ref[0]
    slot = lax.rem(i, 2)

    def is_compute(b):
        return be_ref[b] < N_EXPERTS

    def start_loads(b, sl):
        @pl.when(is_compute(b))
        def _():
            for s in range(EXPERT_BLOCK_CHUNKS):
                src = pl.multiple_of(src_ref[b * EXPERT_BLOCK_CHUNKS + s], CHUNK_ROWS)
                pltpu.make_async_copy(xs_ref.at[pl.ds(src, CHUNK_ROWS), :],
                                      xbuf.at[sl, s * CHUNK_ROWS:(s + 1) * CHUNK_ROWS, :], sem_in.at[sl]).start()

    def wait_loads(b, sl):
        @pl.when(is_compute(b))
        def _():
            pltpu.make_async_copy(xs_ref.at[0:EXPERT_BLOCK, :], xbuf.at[sl], sem_in.at[sl]).wait()

    def start_stores(b, sl):
        for s in range(EXPERT_BLOCK_CHUNKS):
            dst = pl.multiple_of(dst_ref[b * EXPERT_BLOCK_CHUNKS + s], CHUNK_ROWS)
            pltpu.make_async_copy(ybuf.at[sl, s * CHUNK_ROWS:(s + 1) * CHUNK_ROWS, :],
                                  ys_ref.at[pl.ds(dst, CHUNK_ROWS), :], sem_out.at[sl]).start()

    def wait_stores(sl):
        pltpu.make_async_copy(ybuf.at[sl], ys_ref.at[0:EXPERT_BLOCK, :], sem_out.at[sl]).wait()

    @pl.when(i == 0)
    def _():
        ybuf[...] = jnp.zeros(ybuf.shape, BF16)
        real_rows = ys_ref.shape[0] - 2 * EXPERT_BLOCK
        for par in range(2):
            fill = pltpu.make_async_copy(
                ybuf.at[par], ys_ref.at[real_rows + par * EXPERT_BLOCK:real_rows + (par + 1) * EXPERT_BLOCK, :],
                sem_out.at[par])
            fill.start()
            fill.wait()

        start_loads(0, 0)

    @pl.when(i < nb)
    def _():
        @pl.when(i + 1 < nb)
        def _():
            start_loads(i + 1, 1 - slot)

        wait_loads(i, slot)

        @pl.when(i >= 2)
        def _():
            wait_stores(slot)

        @pl.when(is_compute(i))
        def _():
            @pl.when((i == 0) | (be_ref[i] != be_ref[jnp.maximum(i - 1, 0)]))
            def _():
                wgb[...] = wg_ref[...].astype(BF16)
                wub[...] = wu_ref[...].astype(BF16)
                wdb[...] = wd_ref[...].astype(BF16)

            x = xbuf[slot]
            gate = _dot(x, wgb[...])
            up = _dot(x, wub[...])
            mid = (gate * _sigmoid(gate) * up).astype(BF16)
            ybuf[slot] = _dot(mid, wdb[...]).astype(BF16)

        @pl.when(jnp.logical_not(is_compute(i)))
        def _():
            ybuf[slot] = jnp.zeros((EXPERT_BLOCK, D_MODEL), BF16)

        start_stores(i, slot)

        @pl.when(i == nb - 1)
        def _():
            @pl.when(i >= 1)
            def _():
                wait_stores(1 - slot)

            wait_stores(slot)


def _experts(be, src, dst, nb, xs, wg, wu, wd, layer):
    n_blocks_max = be.shape[0]
    wspec = lambda a: pl.BlockSpec(
        (None,) + a.shape[1:],
        lambda i, be, src, dst, nb: (layer * N_EXPERTS + jnp.minimum(be[i], N_EXPERTS - 1), 0, 0))
    grid_spec = pltpu.PrefetchScalarGridSpec(
        num_scalar_prefetch=4,
        grid=(n_blocks_max,),
        in_specs=[pl.BlockSpec(memory_space=pl.ANY), wspec(wg), wspec(wu), wspec(wd)],
        out_specs=pl.BlockSpec(memory_space=pl.ANY),
        scratch_shapes=[pltpu.VMEM((2, EXPERT_BLOCK, D_MODEL), BF16), pltpu.VMEM((2, EXPERT_BLOCK, D_MODEL), BF16),
                        pltpu.VMEM(wg.shape[1:], BF16), pltpu.VMEM(wu.shape[1:], BF16),
                        pltpu.VMEM(wd.shape[1:], BF16),
                        pltpu.SemaphoreType.DMA((2,)), pltpu.SemaphoreType.DMA((2,))],
    )
    return pl.pallas_call(
        _experts_kernel,
        grid_spec=grid_spec,
        out_shape=jax.ShapeDtypeStruct((xs.shape[0] + 2 * EXPERT_BLOCK, D_MODEL), BF16),
        compiler_params=_cparams(("arbitrary",)),
        name="experts",
    )(be, src, dst, nb, xs, wg, wu, wd)


def _combine_kernel(h_ref, ys_ref, meta_ref, p_ref, pg_ref, gw_ref, pw_ref, fg_ref, o_ref, *, final):
    tm = h_ref.shape[0]
    rr = ys_ref.shape[0]
    meta = meta_ref[...]
    d0 = meta[:, 0:1].astype(I32)
    d1 = meta[:, 1:2].astype(I32)
    r_iota = lax.broadcasted_iota(I32, (tm, rr), 1)
    unperm = (jnp.where(r_iota == d0, meta[:, 2:3], 0.0) + jnp.where(r_iota == d1, meta[:, 3:4], 0.0)).astype(BF16)
    h2 = h_ref[...] + _dot(unperm, ys_ref[...])
    gate = _sigmoid(_dot(_rms(h2, pg_ref[...]).astype(BF16), gw_ref[...]))
    h3 = h2 + gate * _dot(p_ref[...].astype(BF16), pw_ref[...])
    if final:
        h3 = _rms(h3, fg_ref[...])
    o_ref[...] = h3


def _combine(h1, ys, meta, p, layer, pg, gw, pw, fg, tm, final):
    t = h1.shape[0]
    nt = t // tm
    rr = _dispatch_rows(tm)
    row = lambda n: pl.BlockSpec((tm, n), lambda i: (i, 0))
    full = lambda a: pl.BlockSpec(a.shape, lambda i: (0,) * a.ndim)
    return pl.pallas_call(
        functools.partial(_combine_kernel, final=final),
        grid=(nt,),
        in_specs=[row(D_MODEL), pl.BlockSpec((rr, D_MODEL), lambda i: (i, 0)), row(LANES),
                  pl.BlockSpec((tm, P_DIM), lambda i: (layer * nt + i, 0)),
                  full(pg), full(gw), full(pw), full(fg)],
        out_specs=row(D_MODEL),
        out_shape=jax.ShapeDtypeStruct((t, D_MODEL), F32),
        compiler_params=_cparams(("parallel",)),
        name="combine",
    )(h1, ys, meta, p, pg, gw, pw, fg)


def _pad_heads(a, axis):
    shape = a.shape
    a = a.reshape(shape[:axis] + (ML_HEADS, ML_HEAD_DIM) + shape[axis + 1:])
    pad = [(0, 0)] * a.ndim
    pad[axis + 1] = (0, LANES - ML_HEAD_DIM)
    a = jnp.pad(a, pad)
    return a.reshape(shape[:axis] + (ML_PAD,) + shape[axis + 1:])


def _layer_params(i, w_in, conv_w, conv_b, igate_b, fgate_b, mnorm_g, sgu_b, w_out, router_gw, router_gb,
                  router_ew, router_eb):
    w = w_in[i]
    s, m = SB_WIDTH, ML_WIDTH
    a_q, a_k, a_v = w[:, 0:s] * (SB_HEAD_DIM ** -0.5 * LOG2_E), w[:, s:2 * s], w[:, 2 * s:3 * s]
    o = 3 * s
    b_q, b_k, b_v, b_o = (w[:, o + k * m:o + (k + 1) * m] for k in range(4))
    o = o + 4 * m
    gates = w[:, o:o + 2 * ML_HEADS]
    c_uv = w[:, o + 2 * ML_HEADS:]
    w_r = jnp.concatenate([a_q, a_k, a_v, _pad_heads(b_q, 1), _pad_heads(b_k, 1), _pad_heads(b_v, 1),
                           _pad_heads(b_o, 1), c_uv,
                           jnp.pad(gates, ((0, 0), (0, LANES - 2 * ML_HEADS)))], axis=1).astype(BF16)
    cw = jnp.concatenate([_pad_heads(conv_w[i][:, :m], 1), _pad_heads(conv_w[i][:, m:], 1)], axis=1)
    cb = jnp.concatenate([_pad_heads(conv_b[i][:m], 0), _pad_heads(conv_b[i][m:], 0)])[None, :]
    gb = jnp.pad(jnp.concatenate([igate_b[i], fgate_b[i]]), (0, LANES - 2 * ML_HEADS))[None, :]
    mg = _pad_heads(mnorm_g[i], 0)[None, :]
    sgu_bias = jnp.repeat(sgu_b[i].T, SGU_GROUP_DIM, axis=1)
    wo = w_out[i]
    wa = wo[:s].astype(BF16)
    wb = _pad_heads(wo[s:s + m], 0).astype(BF16)
    wc = wo[s + m:].astype(BF16)
    rw = jnp.concatenate([router_ew[i].T, router_gw[i].T,
                          jnp.zeros((ROUTER_ROWS - N_EXPERTS - N_GROUPS, D_MODEL), F32)], axis=0)
    rwh = rw.astype(BF16)
    rwl = (rw - rwh.astype(F32)).astype(BF16)
    rb = jnp.concatenate([router_eb[i], router_gb[i],
                          jnp.zeros((ROUTER_ROWS - N_EXPERTS - N_GROUPS,), F32)])[:, None]
    return w_r, cw, cb, gb, mg, sgu_bias, wa, wb, wc, rwh, rwl, rb


def kernel(x, p, norm1_g, w_in, conv_w, conv_b, igate_b, fgate_b, mnorm_g, sb_out_g, sgu_ln_g, sgu_ln_b, sgu_w,
           sgu_b, sgu_out_g, w_out, norm2_g, router_gw, router_gb, router_ew, router_eb, w_gate, w_up, w_down,
           ple_norm_g, ple_gate_w, ple_proj_w, final_g, *, tile=512):
    batch, seq, d = x.shape
    depth = w_in.shape[0]
    t = batch * seq
    tm = min(tile, t)
    nt = t // tm
    rr = _dispatch_rows(tm)
    n_blocks_max = (nt * rr // CHUNK_ROWS) // EXPERT_BLOCK_CHUNKS + N_EXPERTS + 1
    h = x.astype(F32).reshape(t, d)
    p2 = p.reshape(depth * t, P_DIM)
    wg_all = w_gate.reshape((depth * N_EXPERTS,) + w_gate.shape[2:])
    wu_all = w_up.reshape((depth * N_EXPERTS,) + w_up.shape[2:])
    wd_all = w_down.reshape((depth * N_EXPERTS,) + w_down.shape[2:])
    for i in range(depth):
        (w_r, cw, cb, gb, mg, sgu_bias, wa, wb, wc, rwh, rwl, rb) = _layer_params(
            i, w_in, conv_w, conv_b, igate_b, fgate_b, mnorm_g, sgu_b, w_out, router_gw, router_gb,
            router_ew, router_eb)
        za, zqk, zvo, yc, zg = _in_proj(h, norm1_g[i][None, :], w_r, sgu_ln_g[i][None, :], sgu_ln_b[i][None, :],
                                        sgu_w[i], sgu_bias, sgu_out_g[i][None, :], min(2 * tm, seq))
        ya = _sb_attn(za, batch, seq).reshape(t, SB_WIDTH)
        yb = _mlstm(zqk, zvo, zg, cw, cb, gb, mg, batch, seq).reshape(t, ML_PAD)
        h1, xs, meta, cnt = _mix_out(h, ya, yb, yc, sb_out_g[i][None, :], wa, wb, wc, norm2_g[i][None, :],
                                     rwh, rwl, rb, tm)
        be, src, dst, nb = _chunk_tables(cnt[:, :, 0].astype(I32), rr, n_blocks_max)
        ys = _experts(be, src, dst, nb, xs, wg_all, wu_all, wd_all, i)
        h = _combine(h1, ys, meta, p2, i, ple_norm_g[i][None, :], ple_gate_w[i].astype(BF16),
                     ple_proj_w[i].astype(BF16), final_g[None, :], tm, i == depth - 1)
    return h.reshape(batch, seq, d).astype(x.dtype)
```

```python
import functools

import jax
import jax.numpy as jnp
import numpy as np
from jax import lax
from jax.experimental import pallas as pl
from jax.experimental.pallas import tpu as pltpu

F32 = jnp.float32
BF16 = jnp.bfloat16
I32 = jnp.int32

D_MODEL = 1024
P_DIM = 256
EPS = 1e-6
LANES = 128
SB_HEAD_DIM = 64
SB_WIDTH = 384
SB_BLOCK = 128
SB_QTILE = 1024
LOG2_E = 1.4426950408889634
ML_HEADS = 4
ML_HEAD_DIM = 96
ML_WIDTH = 384
ML_PAD = ML_HEADS * LANES
ML_CHUNK = 512
CONV_K = 4
SGU_WIDTH = 256
SGU_GROUPS = 4
SGU_GROUP_DIM = 64
SGU_BLOCK = 128
STREAM_CHUNK = 64
N_GROUPS = 4
EXPERTS_PER_GROUP = 8
N_EXPERTS = 32
D_EXPERT = 256
ROUTER_ROWS = 40
CHUNK_ROWS = 16
EXPERT_BLOCK_CHUNKS = 32
EXPERT_BLOCK = CHUNK_ROWS * EXPERT_BLOCK_CHUNKS

COL_A = 0
COL_BQK = 3 * SB_WIDTH
COL_BVO = COL_BQK + 2 * ML_PAD
COL_C = COL_BVO + 2 * ML_PAD
COL_G = COL_C + 2 * SGU_WIDTH
N_Z = COL_G + LANES

VMEM_LIMIT = 56 * 1024 * 1024


def _cparams(sem, vmem=VMEM_LIMIT):
    return pltpu.CompilerParams(dimension_semantics=sem, vmem_limit_bytes=vmem)


def _rms(x, g):
    return x * lax.rsqrt(jnp.mean(x * x, axis=-1, keepdims=True) + EPS) * g


def _sigmoid(x):
    return 0.5 * jnp.tanh(0.5 * x) + 0.5


def _split_bf16(x):
    hi = x.astype(BF16)
    lo = (x - hi.astype(F32)).astype(BF16)
    return hi, lo


def _dot(a, b):
    return jnp.dot(a, b, preferred_element_type=F32)


def _dot_nt(a, b):
    return lax.dot_general(a, b, (((1,), (1,)), ((), ())), preferred_element_type=F32)


def _dot_tn(a, b):
    return lax.dot_general(a, b, (((0,), (0,)), ((), ())), preferred_element_type=F32)


def _dot_split_lhs(x, m):
    hi, lo = _split_bf16(x)
    return _dot(hi, m) + _dot(lo, m)


def _dot_split_rhs(m, x):
    hi, lo = _split_bf16(x)
    return _dot(m, hi) + _dot(m, lo)


def _spatial_gating(zc, lg, lb, w_ref, bias, og):
    W = SGU_WIDTH
    gi = lax.broadcasted_iota(I32, (W, W), 0) // SGU_GROUP_DIM
    gj = lax.broadcasted_iota(I32, (W, W), 1) // SGU_GROUP_DIM
    avg = jnp.where(gi == gj, 1.0 / SGU_GROUP_DIM, 0.0).astype(BF16)
    ti = lax.broadcasted_iota(I32, (SGU_BLOCK, SGU_BLOCK), 0) // STREAM_CHUNK
    si = lax.broadcasted_iota(I32, (SGU_BLOCK, SGU_BLOCK), 1) // STREAM_CHUNK
    chunk_causal = si <= ti
    lane_group = lax.broadcasted_iota(I32, (SGU_BLOCK, W), 1) // SGU_GROUP_DIM
    wg = [jnp.where(chunk_causal, w_ref[g], 0.0).astype(BF16) for g in range(SGU_GROUPS)]
    wpair = [jnp.concatenate(wg[g:g + 2], axis=1) for g in range(0, SGU_GROUPS, 2)]
    out = []
    for r in range(zc.shape[0] // SGU_BLOCK):
        rs = slice(r * SGU_BLOCK, (r + 1) * SGU_BLOCK)
        u = jax.nn.gelu(zc[rs, :W])
        v = jax.nn.gelu(zc[rs, W:])
        mu = _dot_split_lhs(v, avg)
        vc = v - mu
        var = _dot_split_lhs(vc * vc, avg)
        vn = (vc * lax.rsqrt(var + EPS) * lg + lb).astype(BF16)
        mixed = bias
        zero = jnp.zeros_like(vn)
        for p, g in enumerate(range(0, SGU_GROUPS, 2)):
            vpair = jnp.concatenate([jnp.where(lane_group == g, vn, zero),
                                     jnp.where(lane_group == g + 1, vn, zero)], axis=0)
            mixed = mixed + _dot(wpair[p], vpair)
        out.append(_rms(u * mixed, og))
    return jnp.concatenate(out, axis=0)


def _in_proj_kernel(x_ref, g_ref, w_ref, lg_ref, lb_ref, sw_ref, sb_ref, og_ref,
                    za_ref, zqk_ref, zvo_ref, yc_ref, zg_ref):
    hn = _rms(x_ref[...], g_ref[...]).astype(BF16)
    za_ref[...] = _dot(hn, w_ref[:, COL_A:COL_BQK]).astype(BF16)
    zqk_ref[...] = _dot(hn, w_ref[:, COL_BQK:COL_BVO])
    zvo_ref[...] = _dot(hn, w_ref[:, COL_BVO:COL_C]).astype(BF16)
    zg_ref[...] = _dot(hn, w_ref[:, COL_G:N_Z])
    zc = _dot(hn, w_ref[:, COL_C:COL_G])
    yc_ref[...] = _spatial_gating(zc, lg_ref[...], lb_ref[...], sw_ref, sb_ref[...], og_ref[...]).astype(BF16)


def _in_proj(h, g, w, lg, lb, sw, sbias, og, tm):
    t = h.shape[0]
    row = lambda n: pl.BlockSpec((tm, n), lambda i: (i, 0))
    full = lambda a: pl.BlockSpec(a.shape, lambda i: (0,) * a.ndim)
    resident = pl.BlockSpec(w.shape, lambda i: (0, 0), pipeline_mode=pl.Buffered(1))
    return pl.pallas_call(
        _in_proj_kernel,
        grid=(t // tm,),
        in_specs=[row(D_MODEL), full(g), resident, full(lg), full(lb), full(sw), full(sbias), full(og)],
        out_specs=[row(3 * SB_WIDTH), row(2 * ML_PAD), row(2 * ML_PAD), row(SGU_WIDTH), row(LANES)],
        out_shape=[jax.ShapeDtypeStruct((t, 3 * SB_WIDTH), BF16),
                   jax.ShapeDtypeStruct((t, 2 * ML_PAD), F32),
                   jax.ShapeDtypeStruct((t, 2 * ML_PAD), BF16),
                   jax.ShapeDtypeStruct((t, SGU_WIDTH), BF16),
                   jax.ShapeDtypeStruct((t, LANES), F32)],
        compiler_params=_cparams(("parallel",)),
        name="in_proj",
    )(h, g, w, lg, lb, sw, sbias, og)


def _sb_kernel(q_ref, k_ref, v_ref, o_ref, acc_ref, car_ref, pre_ref, tot_ref):
    n = pl.program_id(2)
    blk = SB_BLOCK
    tq = q_ref.shape[0]
    sub = tq // blk
    lane_k = lax.broadcasted_iota(I32, (blk, LANES), 1)
    head0 = lane_k < SB_HEAD_DIM
    mj = lax.broadcasted_iota(I32, (2 * blk, 2 * blk), 0)
    ms = lax.broadcasted_iota(I32, (2 * blk, 2 * blk), 1)
    same_head = (mj // blk) == (ms // blk)
    suffix = jnp.where(same_head & (mj > ms), 1.0, 0.0).astype(BF16)
    totals = jnp.where(same_head, 1.0, 0.0).astype(BF16)

    def per_head(x):
        zero = jnp.zeros_like(x)
        return jnp.concatenate([jnp.where(head0, x, zero), jnp.where(head0, zero, x)], axis=0)

    def block_at(ref, j):
        return per_head(ref[pl.ds(pl.multiple_of(j * blk, blk), blk), :])

    def stage1(j, r0, masked, buf):
        z = _dot_nt(q_ref[r0:, :], block_at(k_ref, j))
        sign = jnp.uint32(0x80000000)
        neg_abs = lax.bitcast_convert_type(lax.bitcast_convert_type(z, jnp.uint32) | sign, F32)
        e = jnp.exp2(neg_abs)
        log_sig = jnp.minimum(z, 0.0) - jnp.log2(1.0 + e)
        log1m = log_sig - z
        if masked:
            mask = (lax.broadcasted_iota(I32, (blk, 2 * blk), 1) % blk
                    < lax.broadcasted_iota(I32, (blk, 2 * blk), 0))
            top_l = jnp.where(mask, log1m[:blk], 0.0)
            top_s = jnp.where(mask, log_sig[:blk], -jnp.inf)
            if tq - r0 == blk:
                log1m, log_sig = top_l, top_s
            else:
                log1m = jnp.concatenate([top_l, log1m[blk:]], axis=0)
                log_sig = jnp.concatenate([top_s, log_sig[blk:]], axis=0)
        terms = log1m.astype(BF16)
        pre_ref[buf, r0:, :] = log_sig + _dot(terms, suffix)
        tot_ref[buf, r0:, :] = _dot(terms, totals)

    def stage2(j, r0, buf):
        arg = pre_ref[buf, r0:, :] + car_ref[r0:, :]
        acc_ref[r0:, :] += _dot(jnp.exp2(arg).astype(BF16), block_at(v_ref, j))
        car_ref[r0:, :] += tot_ref[buf, r0:, :]

    acc_ref[...] = jnp.zeros_like(acc_ref)
    car_ref[...] = jnp.zeros_like(car_ref)

    for kj in reversed(range(sub)):
        stage1(n * sub + kj, kj * blk, True, kj)
    for kj in reversed(range(sub)):
        stage2(n * sub + kj, kj * blk, kj)

    top = n * sub - 1

    @pl.when(n > 0)
    def _():
        stage1(top, 0, False, 0)

    unroll = 4

    def body(i, c):
        j0 = top - unroll * i
        for u in range(unroll):
            nxt = j0 - u - 1
            if u == unroll - 1:
                nxt = jnp.maximum(nxt, 0)
            stage1(nxt, 0, False, (u + 1) % 2)
            stage2(j0 - u, 0, u % 2)
        return c

    lax.fori_loop(0, n * (sub // unroll), body, 0)
    o_ref[...] = acc_ref[...].astype(o_ref.dtype)


def _sb_attn(za, batch, seq):
    za3 = za.reshape(batch, seq, 3 * SB_WIDTH)
    pairs = SB_WIDTH // LANES
    tq = min(SB_QTILE, seq)
    return pl.pallas_call(
        _sb_kernel,
        grid=(batch, pairs, seq // tq),
        in_specs=[pl.BlockSpec((None, tq, LANES), lambda b, p, n: (b, n, p)),
                  pl.BlockSpec((None, seq, LANES), lambda b, p, n: (b, 0, pairs + p)),
                  pl.BlockSpec((None, seq, LANES), lambda b, p, n: (b, 0, 2 * pairs + p))],
        out_specs=pl.BlockSpec((None, tq, LANES), lambda b, p, n: (b, n, p)),
        out_shape=jax.ShapeDtypeStruct((batch, seq, SB_WIDTH), BF16),
        scratch_shapes=[pltpu.VMEM((tq, LANES), F32),
                        pltpu.VMEM((tq, 2 * LANES), F32),
                        pltpu.VMEM((tq // SB_BLOCK, tq, 2 * LANES), F32),
                        pltpu.VMEM((tq // SB_BLOCK, tq, 2 * LANES), F32)],
        compiler_params=_cparams(("parallel", "parallel", "arbitrary")),
        name="sb_attn",
    )(za3, za3, za3)


def _mlstm_kernel(qk_ref, vo_ref, g_ref, cw_ref, cb_ref, gb_ref, mg_ref, o_ref,
                  q_s, k_s, vt_s, cum_s, gmb_s, pm_s, gmbt_s, ct_s, m_s):
    seq = qk_ref.shape[0]
    L = ML_CHUNK
    cw = cw_ref[...]
    cb = cb_ref[...]
    rows = lax.broadcasted_iota(I32, (L, 2 * ML_PAD), 0)

    def conv_silu(r0, taps):
        y = taps[0] * cw[CONV_K - 1:CONV_K, :] + cb
        for d in range(1, CONV_K):
            y = y + taps[d] * cw[CONV_K - 1 - d:CONV_K - d, :]
        y = y * _sigmoid(y)
        q_s[pl.ds(r0, L), :] = y[:, :ML_PAD].astype(BF16)
        k_s[pl.ds(r0, L), :] = (y[:, ML_PAD:] * (ML_HEAD_DIM ** -0.5)).astype(BF16)

    ct_s[...] = jnp.zeros_like(ct_s)
    m_s[...] = jnp.zeros_like(m_s)

    ti = lax.broadcasted_iota(I32, (L, L), 0)
    si = lax.broadcasted_iota(I32, (L, L), 1)
    causal = si <= ti
    tril = jnp.where(causal, 1.0, 0.0).astype(BF16)
    lane = lax.broadcasted_iota(I32, (L, LANES), 1)
    row = lax.broadcasted_iota(I32, (L, LANES), 0)
    ones2 = jnp.full((2 * LANES, LANES), 1.0, BF16)
    gb = gb_ref[...]
    mg = mg_ref[...]
    norm_lane = LANES - 1

    def gates(c):
        r0 = pl.multiple_of(c * L, L)
        g = g_ref[pl.ds(r0, L), :] + gb
        logf = jnp.minimum(g, 0.0) - jnp.log1p(jnp.exp(-jnp.abs(g)))
        cum = pltpu.roll(_dot_split_rhs(tril, logf), LANES - ML_HEADS, axis=1)
        gmb = g - cum
        pm = gmb
        step = 1
        while step < L:
            pm = jnp.maximum(pm, jnp.where(row >= step, pltpu.roll(pm, step, axis=0), -jnp.inf))
            step *= 2
        cum_s[pl.ds(r0, L), :] = cum
        gmb_s[pl.ds(r0, L), :] = gmb
        pm_s[pl.ds(r0, L), :] = pm
        gmbt_s[c] = jnp.transpose(gmb)
        for h in range(ML_HEADS):
            v = vo_ref[pl.ds(r0, L), h * LANES:(h + 1) * LANES].astype(F32)
            vt_s[h, c] = jnp.transpose(jnp.where(lane == norm_lane, 1.0, v)).astype(BF16)

    x0 = qk_ref[0:L, :]
    conv_silu(0, [x0] + [jnp.where(rows >= d, pltpu.roll(x0, d, axis=0), 0.0) for d in range(1, CONV_K)])

    def conv_chunk(c, carry):
        r0 = pl.multiple_of(c * L, L)
        xe = qk_ref[pl.ds(r0 - 8, L + 8), :]
        conv_silu(r0, [xe[8 - d:8 - d + L, :] for d in range(CONV_K)])
        return carry

    lax.fori_loop(1, seq // L, conv_chunk, 0)

    gate_unroll = 4 if (seq // L) % 4 == 0 else 1

    def gate_chunk(c, carry):
        for u in range(gate_unroll):
            gates(gate_unroll * c + u)
        return carry

    lax.fori_loop(0, seq // (gate_unroll * L), gate_chunk, 0)

    heads = range(ML_HEADS)

    def chunk(c, carry):
        r0 = pl.multiple_of(c * L, L)
        cum = cum_s[pl.ds(r0, L), :]
        gmb = gmb_s[pl.ds(r0, L), :]
        pm = pm_s[pl.ds(r0, L), :]
        gmb_t = gmbt_s[c]
        qc = [q_s[pl.ds(r0, L), h * LANES:(h + 1) * LANES] for h in heads]
        kc = [k_s[pl.ds(r0, L), h * LANES:(h + 1) * LANES] for h in heads]
        vt = [vt_s[h, c] for h in heads]
        m_in = [m_s[h][0:1, 0:1] for h in heads]
        ct = [ct_s[h] for h in heads]
        s_raw = [_dot_nt(qc[h], kc[h]) for h in heads]
        mm = [jnp.maximum(m_in[h], pm[:, h:h + 1]) for h in heads]
        inter = [jnp.exp(m_in[h] - mm[h]) for h in heads]
        num = []
        for h in heads:
            wd = jnp.where(causal, jnp.exp(gmb_t[h:h + 1, :] - mm[h]), 0.0)
            lhs = jnp.concatenate([(s_raw[h] * wd).astype(BF16),
                                   (qc[h].astype(F32) * inter[h]).astype(BF16)], axis=1)
            rhs_t = jnp.concatenate([vt[h], ct[h].astype(BF16)], axis=1)
            num.append(_dot_nt(lhs, rhs_t))
        hh = []
        ms = []
        for h in heads:
            den = jnp.maximum(jnp.abs(num[h][:, norm_lane:]), jnp.exp(-(cum[:, h:h + 1] + mm[h])))
            x = jnp.where(lane == norm_lane, 0.0, num[h] * (1.0 / den))
            hi, lo = _split_bf16(x * x)
            hh.append(x)
            ms.append(_dot(jnp.concatenate([hi, lo], axis=1), ones2) * (1.0 / ML_HEAD_DIM))
        for h in heads:
            cs = slice(h * LANES, (h + 1) * LANES)
            oc = vo_ref[pl.ds(r0, L), ML_PAD + h * LANES:ML_PAD + (h + 1) * LANES].astype(F32)
            o_ref[pl.ds(r0, L), cs] = (hh[h] * lax.rsqrt(ms[h] + EPS) * mg[:, cs]
                                       * _sigmoid(oc)).astype(o_ref.dtype)
        for h in heads:
            mm_last = mm[h][L - 1:L, :]
            kw = (kc[h].astype(F32) * jnp.exp(gmb[:, h:h + 1] - mm_last)).astype(BF16)
            ct_s[h] = inter[h][L - 1:L, :] * ct[h] + _dot(vt[h], kw)
            m_s[h] = jnp.broadcast_to(cum[L - 1:L, h:h + 1] + mm_last, (8, LANES))
        return carry

    lax.fori_loop(0, seq // L, chunk, 0)


def _mlstm(zqk, zvo, zg, cw, cb, gb, mg, batch, seq):
    full = lambda a: pl.BlockSpec(a.shape, lambda b: (0,) * a.ndim)
    seq_blk = lambda n: pl.BlockSpec((None, seq, n), lambda b: (b, 0, 0))
    return pl.pallas_call(
        _mlstm_kernel,
        grid=(batch,),
        in_specs=[seq_blk(2 * ML_PAD), seq_blk(2 * ML_PAD), seq_blk(LANES),
                  full(cw), full(cb), full(gb), full(mg)],
        out_specs=seq_blk(ML_PAD),
        out_shape=jax.ShapeDtypeStruct((batch, seq, ML_PAD), BF16),
        scratch_shapes=[pltpu.VMEM((seq, ML_PAD), BF16), pltpu.VMEM((seq, ML_PAD), BF16),
                        pltpu.VMEM((ML_HEADS, seq // ML_CHUNK, LANES, ML_CHUNK), BF16),
                        pltpu.VMEM((seq, LANES), F32), pltpu.VMEM((seq, LANES), F32),
                        pltpu.VMEM((seq, LANES), F32),
                        pltpu.VMEM((seq // ML_CHUNK, LANES, ML_CHUNK), F32),
                        pltpu.VMEM((ML_HEADS, LANES, LANES), F32),
                        pltpu.VMEM((ML_HEADS, 8, LANES), F32)],
        compiler_params=_cparams(("parallel",)),
        name="mlstm",
    )(zqk.reshape(batch, seq, 2 * ML_PAD), zvo.reshape(batch, seq, 2 * ML_PAD),
      zg.reshape(batch, seq, LANES), cw, cb, gb, mg)


def _dispatch_rows(tm):
    return 2 * tm + N_EXPERTS * (CHUNK_ROWS - 1) + (N_EXPERTS * (CHUNK_ROWS - 1)) % CHUNK_ROWS


def _mix_out_kernel(h_ref, ya_ref, yb_ref, yc_ref, ag_ref, wa_ref, wb_ref, wc_ref, n2_ref,
                    rwh_ref, rwl_ref, rb_ref, h1_ref, xs_ref, meta_ref, cnt_ref):
    tm = h_ref.shape[0]
    rr = xs_ref.shape[0]
    ya = _rms(ya_ref[...].astype(F32), ag_ref[...]).astype(BF16)
    h1 = h_ref[...] + _dot(ya, wa_ref[...]) + _dot(yb_ref[...], wb_ref[...]) + _dot(yc_ref[...], wc_ref[...])
    h1_ref[...] = h1
    xn = _rms(h1, n2_ref[...])
    x_hi, x_lo = _split_bf16(xn)
    by_hi = _dot_nt(jnp.concatenate([rwh_ref[...], rwl_ref[...]], axis=0), x_hi)
    logits = by_hi[:ROUTER_ROWS] + by_hi[ROUTER_ROWS:] + _dot_nt(rwh_ref[...], x_lo) + rb_ref[...]
    e_log = logits[:N_EXPERTS]
    g_log = logits[N_EXPERTS:N_EXPERTS + N_GROUPS]
    g_iota = lax.broadcasted_iota(I32, (N_GROUPS, tm), 0)
    g_max = jnp.max(g_log, axis=0, keepdims=True)
    g_top = jnp.min(jnp.where(g_log == g_max, g_iota, N_GROUPS), axis=0, keepdims=True)
    g_w = 1.0 / jnp.sum(jnp.exp(g_log - g_max), axis=0, keepdims=True)
    e_iota = lax.broadcasted_iota(I32, (N_EXPERTS, tm), 0)
    val = jnp.where(e_iota // EXPERTS_PER_GROUP == g_top, e_log, -jnp.inf)
    m1 = jnp.max(val, axis=0, keepdims=True)
    i1 = jnp.min(jnp.where(val == m1, e_iota, N_EXPERTS), axis=0, keepdims=True)
    val2 = jnp.where(e_iota == i1, -jnp.inf, val)
    m2 = jnp.max(val2, axis=0, keepdims=True)
    i2 = jnp.min(jnp.where(val2 == m2, e_iota, N_EXPERTS), axis=0, keepdims=True)
    e2 = jnp.exp(m2 - m1)
    w0 = g_w / (1.0 + e2)
    w1 = g_w * e2 / (1.0 + e2)
    oh0 = e_iota == i1
    oh1 = e_iota == i2
    oh = jnp.concatenate([jnp.where(oh0, 1.0, 0.0), jnp.where(oh1, 1.0, 0.0)], axis=0).astype(BF16)
    ta = lax.broadcasted_iota(I32, (tm, tm), 0)
    tb = lax.broadcasted_iota(I32, (tm, tm), 1)
    before = jnp.where(ta < tb, 1.0, 0.0).astype(BF16)
    rank = _dot(oh, before)
    c0 = jnp.sum(jnp.where(oh0, 1.0, 0.0), axis=1, keepdims=True)
    c1 = jnp.sum(jnp.where(oh1, 1.0, 0.0), axis=1, keepdims=True)
    nchunk = jnp.floor((c0 + c1 + (CHUNK_ROWS - 1)) * (1.0 / CHUNK_ROWS))
    nchunk_b = jnp.broadcast_to(nchunk, (N_EXPERTS, LANES))
    ea = lax.broadcasted_iota(I32, (N_EXPERTS, N_EXPERTS), 0)
    eb = lax.broadcasted_iota(I32, (N_EXPERTS, N_EXPERTS), 1)
    lower = jnp.where(eb < ea, 1.0, 0.0).astype(BF16)
    loc = CHUNK_ROWS * _dot(lower, nchunk_b.astype(BF16))[:, 0:1]
    dest0 = jnp.sum(jnp.where(oh0, loc + rank[:N_EXPERTS], 0.0), axis=0, keepdims=True)
    dest1 = jnp.sum(jnp.where(oh1, loc + c0 + rank[N_EXPERTS:], 0.0), axis=0, keepdims=True)
    r_iota = lax.broadcasted_iota(I32, (rr, tm), 0)
    d0 = dest0.astype(I32)
    d1 = dest1.astype(I32)
    perm = jnp.where(r_iota == d0, 1.0, jnp.where(r_iota == d1, 1.0, 0.0)).astype(BF16)
    xs_ref[...] = _dot(perm, x_hi).astype(BF16)
    cnt_ref[...] = nchunk_b
    m_iota = lax.broadcasted_iota(I32, (8, tm), 0)
    meta = jnp.where(m_iota == 0, dest0, jnp.where(m_iota == 1, dest1, jnp.where(m_iota == 2, w0,
                     jnp.where(m_iota == 3, w1, 0.0))))
    meta = jnp.concatenate([meta, jnp.zeros((LANES - 8, tm), F32)], axis=0)
    meta_ref[...] = jnp.transpose(meta)


def _mix_out(h, ya, yb, yc, ag, wa, wb, wc, n2, rwh, rwl, rb, tm):
    t = h.shape[0]
    nt = t // tm
    rr = _dispatch_rows(tm)
    row = lambda n: pl.BlockSpec((tm, n), lambda i: (i, 0))
    full = lambda a: pl.BlockSpec(a.shape, lambda i: (0,) * a.ndim)
    return pl.pallas_call(
        _mix_out_kernel,
        grid=(nt,),
        in_specs=[row(D_MODEL), row(SB_WIDTH), row(ML_PAD), row(SGU_WIDTH), full(ag), full(wa), full(wb),
                  full(wc), full(n2), full(rwh), full(rwl), full(rb)],
        out_specs=[row(D_MODEL), pl.BlockSpec((rr, D_MODEL), lambda i: (i, 0)), row(LANES),
                   pl.BlockSpec((None, N_EXPERTS, LANES), lambda i: (i, 0, 0))],
        out_shape=[jax.ShapeDtypeStruct((t, D_MODEL), F32),
                   jax.ShapeDtypeStruct((nt * rr, D_MODEL), BF16),
                   jax.ShapeDtypeStruct((t, LANES), F32),
                   jax.ShapeDtypeStruct((nt, N_EXPERTS, LANES), F32)],
        compiler_params=_cparams(("parallel",)),
        name="mix_out",
    )(h, ya, yb, yc, ag, wa, wb, wc, n2, rwh, rwl, rb)


def _chunk_tables(cnt, rr, n_blocks_max):
    nt = cnt.shape[0]
    ne = N_EXPERTS + 1
    tail = rr // CHUNK_ROWS - jnp.sum(cnt, axis=1)
    cnt = jnp.concatenate([cnt, tail[:, None]], axis=1)
    loc = (jnp.cumsum(cnt, axis=1) - cnt) * CHUNK_ROWS
    cnt_e = cnt.T
    cum_e = jnp.cumsum(cnt_e, axis=1)
    total = cum_e[:, -1]
    nblk = (total + EXPERT_BLOCK_CHUNKS - 1) // EXPERT_BLOCK_CHUNKS
    bend = jnp.cumsum(nblk)
    bstart = bend - nblk
    n_blocks = bend[-1]
    i = jnp.arange(n_blocks_max, dtype=I32)
    be = jnp.minimum(jnp.sum(bend[None, :] <= i[:, None], axis=1), ne - 1).astype(I32)
    oh_b = be[:, None] == jnp.arange(ne, dtype=I32)[None, :]
    pick = lambda v: jnp.sum(jnp.where(oh_b, v[None, :], 0), axis=1)
    pick2 = lambda m: jnp.sum(jnp.where(oh_b[:, :, None], m[None, :, :], 0), axis=1)
    q = (i - pick(bstart))[:, None] * EXPERT_BLOCK_CHUNKS + jnp.arange(EXPERT_BLOCK_CHUNKS, dtype=I32)[None, :]
    valid = (q < pick(total)[:, None]) & (i < n_blocks)[:, None]
    cum_b = pick2(cum_e)
    base = jnp.arange(nt, dtype=I32)[None, :] * rr + loc.T - (cum_e - cnt_e) * CHUNK_ROWS
    base_b = pick2(base)
    j = jnp.minimum(jnp.sum(cum_b[:, None, :] <= q[:, :, None], axis=-1), nt - 1)
    oh_j = j[:, :, None] == jnp.arange(nt, dtype=I32)[None, None, :]
    row = jnp.sum(jnp.where(oh_j, base_b[:, None, :], 0), axis=-1) + q * CHUNK_ROWS
    spare = nt * rr + ((i % 2)[:, None] * EXPERT_BLOCK_CHUNKS
                       + jnp.arange(EXPERT_BLOCK_CHUNKS, dtype=I32)[None, :]) * CHUNK_ROWS
    src = jnp.where(valid, row, 0).astype(I32)
    dst = jnp.where(valid, row, spare).astype(I32)
    return be, src.reshape(-1), dst.reshape(-1), n_blocks.reshape(1).astype(I32)


def _experts_kernel(be_ref, src_ref, dst_ref, nb_ref, xs_ref, wg_ref, wu_ref, wd_ref, ys_ref,
                    xbuf, ybuf, wgb, wub, wdb, sem_in, sem_out):
    i = pl.program_id(0)
    nb = nb_ref[0]
    slot = lax.rem(i, 2)

    def is_compute(b):
        return be_ref[b] < N_EXPERTS

    def start_loads(b, sl):
        @pl.when(is_compute(b))
        def _():
            for s in range(EXPERT_BLOCK_CHUNKS):
                src = pl.multiple_of(src_ref[b * EXPERT_BLOCK_CHUNKS + s], CHUNK_ROWS)
                pltpu.make_async_copy(xs_ref.at[pl.ds(src, CHUNK_ROWS), :],
                                      xbuf.at[sl, s * CHUNK_ROWS:(s + 1) * CHUNK_ROWS, :], sem_in.at[sl]).start()

    def wait_loads(b, sl):
        @pl.when(is_compute(b))
        def _():
            pltpu.make_async_copy(xs_ref.at[0:EXPERT_BLOCK, :], xbuf.at[sl], sem_in.at[sl]).wait()

    def start_stores(b, sl):
        for s in range(EXPERT_BLOCK_CHUNKS):
            dst = pl.multiple_of(dst_ref[b * EXPERT_BLOCK_CHUNKS + s], CHUNK_ROWS)
            pltpu.make_async_copy(ybuf.at[sl, s * CHUNK_ROWS:(s + 1) * CHUNK_ROWS, :],
                                  ys_ref.at[pl.ds(dst, CHUNK_ROWS), :], sem_out.at[sl]).start()

    def wait_stores(sl):
        pltpu.make_async_copy(ybuf.at[sl], ys_ref.at[0:EXPERT_BLOCK, :], sem_out.at[sl]).wait()

    @pl.when(i == 0)
    def _():
        ybuf[...] = jnp.zeros(ybuf.shape, BF16)
        real_rows = ys_ref.shape[0] - 2 * EXPERT_BLOCK
        for par in range(2):
            fill = pltpu.make_async_copy(
                ybuf.at[par], ys_ref.at[real_rows + par * EXPERT_BLOCK:real_rows + (par + 1) * EXPERT_BLOCK, :],
                sem_out.at[par])
            fill.start()
            fill.wait()

        start_loads(0, 0)

    @pl.when(i < nb)
    def _():
        @pl.when(i + 1 < nb)
        def _():
            start_loads(i + 1, 1 - slot)

        wait_loads(i, slot)

        @pl.when(i >= 2)
        def _():
            wait_stores(slot)

        @pl.when(is_compute(i))
        def _():
            @pl.when((i == 0) | (be_ref[i] != be_ref[jnp.maximum(i - 1, 0)]))
            def _():
                wgb[...] = wg_ref[...].astype(BF16)
                wub[...] = wu_ref[...].astype(BF16)
                wdb[...] = wd_ref[...].astype(BF16)

            x = xbuf[slot]
            gate = _dot(x, wgb[...])
            up = _dot(x, wub[...])
            mid = (gate * _sigmoid(gate) * up).astype(BF16)
            ybuf[slot] = _dot(mid, wdb[...]).astype(BF16)

        @pl.when(jnp.logical_not(is_compute(i)))
        def _():
            ybuf[slot] = jnp.zeros((EXPERT_BLOCK, D_MODEL), BF16)

        start_stores(i, slot)

        @pl.when(i == nb - 1)
        def _():
            @pl.when(i >= 1)
            def _():
                wait_stores(1 - slot)

            wait_stores(slot)


def _experts(be, src, dst, nb, xs, wg, wu, wd, layer):
    n_blocks_max = be.shape[0]
    wspec = lambda a: pl.BlockSpec(
        (None,) + a.shape[1:],
        lambda i, be, src, dst, nb: (layer * N_EXPERTS + jnp.minimum(be[i], N_EXPERTS - 1), 0, 0))
    grid_spec = pltpu.PrefetchScalarGridSpec(
        num_scalar_prefetch=4,
        grid=(n_blocks_max,),
        in_specs=[pl.BlockSpec(memory_space=pl.ANY), wspec(wg), wspec(wu), wspec(wd)],
        out_specs=pl.BlockSpec(memory_space=pl.ANY),
        scratch_shapes=[pltpu.VMEM((2, EXPERT_BLOCK, D_MODEL), BF16), pltpu.VMEM((2, EXPERT_BLOCK, D_MODEL), BF16),
                        pltpu.VMEM(wg.shape[1:], BF16), pltpu.VMEM(wu.shape[1:], BF16),
                        pltpu.VMEM(wd.shape[1:], BF16),
                        pltpu.SemaphoreType.DMA((2,)), pltpu.SemaphoreType.DMA((2,))],
    )
    return pl.pallas_call(
        _experts_kernel,
        grid_spec=grid_spec,
        out_shape=jax.ShapeDtypeStruct((xs.shape[0] + 2 * EXPERT_BLOCK, D_MODEL), BF16),
        compiler_params=_cparams(("arbitrary",)),
        name="experts",
    )(be, src, dst, nb, xs, wg, wu, wd)


def _combine_kernel(h_ref, ys_ref, meta_ref, p_ref, pg_ref, gw_ref, pw_ref, fg_ref, o_ref, *, final):
    tm = h_ref.shape[0]
    rr = ys_ref.shape[0]
    meta = meta_ref[...]
    d0 = meta[:, 0:1].astype(I32)
    d1 = meta[:, 1:2].astype(I32)
    r_iota = lax.broadcasted_iota(I32, (tm, rr), 1)
    unperm = (jnp.where(r_iota == d0, meta[:, 2:3], 0.0) + jnp.where(r_iota == d1, meta[:, 3:4], 0.0)).astype(BF16)
    h2 = h_ref[...] + _dot(unperm, ys_ref[...])
    gate = _sigmoid(_dot(_rms(h2, pg_ref[...]).astype(BF16), gw_ref[...]))
    h3 = h2 + gate * _dot(p_ref[...].astype(BF16), pw_ref[...])
    if final:
        h3 = _rms(h3, fg_ref[...])
    o_ref[...] = h3


def _combine(h1, ys, meta, p, layer, pg, gw, pw, fg, tm, final):
    t = h1.shape[0]
    nt = t // tm
    rr = _dispatch_rows(tm)
    row = lambda n: pl.BlockSpec((tm, n), lambda i: (i, 0))
    full = lambda a: pl.BlockSpec(a.shape, lambda i: (0,) * a.ndim)
    return pl.pallas_call(
        functools.partial(_combine_kernel, final=final),
        grid=(nt,),
        in_specs=[row(D_MODEL), pl.BlockSpec((rr, D_MODEL), lambda i: (i, 0)), row(LANES),
                  pl.BlockSpec((tm, P_DIM), lambda i: (layer * nt + i, 0)),
                  full(pg), full(gw), full(pw), full(fg)],
        out_specs=row(D_MODEL),
        out_shape=jax.ShapeDtypeStruct((t, D_MODEL), F32),
        compiler_params=_cparams(("parallel",)),
        name="combine",
    )(h1, ys, meta, p, pg, gw, pw, fg)


def _pad_heads(a, axis):
    shape = a.shape
    a = a.reshape(shape[:axis] + (ML_HEADS, ML_HEAD_DIM) + shape[axis + 1:])
    pad = [(0, 0)] * a.ndim
    pad[axis + 1] = (0, LANES - ML_HEAD_DIM)
    a = jnp.pad(a, pad)
    return a.reshape(shape[:axis] + (ML_PAD,) + shape[axis + 1:])


def _layer_params(i, w_in, conv_w, conv_b, igate_b, fgate_b, mnorm_g, sgu_b, w_out, router_gw, router_gb,
                  router_ew, router_eb):
    w = w_in[i]
    s, m = SB_WIDTH, ML_WIDTH
    a_q, a_k, a_v = w[:, 0:s] * (SB_HEAD_DIM ** -0.5 * LOG2_E), w[:, s:2 * s], w[:, 2 * s:3 * s]
    o = 3 * s
    b_q, b_k, b_v, b_o = (w[:, o + k * m:o + (k + 1) * m] for k in range(4))
    o = o + 4 * m
    gates = w[:, o:o + 2 * ML_HEADS]
    c_uv = w[:, o + 2 * ML_HEADS:]
    w_r = jnp.concatenate([a_q, a_k, a_v, _pad_heads(b_q, 1), _pad_heads(b_k, 1), _pad_heads(b_v, 1),
                           _pad_heads(b_o, 1), c_uv,
                           jnp.pad(gates, ((0, 0), (0, LANES - 2 * ML_HEADS)))], axis=1).astype(BF16)
    cw = jnp.concatenate([_pad_heads(conv_w[i][:, :m], 1), _pad_heads(conv_w[i][:, m:], 1)], axis=1)
    cb = jnp.concatenate([_pad_heads(conv_b[i][:m], 0), _pad_heads(conv_b[i][m:], 0)])[None, :]
    gb = jnp.pad(jnp.concatenate([igate_b[i], fgate_b[i]]), (0, LANES - 2 * ML_HEADS))[None, :]
    mg = _pad_heads(mnorm_g[i], 0)[None, :]
    sgu_bias = jnp.repeat(sgu_b[i].T, SGU_GROUP_DIM, axis=1)
    wo = w_out[i]
    wa = wo[:s].astype(BF16)
    wb = _pad_heads(wo[s:s + m], 0).astype(BF16)
    wc = wo[s + m:].astype(BF16)
    rw = jnp.concatenate([router_ew[i].T, router_gw[i].T,
                          jnp.zeros((ROUTER_ROWS - N_EXPERTS - N_GROUPS, D_MODEL), F32)], axis=0)
    rwh = rw.astype(BF16)
    rwl = (rw - rwh.astype(F32)).astype(BF16)
    rb = jnp.concatenate([router_eb[i], router_gb[i],
                          jnp.zeros((ROUTER_ROWS - N_EXPERTS - N_GROUPS,), F32)])[:, None]
    return w_r, cw, cb, gb, mg, sgu_bias, wa, wb, wc, rwh, rwl, rb


def kernel(x, p, norm1_g, w_in, conv_w, conv_b, igate_b, fgate_b, mnorm_g, sb_out_g, sgu_ln_g, sgu_ln_b, sgu_w,
           sgu_b, sgu_out_g, w_out, norm2_g, router_gw, router_gb, router_ew, router_eb, w_gate, w_up, w_down,
           ple_norm_g, ple_gate_w, ple_proj_w, final_g, *, tile=512):
    batch, seq, d = x.shape
    depth = w_in.shape[0]
    t = batch * seq
    tm = min(tile, t)
    nt = t // tm
    rr = _dispatch_rows(tm)
    n_blocks_max = (nt * rr // CHUNK_ROWS) // EXPERT_BLOCK_CHUNKS + N_EXPERTS + 1
    h = x.astype(F32).reshape(t, d)
    p2 = p.reshape(depth * t, P_DIM)
    wg_all = w_gate.reshape((depth * N_EXPERTS,) + w_gate.shape[2:])
    wu_all = w_up.reshape((depth * N_EXPERTS,) + w_up.shape[2:])
    wd_all = w_down.reshape((depth * N_EXPERTS,) + w_down.shape[2:])
    for i in range(depth):
        (w_r, cw, cb, gb, mg, sgu_bias, wa, wb, wc, rwh, rwl, rb) = _layer_params(
            i, w_in, conv_w, conv_b, igate_b, fgate_b, mnorm_g, sgu_b, w_out, router_gw, router_gb,
            router_ew, router_eb)
        za, zqk, zvo, yc, zg = _in_proj(h, norm1_g[i][None, :], w_r, sgu_ln_g[i][None, :], sgu_ln_b[i][None, :],
                                        sgu_w[i], sgu_bias, sgu_out_g[i][None, :], min(2 * tm, seq))
        ya = _sb_attn(za, batch, seq).reshape(t, SB_WIDTH)
        yb = _mlstm(zqk, zvo, zg, cw, cb, gb, mg, batch, seq).reshape(t, ML_PAD)
        h1, xs, meta, cnt = _mix_out(h, ya, yb, yc, sb_out_g[i][None, :], wa, wb, wc, norm2_g[i][None, :],
                                     rwh, rwl, rb, tm)
        be, src, dst, nb = _chunk_tables(cnt[:, :, 0].astype(I32), rr, n_blocks_max)
        ys = _experts(be, src, dst, nb, xs, wg_all, wu_all, wd_all, i)
        h = _combine(h1, ys, meta, p2, i, ple_norm_g[i][None, :], ple_gate_w[i].astype(BF16),
                     ple_proj_w[i].astype(BF16), final_g[None, :], tm, i == depth - 1)
    return h.reshape(batch, seq, d).astype(x.dtype)
```
